```python
import math
import jax
import jax.numpy as jnp
from jax import lax
import numpy as np

D_MODEL = 2048
BATCH = 2
SEQ = 4096
DEPTH = 4
DEC_BATCH = 8
DEC_SEQ = 8
PAST_LEN = 16384
PAGE_SIZE = 128

HEAD_DIM = 128
MEM_HEADS = 4
MEM_WIDTH = MEM_HEADS * HEAD_DIM
N_MEM = 256
MIX_WIDTH = D_MODEL - MEM_WIDTH
N_A_LAYERS = DEPTH // 2
N_B_LAYERS = DEPTH - N_A_LAYERS
CONV_CH = MIX_WIDTH
CONV_WIDTH = 31
B_GROUPS = ((128, 1), (512, 4), (2048, 16))
N_GROUPS = len(B_GROUPS)
B_HEADS_PER_GROUP = MIX_WIDTH // (N_GROUPS * HEAD_DIM)
B_HEADS = N_GROUPS * B_HEADS_PER_GROUP
B_OUT = B_HEADS_PER_GROUP * HEAD_DIM
REL_BUCKETS = 32
REL_MAX_DIST = max(w for w, _ in B_GROUPS)
Q_BLOCK = 128
LN_EPS = 1e-5
ALPHA = (2 * DEPTH) ** 0.25
BETA = (8 * DEPTH) ** -0.25
NEG_INF = -1e30
A_IN_WIDTH = 3 * CONV_CH + 2 * MEM_WIDTH
B_IN_WIDTH = B_HEADS * HEAD_DIM + B_OUT + 2 * MEM_WIDTH

kernel_name = 'yoco_conformer_dilated_attn_decoder_step'


def _layernorm(x, g, b):
    xf = x.astype(jnp.float32)
    mu = jnp.mean(xf, -1, keepdims=True)
    var = jnp.mean(jnp.square(xf - mu), -1, keepdims=True)
    y = (xf - mu) * lax.rsqrt(var + LN_EPS)
    return (y * g.astype(jnp.float32) + b.astype(jnp.float32)).astype(x.dtype)


def _t5_bucket(dist):
    max_exact = REL_BUCKETS // 2
    safe = jnp.maximum(dist, 1).astype(jnp.float32)
    large = max_exact + (jnp.log(safe / max_exact) / math.log(REL_MAX_DIST / max_exact)
                         * (REL_BUCKETS - max_exact)).astype(jnp.int32)
    large = jnp.minimum(large, REL_BUCKETS - 1)
    return jnp.where(dist < max_exact, dist, large)


def _group_biases(rel_bias):
    out = []
    for g, (w, d) in enumerate(B_GROUPS):
        dist = d * jnp.arange(w // d + 1, dtype=jnp.int32)
        tab = rel_bias[_t5_bucket(dist)]
        out.append(tab[:, g * B_HEADS_PER_GROUP:(g + 1) * B_HEADS_PER_GROUP].T.astype(jnp.float32))
    return out


def _dilated_mixture(qs, kvs, q_idxs, biases):
    ms, ss, nums = [], [], []
    for g, (w, d) in enumerate(B_GROUPS):
        j = jnp.arange(w // d + 1, dtype=jnp.int32)
        idx = q_idxs[g][:, None] - d * j[None, :]
        valid = idx >= 0
        kv = kvs[g][:, jnp.maximum(idx, 0)]
        k = kv[:, :, :, 0].astype(jnp.float32)
        v = kv[:, :, :, 1].astype(jnp.float32)
        logits = jnp.einsum('bqhd,bqjhd->bqhj', qs[g].astype(jnp.float32), k) * (HEAD_DIM ** -0.5) + biases[g]
        logits = jnp.where(valid[None, :, None, :], logits, NEG_INF)
        m = jnp.max(logits, -1, keepdims=True)
        e = jnp.exp(logits - m)
        ms.append(m[..., 0])
        ss.append(jnp.sum(e, -1))
        nums.append(jnp.einsum('bqhj,bqjhd->bqhd', e, v))
    m_all = jnp.max(jnp.stack(ms), 0)
    coef = [jnp.exp(m - m_all) for m in ms]
    num = coef[0][..., None] * nums[0]
    den = coef[0] * ss[0]
    for g in range(1, N_GROUPS):
        num = num + coef[g][..., None] * nums[g]
        den = den + coef[g] * ss[g]
    return num / den[..., None]


def _prompt_attend(kvs, biases, seq_len):
    def attend(qs):
        def block(start):
            q_idx = start + jnp.arange(Q_BLOCK, dtype=jnp.int32)
            qb = [lax.dynamic_slice_in_dim(q, start, Q_BLOCK, axis=1) for q in qs]
            return _dilated_mixture(qb, kvs, [q_idx] * N_GROUPS, biases)
        starts = jnp.arange(seq_len // Q_BLOCK, dtype=jnp.int32) * Q_BLOCK
        out = lax.map(block, starts)
        return jnp.moveaxis(out, 0, 1).reshape(out.shape[1], seq_len, B_HEADS_PER_GROUP, HEAD_DIM)
    return attend


def _sample_attend(kvs, biases, n_new):
    q_idxs = [kv.shape[1] - n_new + jnp.arange(n_new, dtype=jnp.int32) for kv in kvs]
    def attend(qs):
        return _dilated_mixture(qs, kvs, q_idxs, biases)
    return attend


def _mem_attn(qm, mem_kv):
    b, t, _ = qm.shape
    q = qm.reshape(b, t, MEM_HEADS, HEAD_DIM).astype(jnp.float32)
    k = mem_kv[:, :, 0].astype(jnp.float32)
    v = mem_kv[:, :, 1].astype(jnp.float32)
    p = jax.nn.softmax(jnp.einsum('bthd,bnhd->bhtn', q, k) * (HEAD_DIM ** -0.5), axis=-1)
    return jnp.einsum('bhtn,bnhd->bthd', p, v).reshape(b, t, MEM_WIDTH).astype(qm.dtype)


def _a_layer(x, conv_prev, mem_kv, w_in, w_dw, b_dw, cn_g, cn_b, w_out, ln_g, ln_b):
    h = jnp.einsum('btd,de->bte', x, w_in)
    ga, gb, gate, qm, gm = jnp.split(h, [CONV_CH, 2 * CONV_CH, 3 * CONV_CH, 3 * CONV_CH + MEM_WIDTH], axis=-1)
    u = ga * jax.nn.sigmoid(gb)
    ext = jnp.concatenate([conv_prev.astype(u.dtype), u], axis=1)
    conv = lax.conv_general_dilated(ext, w_dw[:, None, :].astype(ext.dtype), (1,), 'VALID',
                                    dimension_numbers=('NWC', 'WIO', 'NWC'),
                                    feature_group_count=CONV_CH) + b_dw
    c = jax.nn.silu(_layernorm(conv, cn_g, cn_b)) * jax.nn.silu(gate)
    m = _mem_attn(qm, mem_kv) * jax.nn.silu(gm)
    y = jnp.einsum('bte,ed->btd', jnp.concatenate([c, m], axis=-1), w_out)
    new_conv = ext[:, ext.shape[1] - (CONV_WIDTH - 1):]
    return _layernorm(ALPHA * x + y, ln_g, ln_b), new_conv


def _b_layer(x, attend, mem_kv, w_in, w_out, ln_g, ln_b):
    b, t, _ = x.shape
    h = jnp.einsum('btd,de->bte', x, w_in)
    q, gate, qm, gm = jnp.split(h, [MIX_WIDTH, MIX_WIDTH + B_OUT, MIX_WIDTH + B_OUT + MEM_WIDTH], axis=-1)
    q = q.reshape(b, t, N_GROUPS, B_HEADS_PER_GROUP, HEAD_DIM)
    o = attend([q[:, :, g] for g in range(N_GROUPS)])
    o = o.astype(x.dtype).reshape(b, t, B_OUT) * jax.nn.silu(gate)
    m = _mem_attn(qm, mem_kv) * jax.nn.silu(gm)
    y = jnp.einsum('bte,ed->btd', jnp.concatenate([o, m], axis=-1), w_out)
    return _layernorm(ALPHA * x + y, ln_g, ln_b)


def _trunk(x, conv_prev, mem_kv, kv_bufs, a_w_in, a_w_dw, a_b_dw, a_cn_g, a_cn_b, a_w_out,
           b_w_in, b_w_out, w_kv_shared, rel_bias, ln_g, ln_b):
    b, t, _ = x.shape
    biases = _group_biases(rel_bias)
    new_conv = []
    new_bufs = []
    attend = None
    for l in range(DEPTH):
        if l < N_A_LAYERS:
            x, c = _a_layer(x, conv_prev[l], mem_kv[l], a_w_in[l], a_w_dw[l], a_b_dw[l], a_cn_g[l], a_cn_b[l],
                            a_w_out[l], ln_g[l], ln_b[l])
            new_conv.append(c)
        else:
            if l == N_A_LAYERS:
                kv = jnp.einsum('btd,de->bte', x, w_kv_shared).reshape(b, t, 2, N_GROUPS, B_HEADS_PER_GROUP, HEAD_DIM)
                kv_new = [kv[:, :, :, g] for g in range(N_GROUPS)]
                if kv_bufs is None:
                    new_bufs = [k[:, t - min(w, t):] for k, (w, _) in zip(kv_new, B_GROUPS)]
                    attend = _prompt_attend(kv_new, biases, t)
                else:
                    exts = [jnp.concatenate([buf.astype(k.dtype), k], axis=1) for buf, k in zip(kv_bufs, kv_new)]
                    new_bufs = [e[:, e.shape[1] - buf.shape[1]:] for e, buf in zip(exts, kv_bufs)]
                    attend = _sample_attend(exts, biases, t)
            i = l - N_A_LAYERS
            x = _b_layer(x, attend, mem_kv[l], b_w_in[i], b_w_out[i], ln_g[l], ln_b[l])
    return x, jnp.stack(new_conv), new_bufs


def setup_inputs(seed: int = 0) -> dict:
    key = jax.random.key(seed)
    ks = iter(jax.random.split(key, 32))
    f32 = jnp.float32

    def nrm(shape, scale):
        return jax.random.normal(next(ks), shape, f32) * scale

    kv_scale = jnp.array([1.0, BETA], f32)
    inp = {}
    inp['x_prompt'] = nrm((BATCH, SEQ, D_MODEL), 1.0)
    inp['x_sample'] = nrm((DEC_BATCH, DEC_SEQ, D_MODEL), 1.0)
    inp['state_conv'] = nrm((N_A_LAYERS, DEC_BATCH, CONV_WIDTH - 1, CONV_CH), 0.5)
    for g, (w, _) in enumerate(B_GROUPS):
        inp['state_kv_g%d' % g] = nrm((DEC_BATCH, min(w, PAST_LEN), 2, B_HEADS_PER_GROUP, HEAD_DIM), 1.0) * kv_scale[None, None, :, None, None]
    inp['cache_mem_kv'] = nrm((DEPTH, DEC_BATCH, N_MEM, 2, MEM_HEADS, HEAD_DIM), 1.0) * kv_scale[None, None, None, :, None, None]
    inp['mem_prompt'] = nrm((BATCH, N_MEM, D_MODEL), 1.0)
    inp['a_w_in'] = nrm((N_A_LAYERS, D_MODEL, A_IN_WIDTH), D_MODEL ** -0.5)
    inp['a_w_dw'] = nrm((N_A_LAYERS, CONV_WIDTH, CONV_CH), CONV_WIDTH ** -0.5)
    inp['a_b_dw'] = nrm((N_A_LAYERS, CONV_CH), 0.01)
    inp['a_cn_g'] = 1.0 + nrm((N_A_LAYERS, CONV_CH), 0.01)
    inp['a_cn_b'] = nrm((N_A_LAYERS, CONV_CH), 0.01)
    inp['a_w_out'] = nrm((N_A_LAYERS, CONV_CH + MEM_WIDTH, D_MODEL), (CONV_CH + MEM_WIDTH) ** -0.5 * BETA)
    inp['b_w_in'] = nrm((N_B_LAYERS, D_MODEL, B_IN_WIDTH), D_MODEL ** -0.5)
    inp['b_w_out'] = nrm((N_B_LAYERS, B_OUT + MEM_WIDTH, D_MODEL), (B_OUT + MEM_WIDTH) ** -0.5 * BETA)
    inp['w_kv_shared'] = (nrm((D_MODEL, 2, B_HEADS * HEAD_DIM), D_MODEL ** -0.5) * kv_scale[None, :, None]).reshape(D_MODEL, 2 * B_HEADS * HEAD_DIM)
    inp['w_mem_kv'] = (nrm((DEPTH, D_MODEL, 2, MEM_WIDTH), D_MODEL ** -0.5) * kv_scale[None, None, :, None]).reshape(DEPTH, D_MODEL, 2 * MEM_WIDTH)
    inp['rel_bias'] = nrm((REL_BUCKETS, B_HEADS), 0.5)
    inp['ln_g'] = 1.0 + nrm((DEPTH, D_MODEL), 0.01)
    inp['ln_b'] = nrm((DEPTH, D_MODEL), 0.01)
    return inp


def reference(x_prompt, x_sample, state_conv, state_kv_g0, state_kv_g1, state_kv_g2, cache_mem_kv, mem_prompt,
              a_w_in, a_w_dw, a_b_dw, a_cn_g, a_cn_b, a_w_out, b_w_in, b_w_out, w_kv_shared, w_mem_kv,
              rel_bias, ln_g, ln_b):
    bp = x_prompt.shape[0]
    new_mem_kv_prompt = jnp.einsum('bnd,lde->lbne', mem_prompt, w_mem_kv).reshape(DEPTH, bp, N_MEM, 2, MEM_HEADS, HEAD_DIM)
    conv_zero = jnp.zeros((N_A_LAYERS, bp, CONV_WIDTH - 1, CONV_CH), x_prompt.dtype)
    y_prompt, conv_p, bufs_p = _trunk(x_prompt, conv_zero, new_mem_kv_prompt, None,
                                      a_w_in, a_w_dw, a_b_dw, a_cn_g, a_cn_b, a_w_out,
                                      b_w_in, b_w_out, w_kv_shared, rel_bias, ln_g, ln_b)
    y_sample, conv_s, bufs_s = _trunk(x_sample, state_conv, cache_mem_kv, [state_kv_g0, state_kv_g1, state_kv_g2],
                                      a_w_in, a_w_dw, a_b_dw, a_cn_g, a_cn_b, a_w_out,
                                      b_w_in, b_w_out, w_kv_shared, rel_bias, ln_g, ln_b)
    return (y_prompt, y_sample, conv_p, conv_s, bufs_p[0], bufs_s[0], bufs_p[1], bufs_s[1], bufs_p[2], bufs_s[2], new_mem_kv_prompt)
```

```python
import functools
import math

import jax
import jax.numpy as jnp
from jax.experimental import pallas as pl
from jax.experimental.pallas import tpu as pltpu

D_MODEL = 2048
DEPTH = 4
HEAD_DIM = 128
MEM_HEADS = 4
MEM_WIDTH = MEM_HEADS * HEAD_DIM
N_MEM = 256
MIX_WIDTH = D_MODEL - MEM_WIDTH
N_A_LAYERS = DEPTH // 2
CONV_CH = MIX_WIDTH
CONV_WIDTH = 31
B_GROUPS = ((128, 1), (512, 4), (2048, 16))
N_GROUPS = len(B_GROUPS)
HEADS_PER_GROUP = 4
GROUP_WIDTH = HEADS_PER_GROUP * HEAD_DIM
B_OUT = GROUP_WIDTH
REL_BUCKETS = 32
REL_MAX_DIST = 2048
BAND = 128
LN_EPS = 1e-5
ALPHA = (2 * DEPTH) ** 0.25
NEG_INF = -1e30
SCALE = HEAD_DIM ** -0.5

HALO = 32
VMEM_LIMIT = 56 * 1024 * 1024
BF16 = jnp.bfloat16
F32 = jnp.float32


def _params(*sem):
    return pltpu.CompilerParams(dimension_semantics=sem, vmem_limit_bytes=VMEM_LIMIT)


def _sigmoid(x):
    return 1.0 / (1.0 + jnp.exp(-x))


def _silu(x):
    return x * _sigmoid(x)


def _mm_kernel(x_ref, w_ref, o_ref):
    o_ref[...] = jnp.dot(x_ref[...].astype(BF16), w_ref[...].astype(BF16), preferred_element_type=F32)


def _mm(x, w, *, tn=512):
    m, k = x.shape
    n = w.shape[1]
    tm = min(m, 1024)
    return pl.pallas_call(
        _mm_kernel,
        grid=(m // tm, n // tn),
        in_specs=[pl.BlockSpec((tm, k), lambda i, j: (i, 0)),
                  pl.BlockSpec((k, tn), lambda i, j: (0, j))],
        out_specs=pl.BlockSpec((tm, tn), lambda i, j: (i, j)),
        out_shape=jax.ShapeDtypeStruct((m, n), F32),
        compiler_params=_params("parallel", "arbitrary"),
        name="mm",
    )(x, w)


def _out_ln_kernel(cm_ref, w_ref, x_ref, g_ref, b_ref, o_ref, acc_ref):
    k = pl.program_id(1)

    @pl.when(k == 0)
    def _():
        acc_ref[...] = jnp.zeros_like(acc_ref)

    acc_ref[...] += jnp.dot(cm_ref[...].astype(BF16), w_ref[...].astype(BF16), preferred_element_type=F32)

    @pl.when(k == pl.num_programs(1) - 1)
    def _():
        z = ALPHA * x_ref[...] + acc_ref[...]
        mu = jnp.mean(z, -1, keepdims=True)
        zc = z - mu
        var = jnp.mean(zc * zc, -1, keepdims=True)
        o_ref[...] = zc * jax.lax.rsqrt(var + LN_EPS) * g_ref[...] + b_ref[...]


def _out_ln(cm, w, x, g, b, *, tk=512):
    m, kk = cm.shape
    d = w.shape[1]
    tm = min(m, 512)
    return pl.pallas_call(
        _out_ln_kernel,
        grid=(m // tm, kk // tk),
        in_specs=[pl.BlockSpec((tm, tk), lambda i, k: (i, k)),
                  pl.BlockSpec((tk, d), lambda i, k: (k, 0)),
                  pl.BlockSpec((tm, d), lambda i, k: (i, 0)),
                  pl.BlockSpec((1, d), lambda i, k: (0, 0)),
                  pl.BlockSpec((1, d), lambda i, k: (0, 0))],
        out_specs=pl.BlockSpec((tm, d), lambda i, k: (i, 0)),
        out_shape=jax.ShapeDtypeStruct((m, d), F32),
        scratch_shapes=[pltpu.VMEM((tm, d), F32)],
        compiler_params=_params("parallel", "arbitrary"),
        name="out_ln",
    )(cm, w, x, g.reshape(1, d), b.reshape(1, d))


def _mem_attn(qm, gm, mem):
    outs = []
    for h in range(MEM_HEADS):
        lo = h * HEAD_DIM
        q = qm[:, lo:lo + HEAD_DIM].astype(BF16)
        k = mem[:, lo:lo + HEAD_DIM].astype(BF16)
        v = mem[:, MEM_WIDTH + lo:MEM_WIDTH + lo + HEAD_DIM].astype(BF16)
        s = jax.lax.dot_general(q, k, (((1,), (1,)), ((), ())), preferred_element_type=F32) * SCALE
        s = s - jnp.max(s, -1, keepdims=True)
        e = jnp.exp(s)
        p = e / jnp.sum(e, -1, keepdims=True)
        outs.append(jnp.dot(p.astype(BF16), v, preferred_element_type=F32))
    return jnp.concatenate(outs, -1) * _silu(gm)


def _a_mix_kernel(*refs, tt, tr, has_halo):
    if has_halo:
        (ga, gb, gate, qm, gm, gap, gbp, cprev, mem, wdw, bdw, cng, cnb,
         cm_out, nc_out, ext, sh, conv) = refs
    else:
        (ga, gb, gate, qm, gm, cprev, mem, wdw, bdw, cng, cnb,
         cm_out, nc_out, ext, sh, conv) = refs
    i = pl.program_id(1)

    u = ga[0] * _sigmoid(gb[0])
    if has_halo:
        prev_u = gap[0] * _sigmoid(gbp[0])
        halo = jnp.where(i == 0, cprev[0], prev_u)
    else:
        halo = cprev[0]
    ext[0:HALO, :] = halo
    ext[HALO:HALO + tt, :] = u

    nc_out[0] = ext[HALO + tt - (CONV_WIDTH - 1):HALO + tt, :]

    n_sh = tt + HALO - 8
    sh[0, :, :] = ext[...]
    for s in range(1, 8):
        sh[s, 0:n_sh, :] = ext[s:s + n_sh, :]

    off0 = HALO - (CONV_WIDTH - 1)
    lane_chunk = 512
    for c in range(CONV_CH // lane_chunk):
        cl = c * lane_chunk

        def body(rc, carry, cl=cl):
            r0 = pl.multiple_of(rc * tr, 8)
            acc = jnp.zeros((tr, lane_chunk), F32)
            for k in range(CONV_WIDTH):
                o = k + off0
                a, s = divmod(o, 8)
                win = sh[s, pl.ds(pl.multiple_of(r0 + 8 * a, 8), tr), cl:cl + lane_chunk]
                acc = acc + win * wdw[k:k + 1, cl:cl + lane_chunk]
            conv[pl.ds(r0, tr), cl:cl + lane_chunk] = acc + bdw[:, cl:cl + lane_chunk]
            return carry

        jax.lax.fori_loop(0, tt // tr, body, 0)

    z = conv[...]
    mu = jnp.mean(z, -1, keepdims=True)
    zc = z - mu
    var = jnp.mean(zc * zc, -1, keepdims=True)
    zn = zc * jax.lax.rsqrt(var + LN_EPS) * cng[...] + cnb[...]
    cm_out[0, :, 0:CONV_CH] = _silu(zn) * _silu(gate[0])
    cm_out[0, :, CONV_CH:D_MODEL] = _mem_attn(qm[0], gm[0], mem[0])


def _a_mix(h, conv_prev, mem, w_dw, b_dw, cn_g, cn_b):
    b, t, _ = h.shape
    tt = min(t, 256)
    tr = min(tt, 32)
    nt = t // tt
    has_halo = nt > 1
    cprev = jnp.pad(conv_prev, ((0, 0), (HALO - (CONV_WIDTH - 1), 0), (0, 0)))
    c3 = CONV_CH
    qcol = 3 * CONV_CH // MEM_WIDTH

    in_specs = [pl.BlockSpec((1, tt, c3), lambda bi, i: (bi, i, 0)),
                pl.BlockSpec((1, tt, c3), lambda bi, i: (bi, i, 1)),
                pl.BlockSpec((1, tt, c3), lambda bi, i: (bi, i, 2)),
                pl.BlockSpec((1, tt, MEM_WIDTH), lambda bi, i: (bi, i, qcol)),
                pl.BlockSpec((1, tt, MEM_WIDTH), lambda bi, i: (bi, i, qcol + 1))]
    args = [h, h, h, h, h]
    if has_halo:
        per = tt // HALO
        in_specs += [pl.BlockSpec((1, HALO, c3), lambda bi, i: (bi, jnp.maximum(i * per - 1, 0), 0)),
                     pl.BlockSpec((1, HALO, c3), lambda bi, i: (bi, jnp.maximum(i * per - 1, 0), 1))]
        args += [h, h]
    in_specs += [pl.BlockSpec((1, HALO, c3), lambda bi, i: (bi, 0, 0)),
                 pl.BlockSpec((1, N_MEM, 2 * MEM_WIDTH), lambda bi, i: (bi, 0, 0)),
                 pl.BlockSpec((CONV_WIDTH, c3), lambda bi, i: (0, 0)),
                 pl.BlockSpec((1, c3), lambda bi, i: (0, 0)),
                 pl.BlockSpec((1, c3), lambda bi, i: (0, 0)),
                 pl.BlockSpec((1, c3), lambda bi, i: (0, 0))]
    args += [cprev, mem, w_dw, b_dw.reshape(1, c3), cn_g.reshape(1, c3), cn_b.reshape(1, c3)]

    return pl.pallas_call(
        functools.partial(_a_mix_kernel, tt=tt, tr=tr, has_halo=has_halo),
        grid=(b, nt),
        in_specs=in_specs,
        out_specs=[pl.BlockSpec((1, tt, D_MODEL), lambda bi, i: (bi, i, 0)),
                   pl.BlockSpec((1, CONV_WIDTH - 1, c3), lambda bi, i: (bi, 0, 0))],
        out_shape=[jax.ShapeDtypeStruct((b, t, D_MODEL), F32),
                   jax.ShapeDtypeStruct((b, CONV_WIDTH - 1, c3), F32)],
        scratch_shapes=[pltpu.VMEM((tt + HALO, c3), F32),
                        pltpu.VMEM((8, tt + HALO, c3), F32),
                        pltpu.VMEM((tt, c3), F32)],
        compiler_params=_params("parallel", "arbitrary"),
        name="a_mix",
    )(*args)


def _head_stats(ms, ss, rows):
    lane = jax.lax.broadcasted_iota(jnp.int32, (rows, HEAD_DIM), 1)
    st = jnp.zeros((rows, HEAD_DIM), F32)
    for h in range(HEADS_PER_GROUP):
        st = jnp.where(lane == h, ms[h], st)
        st = jnp.where(lane == HEADS_PER_GROUP + h, ss[h], st)
    return st


def _band_attn_kernel(q_ref, kp_ref, kc_ref, vp_ref, vc_ref, bias_ref, num_ref, st_ref):
    ms, ss = [], []
    for h in range(HEADS_PER_GROUP):
        lo = h * HEAD_DIM
        q = q_ref[0, :, lo:lo + HEAD_DIM].astype(BF16)
        k = jnp.concatenate([kp_ref[0, :, lo:lo + HEAD_DIM], kc_ref[0, :, lo:lo + HEAD_DIM]], 0).astype(BF16)
        v = jnp.concatenate([vp_ref[0, :, lo:lo + HEAD_DIM], vc_ref[0, :, lo:lo + HEAD_DIM]], 0).astype(BF16)
        logits = jax.lax.dot_general(q, k, (((1,), (1,)), ((), ())), preferred_element_type=F32) * SCALE
        logits = logits + bias_ref[0, h]
        m = jnp.max(logits, -1, keepdims=True)
        e = jnp.exp(logits - m)
        ms.append(m)
        ss.append(jnp.sum(e, -1, keepdims=True))
        num_ref[0, :, lo:lo + HEAD_DIM] = jnp.dot(e.astype(BF16), v, preferred_element_type=F32)
    st_ref[0] = _head_stats(ms, ss, BAND)


def _band_attn(q, k, v, qcol, kcol, vcol, biasmat):
    bb, length, _ = q.shape
    nq = length // BAND
    cur = lambda col: (lambda b, i: (b, i, col))
    prev = lambda col: (lambda b, i: (b, jnp.maximum(i - 1, 0), col))
    blk = (1, BAND, GROUP_WIDTH)
    return pl.pallas_call(
        _band_attn_kernel,
        grid=(bb, nq),
        in_specs=[pl.BlockSpec(blk, cur(qcol)),
                  pl.BlockSpec(blk, prev(kcol)), pl.BlockSpec(blk, cur(kcol)),
                  pl.BlockSpec(blk, prev(vcol)), pl.BlockSpec(blk, cur(vcol)),
                  pl.BlockSpec((1, HEADS_PER_GROUP, BAND, 2 * BAND), lambda b, i: (jnp.minimum(i, 1), 0, 0, 0))],
        out_specs=[pl.BlockSpec(blk, lambda b, i: (b, i, 0)),
                   pl.BlockSpec((1, BAND, HEAD_DIM), lambda b, i: (b, i, 0))],
        out_shape=[jax.ShapeDtypeStruct((bb, length, GROUP_WIDTH), F32),
                   jax.ShapeDtypeStruct((bb, length, HEAD_DIM), F32)],
        compiler_params=_params("parallel", "arbitrary"),
        name="band_attn",
    )(q, k, k, v, v, biasmat)


def _samp_attn_kernel(q_ref, st_ref, kn_ref, vn_ref, bs_ref, bn_ref, num_ref, stat_ref, *, nq):
    ms, ss = [], []
    for h in range(HEADS_PER_GROUP):
        lo = h * HEAD_DIM
        q = q_ref[0, :, lo:lo + HEAD_DIM].astype(BF16)
        ks = st_ref[0, :, lo:lo + HEAD_DIM].astype(BF16)
        vs = st_ref[0, :, GROUP_WIDTH + lo:GROUP_WIDTH + lo + HEAD_DIM].astype(BF16)
        kn = kn_ref[0, :, lo:lo + HEAD_DIM].astype(BF16)
        vn = vn_ref[0, :, lo:lo + HEAD_DIM].astype(BF16)
        dn = (((1,), (1,)), ((), ()))
        ls = jax.lax.dot_general(q, ks, dn, preferred_element_type=F32) * SCALE + bs_ref[h]
        ln = jax.lax.dot_general(q, kn, dn, preferred_element_type=F32) * SCALE + bn_ref[h]
        m = jnp.maximum(jnp.max(ls, -1, keepdims=True), jnp.max(ln, -1, keepdims=True))
        es = jnp.exp(ls - m)
        en = jnp.exp(ln - m)
        ms.append(m)
        ss.append(jnp.sum(es, -1, keepdims=True) + jnp.sum(en, -1, keepdims=True))
        num_ref[0, :, lo:lo + HEAD_DIM] = (jnp.dot(es.astype(BF16), vs, preferred_element_type=F32)
                                           + jnp.dot(en.astype(BF16), vn, preferred_element_type=F32))
    stat_ref[0] = _head_stats(ms, ss, nq)


def _samp_attn(h, qcol, state, kv, kcol, vcol, bias_state, bias_new):
    b, nq, _ = h.shape
    w = state.shape[1]
    blk = (1, nq, GROUP_WIDTH)
    return pl.pallas_call(
        functools.partial(_samp_attn_kernel, nq=nq),
        grid=(b,),
        in_specs=[pl.BlockSpec(blk, lambda i: (i, 0, qcol)),
                  pl.BlockSpec((1, w, 2 * GROUP_WIDTH), lambda i: (i, 0, 0)),
                  pl.BlockSpec(blk, lambda i: (i, 0, kcol)),
                  pl.BlockSpec(blk, lambda i: (i, 0, vcol)),
                  pl.BlockSpec((HEADS_PER_GROUP, nq, w), lambda i: (0, 0, 0)),
                  pl.BlockSpec((HEADS_PER_GROUP, nq, nq), lambda i: (0, 0, 0))],
        out_specs=[pl.BlockSpec(blk, lambda i: (i, 0, 0)),
                   pl.BlockSpec((1, nq, HEAD_DIM), lambda i: (i, 0, 0))],
        out_shape=[jax.ShapeDtypeStruct((b, nq, GROUP_WIDTH), F32),
                   jax.ShapeDtypeStruct((b, nq, HEAD_DIM), F32)],
        compiler_params=_params("parallel"),
        name="samp_attn",
    )(h, state, kv, kv, bias_state, bias_new)


def _b_mix_kernel(n0, s0, n1, s1, n2, s2, gate, qm, gm, mem, om_out):
    nums = [n0, n1, n2]
    stats = [s0[0], s1[0], s2[0]]
    outs = []
    for h in range(HEADS_PER_GROUP):
        lo = h * HEAD_DIM
        ms = [st[:, h:h + 1] for st in stats]
        ss = [st[:, HEADS_PER_GROUP + h:HEADS_PER_GROUP + h + 1] for st in stats]
        m_all = jnp.maximum(jnp.maximum(ms[0], ms[1]), ms[2])
        coef = [jnp.exp(m - m_all) for m in ms]
        num = coef[0] * nums[0][0, :, lo:lo + HEAD_DIM]
        den = coef[0] * ss[0]
        for g in range(1, N_GROUPS):
            num = num + coef[g] * nums[g][0, :, lo:lo + HEAD_DIM]
            den = den + coef[g] * ss[g]
        outs.append(num / den)
    om_out[0, :, 0:B_OUT] = jnp.concatenate(outs, -1) * _silu(gate[0])
    om_out[0, :, B_OUT:B_OUT + MEM_WIDTH] = _mem_attn(qm[0], gm[0], mem[0])


def _b_mix(nums, stats, h, mem):
    b, t, _ = h.shape
    tt = min(t, 256)
    gcol = MIX_WIDTH // GROUP_WIDTH
    nspec = pl.BlockSpec((1, tt, GROUP_WIDTH), lambda bi, i: (bi, i, 0))
    sspec = pl.BlockSpec((1, tt, HEAD_DIM), lambda bi, i: (bi, i, 0))
    hspec = lambda col: pl.BlockSpec((1, tt, GROUP_WIDTH), lambda bi, i: (bi, i, col))
    return pl.pallas_call(
        _b_mix_kernel,
        grid=(b, t // tt),
        in_specs=[nspec, sspec, nspec, sspec, nspec, sspec,
                  hspec(gcol), hspec(gcol + 1), hspec(gcol + 2),
                  pl.BlockSpec((1, N_MEM, 2 * MEM_WIDTH), lambda bi, i: (bi, 0, 0))],
        out_specs=pl.BlockSpec((1, tt, B_OUT + MEM_WIDTH), lambda bi, i: (bi, i, 0)),
        out_shape=jax.ShapeDtypeStruct((b, t, B_OUT + MEM_WIDTH), F32),
        compiler_params=_params("parallel", "arbitrary"),
        name="b_mix",
    )(nums[0], stats[0], nums[1], stats[1], nums[2], stats[2], h, h, h, mem)


def _t5_bucket(dist):
    max_exact = REL_BUCKETS // 2
    safe = jnp.maximum(dist, 1).astype(F32)
    large = max_exact + (jnp.log(safe / max_exact) / math.log(REL_MAX_DIST / max_exact)
                         * (REL_BUCKETS - max_exact)).astype(jnp.int32)
    large = jnp.minimum(large, REL_BUCKETS - 1)
    return jnp.where(dist < max_exact, dist, large)


def _group_bias(rel_bias, g):
    w, d = B_GROUPS[g]
    dist = d * jnp.arange(w // d + 1, dtype=jnp.int32)
    tab = rel_bias[_t5_bucket(dist)]
    return tab[:, g * HEADS_PER_GROUP:(g + 1) * HEADS_PER_GROUP].T.astype(F32)


def _band_bias(bias):
    r = jnp.arange(BAND, dtype=jnp.int32)[:, None]
    c = jnp.arange(2 * BAND, dtype=jnp.int32)[None, :]
    j = BAND + r - c
    ok = (j >= 0) & (j <= BAND)
    mat = jnp.where(ok[None], bias[:, jnp.clip(j, 0, BAND)], NEG_INF)
    first = jnp.where((c >= BAND)[None], mat, NEG_INF)
    return jnp.stack([first, mat])


def _sample_bias(bias, w, d, nq):
    n = jnp.arange(nq, dtype=jnp.int32)[:, None]
    c = jnp.arange(w, dtype=jnp.int32)[None, :]
    diff = w + n - c
    ok = (diff % d == 0) & (diff <= w)
    bs = jnp.where(ok[None], bias[:, jnp.clip(diff // d, 0, w // d)], NEG_INF)
    c2 = jnp.arange(nq, dtype=jnp.int32)[None, :]
    diff2 = n - c2
    ok2 = (diff2 % d == 0) & (diff2 >= 0)
    bn = jnp.where(ok2[None], bias[:, jnp.clip(diff2 // d, 0, w // d)], NEG_INF)
    return bs, bn


def _split_residues(a, d):
    b, t, c = a.shape
    return a.reshape(b, t // d, d, c).transpose(0, 2, 1, 3).reshape(b * d, t // d, c)


def _merge_residues(a, b, d):
    bd, l, c = a.shape
    return a.reshape(b, d, l, c).transpose(0, 2, 1, 3).reshape(b, l * d, c)


def _trunk(x, conv_prev, mem_kv, kv_state, a_w_in, a_w_dw, a_b_dw, a_cn_g, a_cn_b, a_w_out,
           b_w_in, b_w_out, w_kv_shared, rel_bias, ln_g, ln_b):
    b, t, d_model = x.shape
    m = b * t
    x2 = x.reshape(m, d_model)
    new_conv = []
    for l in range(N_A_LAYERS):
        h = _mm(x2, a_w_in[l]).reshape(b, t, -1)
        cm, nc = _a_mix(h, conv_prev[l], mem_kv[l], a_w_dw[l], a_b_dw[l], a_cn_g[l], a_cn_b[l])
        new_conv.append(nc)
        x2 = _out_ln(cm.reshape(m, d_model), a_w_out[l], x2, ln_g[l], ln_b[l])

    kv = _mm(x2, w_kv_shared).reshape(b, t, 2 * MIX_WIDTH)
    vcol0 = MIX_WIDTH // GROUP_WIDTH
    biases = [_group_bias(rel_bias, g) for g in range(N_GROUPS)]
    kv_new = [jnp.concatenate([kv[:, :, g * GROUP_WIDTH:(g + 1) * GROUP_WIDTH],
                               kv[:, :, MIX_WIDTH + g * GROUP_WIDTH:MIX_WIDTH + (g + 1) * GROUP_WIDTH]], -1)
              for g in range(N_GROUPS)]
    if kv_state is None:
        new_bufs = [kn[:, t - min(w, t):] for kn, (w, _) in zip(kv_new, B_GROUPS)]
        band_bias = [_band_bias(bi) for bi in biases]
        k_split = [None] + [_split_residues(kv[:, :, g * GROUP_WIDTH:(g + 1) * GROUP_WIDTH], B_GROUPS[g][1])
                            for g in range(1, N_GROUPS)]
        v_split = [None] + [_split_residues(kv[:, :, MIX_WIDTH + g * GROUP_WIDTH:MIX_WIDTH + (g + 1) * GROUP_WIDTH],
                                            B_GROUPS[g][1]) for g in range(1, N_GROUPS)]
    else:
        new_bufs = [jnp.concatenate([st[:, t:], kn], 1) for st, kn in zip(kv_state, kv_new)]
        samp_bias = [_sample_bias(bi, w, d, t) for bi, (w, d) in zip(biases, B_GROUPS)]

    for i in range(DEPTH - N_A_LAYERS):
        l = N_A_LAYERS + i
        h = _mm(x2, b_w_in[i]).reshape(b, t, -1)
        nums, stats = [], []
        for g, (w, d) in enumerate(B_GROUPS):
            if kv_state is None:
                if d == 1:
                    num, st = _band_attn(h, kv, kv, g, g, vcol0 + g, band_bias[g])
                else:
                    qg = _split_residues(h[:, :, g * GROUP_WIDTH:(g + 1) * GROUP_WIDTH], d)
                    num, st = _band_attn(qg, k_split[g], v_split[g], 0, 0, 0, band_bias[g])
                    num = _merge_residues(num, b, d)
                    st = _merge_residues(st, b, d)
            else:
                num, st = _samp_attn(h, g, kv_state[g], kv, g, vcol0 + g, *samp_bias[g])
            nums.append(num)
            stats.append(st)
        om = _b_mix(nums, stats, h, mem_kv[l])
        x2 = _out_ln(om.reshape(m, B_OUT + MEM_WIDTH), b_w_out[i], x2, ln_g[l], ln_b[l])
    return x2.reshape(b, t, d_model), jnp.stack(new_conv), new_bufs


def kernel(x_prompt, x_sample, state_conv, state_kv_g0, state_kv_g1, state_kv_g2, cache_mem_kv, mem_prompt,
           a_w_in, a_w_dw, a_b_dw, a_cn_g, a_cn_b, a_w_out, b_w_in, b_w_out, w_kv_shared, w_mem_kv,
           rel_bias, ln_g, ln_b):
    bp = x_prompt.shape[0]
    bs = x_sample.shape[0]
    memp2 = mem_prompt.reshape(bp * N_MEM, D_MODEL)
    mem_p = jnp.stack([_mm(memp2, w_mem_kv[l]) for l in range(DEPTH)]).reshape(DEPTH, bp, N_MEM, 2 * MEM_WIDTH)
    conv_zero = jnp.zeros((N_A_LAYERS, bp, CONV_WIDTH - 1, CONV_CH), x_prompt.dtype)
    weights = (a_w_in, a_w_dw, a_b_dw, a_cn_g, a_cn_b, a_w_out, b_w_in, b_w_out, w_kv_shared, rel_bias, ln_g, ln_b)

    y_p, conv_p, bufs_p = _trunk(x_prompt, conv_zero, mem_p, None, *weights)
    kv_state = [s.reshape(bs, s.shape[1], 2 * GROUP_WIDTH) for s in (state_kv_g0, state_kv_g1, state_kv_g2)]
    mem_s = cache_mem_kv.reshape(DEPTH, bs, N_MEM, 2 * MEM_WIDTH)
    y_s, conv_s, bufs_s = _trunk(x_sample, state_conv, mem_s, kv_state, *weights)

    kv5 = lambda a: a.reshape(a.shape[0], a.shape[1], 2, HEADS_PER_GROUP, HEAD_DIM)
    return (y_p, y_s, conv_p, conv_s,
            kv5(bufs_p[0]), kv5(bufs_s[0]), kv5(bufs_p[1]), kv5(bufs_s[1]), kv5(bufs_p[2]), kv5(bufs_s[2]),
            mem_p.reshape(DEPTH, bp, N_MEM, 2, MEM_HEADS, HEAD_DIM))
```

```python
import functools
import math

import jax
import jax.numpy as jnp
from jax.experimental import pallas as pl
from jax.experimental.pallas import tpu as pltpu

D_MODEL = 2048
DEPTH = 4
HEAD_DIM = 128
MEM_HEADS = 4
MEM_WIDTH = MEM_HEADS * HEAD_DIM
N_MEM = 256
MIX_WIDTH = D_MODEL - MEM_WIDTH
N_A_LAYERS = DEPTH // 2
CONV_CH = MIX_WIDTH
CONV_WIDTH = 31
B_GROUPS = ((128, 1), (512, 4), (2048, 16))
N_GROUPS = len(B_GROUPS)
HEADS_PER_GROUP = 4
B_HEADS = N_GROUPS * HEADS_PER_GROUP
GROUP_WIDTH = HEADS_PER_GROUP * HEAD_DIM
B_OUT = GROUP_WIDTH
REL_BUCKETS = 32
REL_MAX_DIST = 2048
BAND = 128
ATTN_TOKENS = BAND * max(d for _, d in B_GROUPS)
LN_EPS = 1e-5
ALPHA = (2 * DEPTH) ** 0.25
NEG_INF = -1e30
SCALE = HEAD_DIM ** -0.5

HALO = 32
VMEM_LIMIT = 56 * 1024 * 1024
BF16 = jnp.bfloat16
F32 = jnp.float32


def _params(*sem):
    return pltpu.CompilerParams(dimension_semantics=sem, vmem_limit_bytes=VMEM_LIMIT)


def _sigmoid(x):
    return 1.0 / (1.0 + jnp.exp(-x))


def _silu(x):
    return x * _sigmoid(x)


def _layernorm(z, g, b):
    mu = jnp.mean(z, -1, keepdims=True)
    zc = z - mu
    var = jnp.mean(zc * zc, -1, keepdims=True)
    return zc * jax.lax.rsqrt(var + LN_EPS) * g + b


def _mm_kernel(x_ref, w_ref, o_ref, xb_ref):
    @pl.when(pl.program_id(1) == 0)
    def _():
        xb_ref[...] = x_ref[...].astype(BF16)

    o_ref[...] = jnp.dot(xb_ref[...], w_ref[...].astype(BF16), preferred_element_type=F32)


def _mm(x, w, *, tn=512):
    m, k = x.shape
    n = w.shape[1]
    tm = min(m, 1024)
    return pl.pallas_call(
        _mm_kernel,
        grid=(m // tm, n // tn),
        in_specs=[pl.BlockSpec((tm, k), lambda i, j: (i, 0)),
                  pl.BlockSpec((k, tn), lambda i, j: (0, j))],
        out_specs=pl.BlockSpec((tm, tn), lambda i, j: (i, j)),
        out_shape=jax.ShapeDtypeStruct((m, n), F32),
        scratch_shapes=[pltpu.VMEM((tm, k), BF16)],
        compiler_params=_params("arbitrary", "arbitrary"),
        name="mm",
    )(x, w)


def _out_ln_kernel(cm_ref, w_ref, x_ref, g_ref, b_ref, o_ref, acc_ref):
    k = pl.program_id(1)

    @pl.when(k == 0)
    def _():
        acc_ref[...] = jnp.zeros_like(acc_ref)

    acc_ref[...] += jnp.dot(cm_ref[...].astype(BF16), w_ref[...].astype(BF16), preferred_element_type=F32)

    @pl.when(k == pl.num_programs(1) - 1)
    def _():
        o_ref[...] = _layernorm(ALPHA * x_ref[...] + acc_ref[...], g_ref[...], b_ref[...])


def _out_ln(cm, w, x, g, b, *, tk=512):
    m, kk = cm.shape
    d = w.shape[1]
    tm = min(m, 512)
    return pl.pallas_call(
        _out_ln_kernel,
        grid=(m // tm, kk // tk),
        in_specs=[pl.BlockSpec((tm, tk), lambda i, k: (i, k)),
                  pl.BlockSpec((tk, d), lambda i, k: (k, 0)),
                  pl.BlockSpec((tm, d), lambda i, k: (i, 0)),
                  pl.BlockSpec((1, d), lambda i, k: (0, 0)),
                  pl.BlockSpec((1, d), lambda i, k: (0, 0))],
        out_specs=pl.BlockSpec((tm, d), lambda i, k: (i, 0)),
        out_shape=jax.ShapeDtypeStruct((m, d), F32),
        scratch_shapes=[pltpu.VMEM((tm, d), F32)],
        compiler_params=_params("parallel", "arbitrary"),
        name="out_ln",
    )(cm, w, x, g.reshape(1, d), b.reshape(1, d))


def _mem_attn(qm, gm, mem):
    outs = []
    for h in range(MEM_HEADS):
        lo = h * HEAD_DIM
        q = qm[:, lo:lo + HEAD_DIM].astype(BF16)
        k = mem[:, lo:lo + HEAD_DIM].astype(BF16)
        v = mem[:, MEM_WIDTH + lo:MEM_WIDTH + lo + HEAD_DIM].astype(BF16)
        s = jax.lax.dot_general(q, k, (((1,), (1,)), ((), ())), preferred_element_type=F32) * SCALE
        s = s - jnp.max(s, -1, keepdims=True)
        e = jnp.exp(s)
        p = e / jnp.sum(e, -1, keepdims=True)
        outs.append(jnp.dot(p.astype(BF16), v, preferred_element_type=F32))
    return jnp.concatenate(outs, -1) * _silu(gm)


def _a_mix_kernel(*refs, tt, tr, has_halo):
    if has_halo:
        (ga, gb, gate, qm, gm, gap, gbp, cprev, mem, wdw, bdw, cng, cnb,
         cm_out, nc_out, ext, sh, conv) = refs
    else:
        (ga, gb, gate, qm, gm, cprev, mem, wdw, bdw, cng, cnb,
         cm_out, nc_out, ext, sh, conv) = refs
    i = pl.program_id(1)

    u = ga[0] * _sigmoid(gb[0])
    if has_halo:
        prev_u = gap[0] * _sigmoid(gbp[0])
        halo = jnp.where(i == 0, cprev[0], prev_u)
    else:
        halo = cprev[0]
    ext[0:HALO, :] = halo
    ext[HALO:HALO + tt, :] = u

    nc_out[0] = ext[HALO + tt - (CONV_WIDTH - 1):HALO + tt, :]

    n_sh = tt + HALO - 8
    sh[0, :, :] = ext[...]
    for s in range(1, 8):
        sh[s, 0:n_sh, :] = ext[s:s + n_sh, :]

    off0 = HALO - (CONV_WIDTH - 1)
    lane_chunk = 512
    for c in range(CONV_CH // lane_chunk):
        cl = c * lane_chunk

        def body(rc, carry, cl=cl):
            r0 = pl.multiple_of(rc * tr, 8)
            acc = jnp.zeros((tr, lane_chunk), F32)
            for k in range(CONV_WIDTH):
                o = k + off0
                a, s = divmod(o, 8)
                win = sh[s, pl.ds(pl.multiple_of(r0 + 8 * a, 8), tr), cl:cl + lane_chunk]
                acc = acc + win * wdw[k:k + 1, cl:cl + lane_chunk]
            conv[pl.ds(r0, tr), cl:cl + lane_chunk] = acc + bdw[:, cl:cl + lane_chunk]
            return carry

        jax.lax.fori_loop(0, tt // tr, body, 0)

    zn = _layernorm(conv[...], cng[...], cnb[...])
    cm_out[0, :, 0:CONV_CH] = _silu(zn) * _silu(gate[0])
    cm_out[0, :, CONV_CH:D_MODEL] = _mem_attn(qm[0], gm[0], mem[0])


def _a_mix(h, conv_prev, mem, w_dw, b_dw, cn_g, cn_b):
    b, t, _ = h.shape
    tt = min(t, 256)
    tr = min(tt, 32)
    nt = t // tt
    has_halo = nt > 1
    cprev = jnp.pad(conv_prev, ((0, 0), (HALO - (CONV_WIDTH - 1), 0), (0, 0)))
    c3 = CONV_CH
    qcol = 3 * CONV_CH // MEM_WIDTH

    in_specs = [pl.BlockSpec((1, tt, c3), lambda bi, i: (bi, i, 0)),
                pl.BlockSpec((1, tt, c3), lambda bi, i: (bi, i, 1)),
                pl.BlockSpec((1, tt, c3), lambda bi, i: (bi, i, 2)),
                pl.BlockSpec((1, tt, MEM_WIDTH), lambda bi, i: (bi, i, qcol)),
                pl.BlockSpec((1, tt, MEM_WIDTH), lambda bi, i: (bi, i, qcol + 1))]
    args = [h, h, h, h, h]
    if has_halo:
        per = tt // HALO
        in_specs += [pl.BlockSpec((1, HALO, c3), lambda bi, i: (bi, jnp.maximum(i * per - 1, 0), 0)),
                     pl.BlockSpec((1, HALO, c3), lambda bi, i: (bi, jnp.maximum(i * per - 1, 0), 1))]
        args += [h, h]
    in_specs += [pl.BlockSpec((1, HALO, c3), lambda bi, i: (bi, 0, 0)),
                 pl.BlockSpec((1, N_MEM, 2 * MEM_WIDTH), lambda bi, i: (bi, 0, 0)),
                 pl.BlockSpec((CONV_WIDTH, c3), lambda bi, i: (0, 0)),
                 pl.BlockSpec((1, c3), lambda bi, i: (0, 0)),
                 pl.BlockSpec((1, c3), lambda bi, i: (0, 0)),
                 pl.BlockSpec((1, c3), lambda bi, i: (0, 0))]
    args += [cprev, mem, w_dw, b_dw.reshape(1, c3), cn_g.reshape(1, c3), cn_b.reshape(1, c3)]

    return pl.pallas_call(
        functools.partial(_a_mix_kernel, tt=tt, tr=tr, has_halo=has_halo),
        grid=(b, nt),
        in_specs=in_specs,
        out_specs=[pl.BlockSpec((1, tt, D_MODEL), lambda bi, i: (bi, i, 0)),
                   pl.BlockSpec((1, CONV_WIDTH - 1, c3), lambda bi, i: (bi, 0, 0))],
        out_shape=[jax.ShapeDtypeStruct((b, t, D_MODEL), F32),
                   jax.ShapeDtypeStruct((b, CONV_WIDTH - 1, c3), F32)],
        scratch_shapes=[pltpu.VMEM((tt + HALO, c3), F32),
                        pltpu.VMEM((8, tt + HALO, c3), F32),
                        pltpu.VMEM((tt, c3), F32)],
        compiler_params=_params("parallel", "arbitrary"),
        name="a_mix",
    )(*args)


def _rows(ref, start, size, stride):
    if stride == 1:
        return ref[0, start:start + size, :]
    return ref[0, pl.ds(start, size, stride=stride), :]


def _b_attn_kernel(q0, q1, q2, k0, k1, k2, v0, v1, v2, hk0, hk1, hk2, hv0, hv1, hv2, bias_ref, gate_ref,
                   o_ref, num_scr, m_scr, s_scr):
    first = jnp.minimum(pl.program_id(2), 1)
    qs, ks, vs = (q0, q1, q2), (k0, k1, k2), (v0, v1, v2)
    hks, hvs = (hk0, hk1, hk2), (hv0, hv1, hv2)
    dn = (((1,), (1,)), ((), ()))
    for g, (_, d) in enumerate(B_GROUPS):
        for r in range(d):
            for s in range(ATTN_TOKENS // (BAND * d)):
                start = r + d * BAND * s
                q = _rows(qs[g], start, BAND, d)
                if s == 0:
                    k = jnp.concatenate([_rows(hks[g], r, BAND, d), _rows(ks[g], r, BAND, d)], 0)
                    v = jnp.concatenate([_rows(hvs[g], r, BAND, d), _rows(vs[g], r, BAND, d)], 0)
                    bias = bias_ref[g, first, 0]
                else:
                    k = _rows(ks[g], start - d * BAND, 2 * BAND, d)
                    v = _rows(vs[g], start - d * BAND, 2 * BAND, d)
                    bias = bias_ref[g, 1, 0]
                logits = jax.lax.dot_general(q.astype(BF16), k.astype(BF16), dn, preferred_element_type=F32)
                logits = logits * SCALE + bias
                m = jnp.max(logits, -1, keepdims=True)
                e = jnp.exp(logits - m)
                ssum = jnp.sum(e, -1, keepdims=True)
                num = jnp.dot(e.astype(BF16), v.astype(BF16), preferred_element_type=F32)
                if d == 1:
                    dst = pl.ds(start, BAND)
                else:
                    dst = pl.ds(start, BAND, stride=d)
                num_scr[g, dst, :] = num
                m_scr[g, dst, :] = jnp.broadcast_to(m, (BAND, HEAD_DIM))
                s_scr[g, dst, :] = jnp.broadcast_to(ssum, (BAND, HEAD_DIM))

    def merge(c, carry):
        sl = pl.ds(pl.multiple_of(c * BAND, BAND), BAND)
        ms = [m_scr[g, sl, :] for g in range(N_GROUPS)]
        m_all = jnp.maximum(jnp.maximum(ms[0], ms[1]), ms[2])
        coef = [jnp.exp(m - m_all) for m in ms]
        num = coef[0] * num_scr[0, sl, :]
        den = coef[0] * s_scr[0, sl, :]
        for g in range(1, N_GROUPS):
            num = num + coef[g] * num_scr[g, sl, :]
            den = den + coef[g] * s_scr[g, sl, :]
        o_ref[0, sl, :] = num / den * _silu(gate_ref[0, sl, :])
        return carry

    jax.lax.fori_loop(0, ATTN_TOKENS // BAND, merge, 0)


def _b_attn(h, kv, band_bias):
    b, t, _ = h.shape
    tb = ATTN_TOKENS
    blk = (1, tb, HEAD_DIM)

    def cur(col0):
        return lambda bi, hi, i: (bi, i, col0 + hi)

    def halo_spec(d, col0):
        rows = BAND * d
        per = tb // rows
        return pl.BlockSpec((1, rows, HEAD_DIM), lambda bi, hi, i: (bi, jnp.maximum(i * per - 1, 0), col0 + hi))

    hpg = HEADS_PER_GROUP
    in_specs = ([pl.BlockSpec(blk, cur(hpg * g)) for g in range(N_GROUPS)]
                + [pl.BlockSpec(blk, cur(hpg * g)) for g in range(N_GROUPS)]
                + [pl.BlockSpec(blk, cur(B_HEADS + hpg * g)) for g in range(N_GROUPS)]
                + [halo_spec(d, hpg * g) for g, (_, d) in enumerate(B_GROUPS)]
                + [halo_spec(d, B_HEADS + hpg * g) for g, (_, d) in enumerate(B_GROUPS)]
                + [pl.BlockSpec((N_GROUPS, 2, 1, BAND, 2 * BAND), lambda bi, hi, i: (0, 0, hi, 0, 0)),
                   pl.BlockSpec(blk, cur(B_HEADS))])
    return pl.pallas_call(
        _b_attn_kernel,
        grid=(b, hpg, t // tb),
        in_specs=in_specs,
        out_specs=pl.BlockSpec(blk, lambda bi, hi, i: (bi, i, hi)),
        out_shape=jax.ShapeDtypeStruct((b, t, B_OUT), F32),
        scratch_shapes=[pltpu.VMEM((N_GROUPS, tb, HEAD_DIM), F32)] * 3,
        compiler_params=_params("parallel", "parallel", "arbitrary"),
        name="b_attn",
    )(h, h, h, kv, kv, kv, kv, kv, kv, kv, kv, kv, kv, kv, kv, band_bias, h)


def _b_out_kernel(o_ref, qm_ref, gm_ref, mem_ref, w_ref, x_ref, g_ref, b_ref, y_ref, wb_ref):
    @pl.when((pl.program_id(0) == 0) & (pl.program_id(1) == 0))
    def _():
        wb_ref[...] = w_ref[...].astype(BF16)

    mbranch = _mem_attn(qm_ref[0], gm_ref[0], mem_ref[0])
    y = jnp.dot(o_ref[0].astype(BF16), wb_ref[0:B_OUT, :], preferred_element_type=F32)
    y = y + jnp.dot(mbranch.astype(BF16), wb_ref[B_OUT:B_OUT + MEM_WIDTH, :], preferred_element_type=F32)
    y_ref[0] = _layernorm(ALPHA * x_ref[0] + y, g_ref[...], b_ref[...])


def _b_out(o, h, mem, w, x, g, b):
    bsz, t, d = x.shape
    tt = min(t, 256)
    kk = w.shape[0]
    qcol = (MIX_WIDTH + B_OUT) // MEM_WIDTH
    return pl.pallas_call(
        _b_out_kernel,
        grid=(bsz, t // tt),
        in_specs=[pl.BlockSpec((1, tt, B_OUT), lambda bi, i: (bi, i, 0)),
                  pl.BlockSpec((1, tt, MEM_WIDTH), lambda bi, i: (bi, i, qcol)),
                  pl.BlockSpec((1, tt, MEM_WIDTH), lambda bi, i: (bi, i, qcol + 1)),
                  pl.BlockSpec((1, N_MEM, 2 * MEM_WIDTH), lambda bi, i: (bi, 0, 0)),
                  pl.BlockSpec((kk, d), lambda bi, i: (0, 0)),
                  pl.BlockSpec((1, tt, d), lambda bi, i: (bi, i, 0)),
                  pl.BlockSpec((1, d), lambda bi, i: (0, 0)),
                  pl.BlockSpec((1, d), lambda bi, i: (0, 0))],
        out_specs=pl.BlockSpec((1, tt, d), lambda bi, i: (bi, i, 0)),
        out_shape=jax.ShapeDtypeStruct((bsz, t, d), F32),
        scratch_shapes=[pltpu.VMEM((kk, d), BF16)],
        compiler_params=_params("arbitrary", "arbitrary"),
        name="b_out",
    )(o, h, h, mem, w, x, g.reshape(1, d), b.reshape(1, d))


def _samp_attn_kernel(q_ref, st_ref, kn_ref, vn_ref, bs_ref, bn_ref, num_ref, m_ref, s_ref, *, nq):
    dn = (((1,), (1,)), ((), ()))
    for h in range(HEADS_PER_GROUP):
        lo = h * HEAD_DIM
        q = q_ref[0, :, lo:lo + HEAD_DIM].astype(BF16)
        ks = st_ref[0, :, lo:lo + HEAD_DIM].astype(BF16)
        vs = st_ref[0, :, GROUP_WIDTH + lo:GROUP_WIDTH + lo + HEAD_DIM].astype(BF16)
        kn = kn_ref[0, :, lo:lo + HEAD_DIM].astype(BF16)
        vn = vn_ref[0, :, lo:lo + HEAD_DIM].astype(BF16)
        ls = jax.lax.dot_general(q, ks, dn, preferred_element_type=F32) * SCALE + bs_ref[h]
        ln = jax.lax.dot_general(q, kn, dn, preferred_element_type=F32) * SCALE + bn_ref[h]
        m = jnp.maximum(jnp.max(ls, -1, keepdims=True), jnp.max(ln, -1, keepdims=True))
        es = jnp.exp(ls - m)
        en = jnp.exp(ln - m)
        ssum = jnp.sum(es, -1, keepdims=True) + jnp.sum(en, -1, keepdims=True)
        num_ref[0, :, lo:lo + HEAD_DIM] = (jnp.dot(es.astype(BF16), vs, preferred_element_type=F32)
                                           + jnp.dot(en.astype(BF16), vn, preferred_element_type=F32))
        m_ref[0, :, lo:lo + HEAD_DIM] = jnp.broadcast_to(m, (nq, HEAD_DIM))
        s_ref[0, :, lo:lo + HEAD_DIM] = jnp.broadcast_to(ssum, (nq, HEAD_DIM))


def _samp_attn(h, qcol, state, kv, kcol, vcol, bias_state, bias_new):
    b, nq, _ = h.shape
    w = state.shape[1]
    blk = (1, nq, GROUP_WIDTH)
    out = jax.ShapeDtypeStruct((b, nq, GROUP_WIDTH), F32)
    return pl.pallas_call(
        functools.partial(_samp_attn_kernel, nq=nq),
        grid=(b,),
        in_specs=[pl.BlockSpec(blk, lambda i: (i, 0, qcol)),
                  pl.BlockSpec((1, w, 2 * GROUP_WIDTH), lambda i: (i, 0, 0)),
                  pl.BlockSpec(blk, lambda i: (i, 0, kcol)),
                  pl.BlockSpec(blk, lambda i: (i, 0, vcol)),
                  pl.BlockSpec((HEADS_PER_GROUP, nq, w), lambda i: (0, 0, 0)),
                  pl.BlockSpec((HEADS_PER_GROUP, nq, nq), lambda i: (0, 0, 0))],
        out_specs=[pl.BlockSpec(blk, lambda i: (i, 0, 0))] * 3,
        out_shape=[out] * 3,
        compiler_params=_params("parallel"),
        name="samp_attn",
    )(h, state, kv, kv, bias_state, bias_new)


def _samp_merge_kernel(n0, m0, s0, n1, m1, s1, n2, m2, s2, gate, o_out):
    ms = [m0[0], m1[0], m2[0]]
    m_all = jnp.maximum(jnp.maximum(ms[0], ms[1]), ms[2])
    coef = [jnp.exp(m - m_all) for m in ms]
    num = coef[0] * n0[0] + coef[1] * n1[0] + coef[2] * n2[0]
    den = coef[0] * s0[0] + coef[1] * s1[0] + coef[2] * s2[0]
    o_out[0] = num / den * _silu(gate[0])


def _samp_merge(parts, h):
    b, nq, _ = h.shape
    spec = pl.BlockSpec((1, nq, GROUP_WIDTH), lambda i: (i, 0, 0))
    flat = [a for p in parts for a in p]
    return pl.pallas_call(
        _samp_merge_kernel,
        grid=(b,),
        in_specs=[spec] * 9 + [pl.BlockSpec((1, nq, GROUP_WIDTH), lambda i: (i, 0, MIX_WIDTH // GROUP_WIDTH))],
        out_specs=spec,
        out_shape=jax.ShapeDtypeStruct((b, nq, GROUP_WIDTH), F32),
        compiler_params=_params("parallel"),
        name="samp_merge",
    )(*flat, h)


def _t5_bucket(dist):
    max_exact = REL_BUCKETS // 2
    safe = jnp.maximum(dist, 1).astype(F32)
    large = max_exact + (jnp.log(safe / max_exact) / math.log(REL_MAX_DIST / max_exact)
                         * (REL_BUCKETS - max_exact)).astype(jnp.int32)
    large = jnp.minimum(large, REL_BUCKETS - 1)
    return jnp.where(dist < max_exact, dist, large)


def _group_bias(rel_bias, g):
    w, d = B_GROUPS[g]
    dist = d * jnp.arange(w // d + 1, dtype=jnp.int32)
    tab = rel_bias[_t5_bucket(dist)]
    return tab[:, g * HEADS_PER_GROUP:(g + 1) * HEADS_PER_GROUP].T.astype(F32)


def _toeplitz(vec, rows, cols):
    hh, p = vec.shape
    flat = jnp.tile(vec, (1, rows))[:, :rows * (p - 1)]
    return flat.reshape(hh, rows, p - 1)[:, :, :cols]


def _band_bias(bias):
    period = 3 * BAND
    vec = jnp.full((bias.shape[0], period), NEG_INF, F32)
    vec = jax.lax.dynamic_update_slice(vec, bias[:, ::-1], (0, 0))
    mat = _toeplitz(vec, BAND, 2 * BAND)
    c = jnp.arange(2 * BAND, dtype=jnp.int32)[None, None, :]
    first = jnp.where(c >= BAND, mat, NEG_INF)
    return jnp.stack([first, mat])


def _sample_bias(bias, w, d, nq):
    hh = bias.shape[0]
    neg = jnp.full_like(bias, NEG_INF)
    by_dist = jnp.stack([bias] + [neg] * (d - 1), -1).reshape(hh, -1)
    by_dist = jnp.concatenate([by_dist[:, :w + 1], jnp.full((hh, nq), NEG_INF, F32)], 1)
    rev = by_dist[:, ::-1]
    bs = jnp.stack([rev[:, nq - n:nq - n + w] for n in range(nq)], 1)
    small = jnp.concatenate([jnp.full((hh, nq), NEG_INF, F32), by_dist[:, :nq]], 1)[:, ::-1]
    bn = jnp.stack([small[:, nq - 1 - n:2 * nq - 1 - n] for n in range(nq)], 1)
    return bs, bn


def _trunk(x, conv_prev, mem_kv, kv_state, a_w_in, a_w_dw, a_b_dw, a_cn_g, a_cn_b, a_w_out,
           b_w_in, b_w_out, w_kv_shared, rel_bias, ln_g, ln_b):
    b, t, d_model = x.shape
    m = b * t
    x2 = x.reshape(m, d_model)
    new_conv = []
    for l in range(N_A_LAYERS):
        h = _mm(x2, a_w_in[l]).reshape(b, t, -1)
        cm, nc = _a_mix(h, conv_prev[l], mem_kv[l], a_w_dw[l], a_b_dw[l], a_cn_g[l], a_cn_b[l])
        new_conv.append(nc)
        x2 = _out_ln(cm.reshape(m, d_model), a_w_out[l], x2, ln_g[l], ln_b[l])

    kv = _mm(x2, w_kv_shared).reshape(b, t, 2 * MIX_WIDTH)
    vcol0 = MIX_WIDTH // GROUP_WIDTH
    biases = [_group_bias(rel_bias, g) for g in range(N_GROUPS)]
    kv_new = [jnp.concatenate([kv[:, t - min(w, t):, g * GROUP_WIDTH:(g + 1) * GROUP_WIDTH],
                               kv[:, t - min(w, t):, MIX_WIDTH + g * GROUP_WIDTH:MIX_WIDTH + (g + 1) * GROUP_WIDTH]], -1)
              for g, (w, _) in enumerate(B_GROUPS)]
    if kv_state is None:
        new_bufs = kv_new
        band_bias = jnp.stack([_band_bias(bi) for bi in biases])
    else:
        new_bufs = [jnp.concatenate([st[:, t:], kn], 1) for st, kn in zip(kv_state, kv_new)]
        samp_bias = [_sample_bias(bi, w, d, t) for bi, (w, d) in zip(biases, B_GROUPS)]

    x3 = x2.reshape(b, t, d_model)
    for i in range(DEPTH - N_A_LAYERS):
        l = N_A_LAYERS + i
        h = _mm(x3.reshape(m, d_model), b_w_in[i]).reshape(b, t, -1)
        if kv_state is None:
            o = _b_attn(h, kv, band_bias)
        else:
            parts = [_samp_attn(h, g, kv_state[g], kv, g, vcol0 + g, *samp_bias[g]) for g in range(N_GROUPS)]
            o = _samp_merge(parts, h)
        x3 = _b_out(o, h, mem_kv[l], b_w_out[i], x3, ln_g[l], ln_b[l])
    return x3, jnp.stack(new_conv), new_bufs


def kernel(x_prompt, x_sample, state_conv, state_kv_g0, state_kv_g1, state_kv_g2, cache_mem_kv, mem_prompt,
           a_w_in, a_w_dw, a_b_dw, a_cn_g, a_cn_b, a_w_out, b_w_in, b_w_out, w_kv_shared, w_mem_kv,
           rel_bias, ln_g, ln_b):
    bp = x_prompt.shape[0]
    bs = x_sample.shape[0]
    memp2 = mem_prompt.reshape(bp * N_MEM, D_MODEL)
    mem_p = jnp.stack([_mm(memp2, w_mem_kv[l]) for l in range(DEPTH)]).reshape(DEPTH, bp, N_MEM, 2 * MEM_WIDTH)
    conv_zero = jnp.zeros((N_A_LAYERS, bp, CONV_WIDTH - 1, CONV_CH), x_prompt.dtype)
    weights = (a_w_in, a_w_dw, a_b_dw, a_cn_g, a_cn_b, a_w_out, b_w_in, b_w_out, w_kv_shared, rel_bias, ln_g, ln_b)

    y_p, conv_p, bufs_p = _trunk(x_prompt, conv_zero, mem_p, None, *weights)
    kv_state = [s.reshape(bs, s.shape[1], 2 * GROUP_WIDTH) for s in (state_kv_g0, state_kv_g1, state_kv_g2)]
    mem_s = cache_mem_kv.reshape(DEPTH, bs, N_MEM, 2 * MEM_WIDTH)
    y_s, conv_s, bufs_s = _trunk(x_sample, state_conv, mem_s, kv_state, *weights)

    kv5 = lambda a: a.reshape(a.shape[0], a.shape[1], 2, HEADS_PER_GROUP, HEAD_DIM)
    return (y_p, y_s, conv_p, conv_s,
            kv5(bufs_p[0]), kv5(bufs_s[0]), kv5(bufs_p[1]), kv5(bufs_s[1]), kv5(bufs_p[2]), kv5(bufs_s[2]),
            mem_p.reshape(DEPTH, bp, N_MEM, 2, MEM_HEADS, HEAD_DIM))
```

```python
import functools
import math

import jax
import jax.numpy as jnp
from jax.experimental import pallas as pl
from jax.experimental.pallas import tpu as pltpu

D_MODEL = 2048
DEPTH = 4
HEAD_DIM = 128
MEM_HEADS = 4
MEM_WIDTH = MEM_HEADS * HEAD_DIM
N_MEM = 256
MIX_WIDTH = D_MODEL - MEM_WIDTH
N_A_LAYERS = DEPTH // 2
CONV_CH = MIX_WIDTH
CONV_WIDTH = 31
B_GROUPS = ((128, 1), (512, 4), (2048, 16))
N_GROUPS = len(B_GROUPS)
HEADS_PER_GROUP = 4
B_HEADS = N_GROUPS * HEADS_PER_GROUP
GROUP_WIDTH = HEADS_PER_GROUP * HEAD_DIM
KV_SLOTS = 2 * HEADS_PER_GROUP
B_OUT = GROUP_WIDTH
REL_BUCKETS = 32
REL_MAX_DIST = 2048
BAND = 128
ATTN_TOKENS = BAND * max(d for _, d in B_GROUPS)
LN_EPS = 1e-5
ALPHA = (2 * DEPTH) ** 0.25
NEG_INF = -1e30
SCALE = HEAD_DIM ** -0.5

HALO = 32
VMEM_LIMIT = 56 * 1024 * 1024
BF16 = jnp.bfloat16
F32 = jnp.float32


def _params(*sem):
    return pltpu.CompilerParams(dimension_semantics=sem, vmem_limit_bytes=VMEM_LIMIT)


def _sigmoid(x):
    return 1.0 / (1.0 + jnp.exp(-x))


def _silu(x):
    return x * _sigmoid(x)


def _layernorm(z, g, b):
    mu = jnp.mean(z, -1, keepdims=True)
    zc = z - mu
    var = jnp.mean(zc * zc, -1, keepdims=True)
    return zc * jax.lax.rsqrt(var + LN_EPS) * g + b


def _mm_kernel(x_ref, w_ref, o_ref, xb_ref):
    @pl.when(pl.program_id(2) == 0)
    def _():
        xb_ref[...] = x_ref[...].astype(BF16)

    o_ref[...] = jnp.dot(xb_ref[...], w_ref[...].astype(BF16), preferred_element_type=F32)


def _mm(x, w, layer=None, *, tn=512):
    m, k = x.shape
    n = w.shape[2]
    tm = min(m, 1024)
    l0, nl = (0, w.shape[0]) if layer is None else (layer, 1)
    out = pl.pallas_call(
        _mm_kernel,
        grid=(nl, m // tm, n // tn),
        in_specs=[pl.BlockSpec((tm, k), lambda l, i, j: (i, 0)),
                  pl.BlockSpec((None, k, tn), lambda l, i, j: (l0 + l, 0, j))],
        out_specs=pl.BlockSpec((None, tm, tn), lambda l, i, j: (l, i, j)),
        out_shape=jax.ShapeDtypeStruct((nl, m, n), F32),
        scratch_shapes=[pltpu.VMEM((tm, k), BF16)],
        compiler_params=_params("arbitrary", "arbitrary", "arbitrary"),
        name="mm",
    )(x, w)
    return out if layer is None else out.reshape(m, n)


def _out_ln_kernel(cm_ref, w_ref, x_ref, g_ref, b_ref, o_ref, acc_ref):
    k = pl.program_id(1)

    @pl.when(k == 0)
    def _():
        acc_ref[...] = jnp.zeros_like(acc_ref)

    acc_ref[...] += jnp.dot(cm_ref[...].astype(BF16), w_ref[...].astype(BF16), preferred_element_type=F32)

    @pl.when(k == pl.num_programs(1) - 1)
    def _():
        o_ref[...] = _layernorm(ALPHA * x_ref[...] + acc_ref[...], g_ref[...], b_ref[...])


def _out_ln(cm, w, layer, x, g, b, *, tk=512):
    m, kk = cm.shape
    d = w.shape[2]
    tm = min(m, 512)
    return pl.pallas_call(
        _out_ln_kernel,
        grid=(m // tm, kk // tk),
        in_specs=[pl.BlockSpec((tm, tk), lambda i, k: (i, k)),
                  pl.BlockSpec((None, tk, d), lambda i, k: (layer, k, 0)),
                  pl.BlockSpec((tm, d), lambda i, k: (i, 0)),
                  pl.BlockSpec((1, d), lambda i, k: (0, 0)),
                  pl.BlockSpec((1, d), lambda i, k: (0, 0))],
        out_specs=pl.BlockSpec((tm, d), lambda i, k: (i, 0)),
        out_shape=jax.ShapeDtypeStruct((m, d), F32),
        scratch_shapes=[pltpu.VMEM((tm, d), F32)],
        compiler_params=_params("parallel", "arbitrary"),
        name="out_ln",
    )(cm, w, x, g.reshape(1, d), b.reshape(1, d))


def _kv_rows(ref, lead, slot, n):
    return ref[lead + (pl.ds(slot, n, stride=KV_SLOTS), slice(None))]


def _mem_attn(qm, gm, mem_ref):
    outs = []
    for h in range(MEM_HEADS):
        lo = h * HEAD_DIM
        q = qm[:, lo:lo + HEAD_DIM].astype(BF16)
        k = _kv_rows(mem_ref, (0,), h, N_MEM).astype(BF16)
        v = _kv_rows(mem_ref, (0,), MEM_HEADS + h, N_MEM).astype(BF16)
        s = jax.lax.dot_general(q, k, (((1,), (1,)), ((), ())), preferred_element_type=F32) * SCALE
        s = s - jnp.max(s, -1, keepdims=True)
        e = jnp.exp(s)
        p = e / jnp.sum(e, -1, keepdims=True)
        outs.append(jnp.dot(p.astype(BF16), v, preferred_element_type=F32))
    return jnp.concatenate(outs, -1) * _silu(gm)


def _a_mix_kernel(*refs, tt, tr, has_halo):
    if has_halo:
        (ga, gb, gate, qm, gm, gap, gbp, cprev, mem, wdw, bdw, cng, cnb,
         cm_out, nc_out, ext, sh, conv) = refs
    else:
        (ga, gb, gate, qm, gm, cprev, mem, wdw, bdw, cng, cnb,
         cm_out, nc_out, ext, sh, conv) = refs
    i = pl.program_id(1)

    u = ga[0] * _sigmoid(gb[0])
    if has_halo:
        prev_u = gap[0] * _sigmoid(gbp[0])
        halo = jnp.where(i == 0, cprev[0], prev_u)
    else:
        halo = cprev[0]
    ext[0:HALO, :] = halo
    ext[HALO:HALO + tt, :] = u

    nc_out[0] = ext[HALO + tt - (CONV_WIDTH - 1):HALO + tt, :]

    n_sh = tt + HALO - 8
    sh[0, :, :] = ext[...]
    for s in range(1, 8):
        sh[s, 0:n_sh, :] = ext[s:s + n_sh, :]

    off0 = HALO - (CONV_WIDTH - 1)
    lane_chunk = 512
    for c in range(CONV_CH // lane_chunk):
        cl = c * lane_chunk

        def body(rc, carry, cl=cl):
            r0 = pl.multiple_of(rc * tr, 8)
            acc = jnp.zeros((tr, lane_chunk), F32)
            for k in range(CONV_WIDTH):
                o = k + off0
                a, s = divmod(o, 8)
                win = sh[s, pl.ds(pl.multiple_of(r0 + 8 * a, 8), tr), cl:cl + lane_chunk]
                acc = acc + win * wdw[k:k + 1, cl:cl + lane_chunk]
            conv[pl.ds(r0, tr), cl:cl + lane_chunk] = acc + bdw[:, cl:cl + lane_chunk]
            return carry

        jax.lax.fori_loop(0, tt // tr, body, 0)

    zn = _layernorm(conv[...], cng[...], cnb[...])
    cm_out[0, :, 0:CONV_CH] = _silu(zn) * _silu(gate[0])
    cm_out[0, :, CONV_CH:D_MODEL] = _mem_attn(qm[0], gm[0], mem)


def _a_mix(h, conv_prev, mem, layer, w_dw, b_dw, cn_g, cn_b):
    b, t, _ = h.shape
    tt = min(t, 256)
    tr = min(tt, 32)
    nt = t // tt
    has_halo = nt > 1
    cprev = jnp.pad(conv_prev, ((0, 0), (HALO - (CONV_WIDTH - 1), 0), (0, 0)))
    c3 = CONV_CH
    qcol = 3 * CONV_CH // MEM_WIDTH

    in_specs = [pl.BlockSpec((1, tt, c3), lambda bi, i: (bi, i, 0)),
                pl.BlockSpec((1, tt, c3), lambda bi, i: (bi, i, 1)),
                pl.BlockSpec((1, tt, c3), lambda bi, i: (bi, i, 2)),
                pl.BlockSpec((1, tt, MEM_WIDTH), lambda bi, i: (bi, i, qcol)),
                pl.BlockSpec((1, tt, MEM_WIDTH), lambda bi, i: (bi, i, qcol + 1))]
    args = [h, h, h, h, h]
    if has_halo:
        per = tt // HALO
        in_specs += [pl.BlockSpec((1, HALO, c3), lambda bi, i: (bi, jnp.maximum(i * per - 1, 0), 0)),
                     pl.BlockSpec((1, HALO, c3), lambda bi, i: (bi, jnp.maximum(i * per - 1, 0), 1))]
        args += [h, h]
    in_specs += [pl.BlockSpec((1, HALO, c3), lambda bi, i: (bi, 0, 0)),
                 pl.BlockSpec((None, 1, N_MEM * KV_SLOTS, HEAD_DIM), lambda bi, i: (layer, bi, 0, 0)),
                 pl.BlockSpec((CONV_WIDTH, c3), lambda bi, i: (0, 0)),
                 pl.BlockSpec((1, c3), lambda bi, i: (0, 0)),
                 pl.BlockSpec((1, c3), lambda bi, i: (0, 0)),
                 pl.BlockSpec((1, c3), lambda bi, i: (0, 0))]
    args += [cprev, mem, w_dw, b_dw.reshape(1, c3), cn_g.reshape(1, c3), cn_b.reshape(1, c3)]

    return pl.pallas_call(
        functools.partial(_a_mix_kernel, tt=tt, tr=tr, has_halo=has_halo),
        grid=(b, nt),
        in_specs=in_specs,
        out_specs=[pl.BlockSpec((1, tt, D_MODEL), lambda bi, i: (bi, i, 0)),
                   pl.BlockSpec((1, CONV_WIDTH - 1, c3), lambda bi, i: (bi, 0, 0))],
        out_shape=[jax.ShapeDtypeStruct((b, t, D_MODEL), F32),
                   jax.ShapeDtypeStruct((b, CONV_WIDTH - 1, c3), F32)],
        scratch_shapes=[pltpu.VMEM((tt + HALO, c3), F32),
                        pltpu.VMEM((8, tt + HALO, c3), F32),
                        pltpu.VMEM((tt, c3), F32)],
        compiler_params=_params("parallel", "arbitrary"),
        name="a_mix",
    )(*args)


def _rows(ref, start, size, stride):
    if stride == 1:
        return ref[0, start:start + size, :]
    return ref[0, pl.ds(start, size, stride=stride), :]


def _b_attn_kernel(q0, q1, q2, k0, k1, k2, v0, v1, v2, hk0, hk1, hk2, hv0, hv1, hv2, bias_ref, gate_ref,
                   o_ref, num_scr, m_scr, s_scr):
    first = jnp.minimum(pl.program_id(2), 1)
    qs, ks, vs = (q0, q1, q2), (k0, k1, k2), (v0, v1, v2)
    hks, hvs = (hk0, hk1, hk2), (hv0, hv1, hv2)
    dn = (((1,), (1,)), ((), ()))
    for g, (_, d) in enumerate(B_GROUPS):
        for r in range(d):
            for s in range(ATTN_TOKENS // (BAND * d)):
                start = r + d * BAND * s
                q = _rows(qs[g], start, BAND, d)
                if s == 0:
                    k = jnp.concatenate([_rows(hks[g], r, BAND, d), _rows(ks[g], r, BAND, d)], 0)
                    v = jnp.concatenate([_rows(hvs[g], r, BAND, d), _rows(vs[g], r, BAND, d)], 0)
                    bias = bias_ref[g, first, 0]
                else:
                    k = _rows(ks[g], start - d * BAND, 2 * BAND, d)
                    v = _rows(vs[g], start - d * BAND, 2 * BAND, d)
                    bias = bias_ref[g, 1, 0]
                logits = jax.lax.dot_general(q.astype(BF16), k.astype(BF16), dn, preferred_element_type=F32)
                logits = logits * SCALE + bias
                m = jnp.max(logits, -1, keepdims=True)
                e = jnp.exp(logits - m)
                ssum = jnp.sum(e, -1, keepdims=True)
                num = jnp.dot(e.astype(BF16), v.astype(BF16), preferred_element_type=F32)
                if d == 1:
                    dst = pl.ds(start, BAND)
                else:
                    dst = pl.ds(start, BAND, stride=d)
                num_scr[g, dst, :] = num
                m_scr[g, dst, :] = jnp.broadcast_to(m, (BAND, HEAD_DIM))
                s_scr[g, dst, :] = jnp.broadcast_to(ssum, (BAND, HEAD_DIM))

    def merge(c, carry):
        sl = pl.ds(pl.multiple_of(c * BAND, BAND), BAND)
        ms = [m_scr[g, sl, :] for g in range(N_GROUPS)]
        m_all = jnp.maximum(jnp.maximum(ms[0], ms[1]), ms[2])
        coef = [jnp.exp(m - m_all) for m in ms]
        num = coef[0] * num_scr[0, sl, :]
        den = coef[0] * s_scr[0, sl, :]
        for g in range(1, N_GROUPS):
            num = num + coef[g] * num_scr[g, sl, :]
            den = den + coef[g] * s_scr[g, sl, :]
        o_ref[0, sl, :] = num / den * _silu(gate_ref[0, sl, :])
        return carry

    jax.lax.fori_loop(0, ATTN_TOKENS // BAND, merge, 0)


def _b_attn(h, kv, band_bias):
    b, t, _ = h.shape
    tb = ATTN_TOKENS
    blk = (1, tb, HEAD_DIM)

    def cur(col0):
        return lambda bi, hi, i: (bi, i, col0 + hi)

    def halo_spec(d, col0):
        rows = BAND * d
        per = tb // rows
        return pl.BlockSpec((1, rows, HEAD_DIM), lambda bi, hi, i: (bi, jnp.maximum(i * per - 1, 0), col0 + hi))

    hpg = HEADS_PER_GROUP
    in_specs = ([pl.BlockSpec(blk, cur(hpg * g)) for g in range(N_GROUPS)]
                + [pl.BlockSpec(blk, cur(hpg * g)) for g in range(N_GROUPS)]
                + [pl.BlockSpec(blk, cur(B_HEADS + hpg * g)) for g in range(N_GROUPS)]
                + [halo_spec(d, hpg * g) for g, (_, d) in enumerate(B_GROUPS)]
                + [halo_spec(d, B_HEADS + hpg * g) for g, (_, d) in enumerate(B_GROUPS)]
                + [pl.BlockSpec((N_GROUPS, 2, 1, BAND, 2 * BAND), lambda bi, hi, i: (0, 0, hi, 0, 0)),
                   pl.BlockSpec(blk, cur(B_HEADS))])
    return pl.pallas_call(
        _b_attn_kernel,
        grid=(b, hpg, t // tb),
        in_specs=in_specs,
        out_specs=pl.BlockSpec(blk, lambda bi, hi, i: (bi, i, hi)),
        out_shape=jax.ShapeDtypeStruct((b, t, B_OUT), F32),
        scratch_shapes=[pltpu.VMEM((N_GROUPS, tb, HEAD_DIM), F32)] * 3,
        compiler_params=_params("parallel", "parallel", "arbitrary"),
        name="b_attn",
    )(h, h, h, kv, kv, kv, kv, kv, kv, kv, kv, kv, kv, kv, kv, band_bias, h)


def _b_out_kernel(o_ref, qm_ref, gm_ref, mem_ref, w_ref, x_ref, g_ref, b_ref, y_ref, wb_ref):
    @pl.when((pl.program_id(0) == 0) & (pl.program_id(1) == 0))
    def _():
        wb_ref[...] = w_ref[...].astype(BF16)

    mbranch = _mem_attn(qm_ref[0], gm_ref[0], mem_ref)
    y = jnp.dot(o_ref[0].astype(BF16), wb_ref[0:B_OUT, :], preferred_element_type=F32)
    y = y + jnp.dot(mbranch.astype(BF16), wb_ref[B_OUT:B_OUT + MEM_WIDTH, :], preferred_element_type=F32)
    y_ref[0] = _layernorm(ALPHA * x_ref[0] + y, g_ref[...], b_ref[...])


def _b_out(o, h, mem, layer, w, wl, x, g, b):
    bsz, t, d = x.shape
    tt = min(t, 256)
    kk = w.shape[1]
    qcol = (MIX_WIDTH + B_OUT) // MEM_WIDTH
    return pl.pallas_call(
        _b_out_kernel,
        grid=(bsz, t // tt),
        in_specs=[pl.BlockSpec((1, tt, B_OUT), lambda bi, i: (bi, i, 0)),
                  pl.BlockSpec((1, tt, MEM_WIDTH), lambda bi, i: (bi, i, qcol)),
                  pl.BlockSpec((1, tt, MEM_WIDTH), lambda bi, i: (bi, i, qcol + 1)),
                  pl.BlockSpec((None, 1, N_MEM * KV_SLOTS, HEAD_DIM), lambda bi, i: (layer, bi, 0, 0)),
                  pl.BlockSpec((None, kk, d), lambda bi, i: (wl, 0, 0)),
                  pl.BlockSpec((1, tt, d), lambda bi, i: (bi, i, 0)),
                  pl.BlockSpec((1, d), lambda bi, i: (0, 0)),
                  pl.BlockSpec((1, d), lambda bi, i: (0, 0))],
        out_specs=pl.BlockSpec((1, tt, d), lambda bi, i: (bi, i, 0)),
        out_shape=jax.ShapeDtypeStruct((bsz, t, d), F32),
        scratch_shapes=[pltpu.VMEM((kk, d), BF16)],
        compiler_params=_params("arbitrary", "arbitrary"),
        name="b_out",
    )(o, h, h, mem, w, x, g.reshape(1, d), b.reshape(1, d))


def _samp_attn_kernel(q_ref, st_ref, kn_ref, vn_ref, bs_ref, bn_ref, num_ref, m_ref, s_ref, *, nq, w):
    dn = (((1,), (1,)), ((), ()))
    for h in range(HEADS_PER_GROUP):
        lo = h * HEAD_DIM
        q = q_ref[0, :, lo:lo + HEAD_DIM].astype(BF16)
        ks = _kv_rows(st_ref, (0,), h, w).astype(BF16)
        vs = _kv_rows(st_ref, (0,), HEADS_PER_GROUP + h, w).astype(BF16)
        kn = kn_ref[0, :, lo:lo + HEAD_DIM].astype(BF16)
        vn = vn_ref[0, :, lo:lo + HEAD_DIM].astype(BF16)
        ls = jax.lax.dot_general(q, ks, dn, preferred_element_type=F32) * SCALE + bs_ref[h]
        ln = jax.lax.dot_general(q, kn, dn, preferred_element_type=F32) * SCALE + bn_ref[h]
        m = jnp.maximum(jnp.max(ls, -1, keepdims=True), jnp.max(ln, -1, keepdims=True))
        es = jnp.exp(ls - m)
        en = jnp.exp(ln - m)
        ssum = jnp.sum(es, -1, keepdims=True) + jnp.sum(en, -1, keepdims=True)
        num_ref[0, :, lo:lo + HEAD_DIM] = (jnp.dot(es.astype(BF16), vs, preferred_element_type=F32)
                                           + jnp.dot(en.astype(BF16), vn, preferred_element_type=F32))
        m_ref[0, :, lo:lo + HEAD_DIM] = jnp.broadcast_to(m, (nq, HEAD_DIM))
        s_ref[0, :, lo:lo + HEAD_DIM] = jnp.broadcast_to(ssum, (nq, HEAD_DIM))


def _samp_attn(h, qcol, state, kv, kcol, vcol, bias_state, bias_new):
    b, nq, _ = h.shape
    w = state.shape[1] // KV_SLOTS
    blk = (1, nq, GROUP_WIDTH)
    out = jax.ShapeDtypeStruct((b, nq, GROUP_WIDTH), F32)
    return pl.pallas_call(
        functools.partial(_samp_attn_kernel, nq=nq, w=w),
        grid=(b,),
        in_specs=[pl.BlockSpec(blk, lambda i: (i, 0, qcol)),
                  pl.BlockSpec((1, w * KV_SLOTS, HEAD_DIM), lambda i: (i, 0, 0)),
                  pl.BlockSpec(blk, lambda i: (i, 0, kcol)),
                  pl.BlockSpec(blk, lambda i: (i, 0, vcol)),
                  pl.BlockSpec((HEADS_PER_GROUP, nq, w), lambda i: (0, 0, 0)),
                  pl.BlockSpec((HEADS_PER_GROUP, nq, nq), lambda i: (0, 0, 0))],
        out_specs=[pl.BlockSpec(blk, lambda i: (i, 0, 0))] * 3,
        out_shape=[out] * 3,
        compiler_params=_params("parallel"),
        name="samp_attn",
    )(h, state, kv, kv, bias_state, bias_new)


def _samp_merge_kernel(n0, m0, s0, n1, m1, s1, n2, m2, s2, gate, o_out):
    ms = [m0[0], m1[0], m2[0]]
    m_all = jnp.maximum(jnp.maximum(ms[0], ms[1]), ms[2])
    coef = [jnp.exp(m - m_all) for m in ms]
    num = coef[0] * n0[0] + coef[1] * n1[0] + coef[2] * n2[0]
    den = coef[0] * s0[0] + coef[1] * s1[0] + coef[2] * s2[0]
    o_out[0] = num / den * _silu(gate[0])


def _samp_merge(parts, h):
    b, nq, _ = h.shape
    spec = pl.BlockSpec((1, nq, GROUP_WIDTH), lambda i: (i, 0, 0))
    flat = [a for p in parts for a in p]
    return pl.pallas_call(
        _samp_merge_kernel,
        grid=(b,),
        in_specs=[spec] * 9 + [pl.BlockSpec((1, nq, GROUP_WIDTH), lambda i: (i, 0, MIX_WIDTH // GROUP_WIDTH))],
        out_specs=spec,
        out_shape=jax.ShapeDtypeStruct((b, nq, GROUP_WIDTH), F32),
        compiler_params=_params("parallel"),
        name="samp_merge",
    )(*flat, h)


def _shift_state_kernel(main_ref, next_ref, new_ref, o_ref, *, rows, shift):
    last = pl.program_id(1) == pl.num_programs(1) - 1
    if rows > shift:
        o_ref[0, 0:rows - shift, :] = main_ref[0, shift:rows, :]
    o_ref[0, rows - shift:rows, :] = jnp.where(last, new_ref[0], next_ref[0])


def _shift_state(state, new):
    b, total, lanes = state.shape
    shift = new.shape[1]
    rows = min(total, 4096)
    nblk = total // rows
    per = rows // shift
    return pl.pallas_call(
        functools.partial(_shift_state_kernel, rows=rows, shift=shift),
        grid=(b, nblk),
        in_specs=[pl.BlockSpec((1, rows, lanes), lambda bi, i: (bi, i, 0)),
                  pl.BlockSpec((1, shift, lanes), lambda bi, i: (bi, jnp.minimum((i + 1) * per, nblk * per - 1), 0)),
                  pl.BlockSpec((1, shift, lanes), lambda bi, i: (bi, 0, 0))],
        out_specs=pl.BlockSpec((1, rows, lanes), lambda bi, i: (bi, i, 0)),
        out_shape=jax.ShapeDtypeStruct(state.shape, state.dtype),
        compiler_params=_params("parallel", "arbitrary"),
        name="shift_state",
    )(state, state, new)


def _t5_bucket(dist):
    max_exact = REL_BUCKETS // 2
    safe = jnp.maximum(dist, 1).astype(F32)
    large = max_exact + (jnp.log(safe / max_exact) / math.log(REL_MAX_DIST / max_exact)
                         * (REL_BUCKETS - max_exact)).astype(jnp.int32)
    large = jnp.minimum(large, REL_BUCKETS - 1)
    return jnp.where(dist < max_exact, dist, large)


def _group_bias(rel_bias, g):
    w, d = B_GROUPS[g]
    dist = d * jnp.arange(w // d + 1, dtype=jnp.int32)
    tab = rel_bias[_t5_bucket(dist)]
    return tab[:, g * HEADS_PER_GROUP:(g + 1) * HEADS_PER_GROUP].T.astype(F32)


def _toeplitz(vec, rows, cols):
    hh, p = vec.shape
    flat = jnp.tile(vec, (1, rows))[:, :rows * (p - 1)]
    return flat.reshape(hh, rows, p - 1)[:, :, :cols]


def _band_bias(bias):
    period = 3 * BAND
    vec = jnp.full((bias.shape[0], period), NEG_INF, F32)
    vec = jax.lax.dynamic_update_slice(vec, bias[:, ::-1], (0, 0))
    mat = _toeplitz(vec, BAND, 2 * BAND)
    c = jnp.arange(2 * BAND, dtype=jnp.int32)[None, None, :]
    first = jnp.where(c >= BAND, mat, NEG_INF)
    return jnp.stack([first, mat])


def _sample_bias(bias, w, d, nq):
    hh = bias.shape[0]
    neg = jnp.full_like(bias, NEG_INF)
    by_dist = jnp.stack([bias] + [neg] * (d - 1), -1).reshape(hh, -1)
    by_dist = jnp.concatenate([by_dist[:, :w + 1], jnp.full((hh, nq), NEG_INF, F32)], 1)
    rev = by_dist[:, ::-1]
    bs = jnp.stack([rev[:, nq - n:nq - n + w] for n in range(nq)], 1)
    small = jnp.concatenate([jnp.full((hh, nq), NEG_INF, F32), by_dist[:, :nq]], 1)[:, ::-1]
    bn = jnp.stack([small[:, nq - 1 - n:2 * nq - 1 - n] for n in range(nq)], 1)
    return bs, bn


def _trunk(x, conv_prev, mem_kv, kv_state, a_w_in, a_w_dw, a_b_dw, a_cn_g, a_cn_b, a_w_out,
           b_w_in, b_w_out, w_kv_shared, rel_bias, ln_g, ln_b):
    b, t, d_model = x.shape
    m = b * t
    x2 = x.reshape(m, d_model)
    new_conv = []
    for l in range(N_A_LAYERS):
        h = _mm(x2, a_w_in, l).reshape(b, t, -1)
        cm, nc = _a_mix(h, conv_prev[l], mem_kv, l, a_w_dw[l], a_b_dw[l], a_cn_g[l], a_cn_b[l])
        new_conv.append(nc)
        x2 = _out_ln(cm.reshape(m, d_model), a_w_out, l, x2, ln_g[l], ln_b[l])

    kv = _mm(x2, w_kv_shared[None], 0).reshape(b, t, 2 * MIX_WIDTH)
    vcol0 = MIX_WIDTH // GROUP_WIDTH
    biases = [_group_bias(rel_bias, g) for g in range(N_GROUPS)]
    kv_new = [jnp.concatenate([kv[:, t - min(w, t):, g * GROUP_WIDTH:(g + 1) * GROUP_WIDTH],
                               kv[:, t - min(w, t):, MIX_WIDTH + g * GROUP_WIDTH:MIX_WIDTH + (g + 1) * GROUP_WIDTH]], -1)
              for g, (w, _) in enumerate(B_GROUPS)]
    if kv_state is None:
        new_bufs = kv_new
        band_bias = jnp.stack([_band_bias(bi) for bi in biases])
    else:
        new_bufs = [_shift_state(st, kn.reshape(b, t * KV_SLOTS, HEAD_DIM)) for st, kn in zip(kv_state, kv_new)]
        samp_bias = [_sample_bias(bi, w, d, t) for bi, (w, d) in zip(biases, B_GROUPS)]
    new_bufs = [nb.reshape(b, -1, 2, HEADS_PER_GROUP, HEAD_DIM) for nb in new_bufs]

    x3 = x2.reshape(b, t, d_model)
    for i in range(DEPTH - N_A_LAYERS):
        l = N_A_LAYERS + i
        h = _mm(x3.reshape(m, d_model), b_w_in, i).reshape(b, t, -1)
        if kv_state is None:
            o = _b_attn(h, kv, band_bias)
        else:
            parts = [_samp_attn(h, g, kv_state[g], kv, g, vcol0 + g, *samp_bias[g]) for g in range(N_GROUPS)]
            o = _samp_merge(parts, h)
        x3 = _b_out(o, h, mem_kv, l, b_w_out, i, x3, ln_g[l], ln_b[l])
    return x3, jnp.stack(new_conv), new_bufs


def kernel(x_prompt, x_sample, state_conv, state_kv_g0, state_kv_g1, state_kv_g2, cache_mem_kv, mem_prompt,
           a_w_in, a_w_dw, a_b_dw, a_cn_g, a_cn_b, a_w_out, b_w_in, b_w_out, w_kv_shared, w_mem_kv,
           rel_bias, ln_g, ln_b):
    bp = x_prompt.shape[0]
    bs = x_sample.shape[0]
    new_mem_kv = _mm(mem_prompt.reshape(bp * N_MEM, D_MODEL), w_mem_kv).reshape(
        DEPTH, bp, N_MEM, 2, MEM_HEADS, HEAD_DIM)
    mem_p = new_mem_kv.reshape(DEPTH, bp, N_MEM * KV_SLOTS, HEAD_DIM)
    conv_zero = jnp.zeros((N_A_LAYERS, bp, CONV_WIDTH - 1, CONV_CH), x_prompt.dtype)
    weights = (a_w_in, a_w_dw, a_b_dw, a_cn_g, a_cn_b, a_w_out, b_w_in, b_w_out, w_kv_shared, rel_bias, ln_g, ln_b)

    y_p, conv_p, bufs_p = _trunk(x_prompt, conv_zero, mem_p, None, *weights)
    kv_state = [s.reshape(bs, s.shape[1] * KV_SLOTS, HEAD_DIM) for s in (state_kv_g0, state_kv_g1, state_kv_g2)]
    mem_s = cache_mem_kv.reshape(DEPTH, bs, N_MEM * KV_SLOTS, HEAD_DIM)
    y_s, conv_s, bufs_s = _trunk(x_sample, state_conv, mem_s, kv_state, *weights)

    return (y_p, y_s, conv_p, conv_s, bufs_p[0], bufs_s[0], bufs_p[1], bufs_s[1], bufs_p[2], bufs_s[2], new_mem_kv)
```

```python
import functools
import math

import jax
import jax.numpy as jnp
from jax.experimental import pallas as pl
from jax.experimental.pallas import tpu as pltpu

D_MODEL = 2048
DEPTH = 4
HEAD_DIM = 128
MEM_HEADS = 4
MEM_WIDTH = MEM_HEADS * HEAD_DIM
N_MEM = 256
MIX_WIDTH = D_MODEL - MEM_WIDTH
N_A_LAYERS = DEPTH // 2
CONV_CH = MIX_WIDTH
CONV_WIDTH = 31
B_GROUPS = ((128, 1), (512, 4), (2048, 16))
N_GROUPS = len(B_GROUPS)
HEADS_PER_GROUP = 4
B_HEADS = N_GROUPS * HEADS_PER_GROUP
GROUP_WIDTH = HEADS_PER_GROUP * HEAD_DIM
KV_SLOTS = 2 * HEADS_PER_GROUP
B_OUT = GROUP_WIDTH
REL_BUCKETS = 32
REL_MAX_DIST = 2048
BAND = 128
ATTN_TOKENS = BAND * max(d for _, d in B_GROUPS)
LN_EPS = 1e-5
ALPHA = (2 * DEPTH) ** 0.25
NEG_INF = -1e30
SCALE = HEAD_DIM ** -0.5

HALO = 32
VMEM_LIMIT = 56 * 1024 * 1024
BF16 = jnp.bfloat16
F32 = jnp.float32


def _params(*sem):
    return pltpu.CompilerParams(dimension_semantics=sem, vmem_limit_bytes=VMEM_LIMIT)


def _sigmoid(x):
    return 1.0 / (1.0 + jnp.exp(-x))


def _silu(x):
    return x * _sigmoid(x)


def _layernorm(z, g, b):
    mu = jnp.mean(z, -1, keepdims=True)
    zc = z - mu
    var = jnp.mean(zc * zc, -1, keepdims=True)
    return zc * jax.lax.rsqrt(var + LN_EPS) * g + b


def _mm_kernel(x_ref, w_ref, o_ref, xb_ref):
    @pl.when(pl.program_id(2) == 0)
    def _():
        xb_ref[...] = x_ref[...].astype(BF16)

    o_ref[...] = jnp.dot(xb_ref[...], w_ref[...].astype(BF16), preferred_element_type=F32)


def _mm(x, w, layer=None, *, tn=512):
    m, k = x.shape
    n = w.shape[2]
    tm = min(m, 1024)
    l0, nl = (0, w.shape[0]) if layer is None else (layer, 1)
    out = pl.pallas_call(
        _mm_kernel,
        grid=(nl, m // tm, n // tn),
        in_specs=[pl.BlockSpec((tm, k), lambda l, i, j: (i, 0)),
                  pl.BlockSpec((None, k, tn), lambda l, i, j: (l0 + l, 0, j))],
        out_specs=pl.BlockSpec((None, tm, tn), lambda l, i, j: (l, i, j)),
        out_shape=jax.ShapeDtypeStruct((nl, m, n), F32),
        scratch_shapes=[pltpu.VMEM((tm, k), BF16)],
        compiler_params=_params("arbitrary", "arbitrary", "arbitrary"),
        name="mm",
    )(x, w)
    return out if layer is None else out.reshape(m, n)


def _out_ln_kernel(cm_ref, w_ref, x_ref, g_ref, b_ref, o_ref, wb_ref):
    @pl.when(pl.program_id(0) == 0)
    def _():
        wb_ref[...] = w_ref[...].astype(BF16)

    y = jnp.dot(cm_ref[...], wb_ref[...], preferred_element_type=F32)
    o_ref[...] = _layernorm(ALPHA * x_ref[...] + y, g_ref[...], b_ref[...])


def _out_ln(cm, w, layer, x, g, b):
    m, kk = cm.shape
    d = w.shape[2]
    tm = min(m, 512)
    return pl.pallas_call(
        _out_ln_kernel,
        grid=(m // tm,),
        in_specs=[pl.BlockSpec((tm, kk), lambda i: (i, 0)),
                  pl.BlockSpec((None, kk, d), lambda i: (layer, 0, 0), pipeline_mode=pl.Buffered(1)),
                  pl.BlockSpec((tm, d), lambda i: (i, 0)),
                  pl.BlockSpec((1, d), lambda i: (0, 0)),
                  pl.BlockSpec((1, d), lambda i: (0, 0))],
        out_specs=pl.BlockSpec((tm, d), lambda i: (i, 0)),
        out_shape=jax.ShapeDtypeStruct((m, d), F32),
        scratch_shapes=[pltpu.VMEM((kk, d), BF16)],
        compiler_params=_params("arbitrary"),
        name="out_ln",
    )(cm, w, x, g.reshape(1, d), b.reshape(1, d))


def _kv_rows(ref, lead, slot, n):
    return ref[lead + (pl.ds(slot, n, stride=KV_SLOTS), slice(None))]


def _mem_attn(qm, gm, mem_ref):
    outs = []
    for h in range(MEM_HEADS):
        lo = h * HEAD_DIM
        q = qm[:, lo:lo + HEAD_DIM].astype(BF16)
        k = _kv_rows(mem_ref, (0,), h, N_MEM).astype(BF16)
        v = _kv_rows(mem_ref, (0,), MEM_HEADS + h, N_MEM).astype(BF16)
        s = jax.lax.dot_general(q, k, (((1,), (1,)), ((), ())), preferred_element_type=F32) * SCALE
        s = s - jnp.max(s, -1, keepdims=True)
        e = jnp.exp(s)
        p = e / jnp.sum(e, -1, keepdims=True)
        outs.append(jnp.dot(p.astype(BF16), v, preferred_element_type=F32))
    return jnp.concatenate(outs, -1) * _silu(gm)


def _a_mix_kernel(*refs, tt, tr, has_halo):
    if has_halo:
        (ga, gb, gate, qm, gm, gap, gbp, cprev, mem, wdw, bdw, cng, cnb,
         cm_out, nc_out, ext, sh, conv) = refs
    else:
        (ga, gb, gate, qm, gm, cprev, mem, wdw, bdw, cng, cnb,
         cm_out, nc_out, ext, sh, conv) = refs
    i = pl.program_id(1)

    u = ga[0] * _sigmoid(gb[0])
    if has_halo:
        prev_u = gap[0] * _sigmoid(gbp[0])
        halo = jnp.where(i == 0, cprev[0], prev_u)
    else:
        halo = cprev[0]
    ext[0:HALO, :] = halo
    ext[HALO:HALO + tt, :] = u

    nc_out[0] = ext[HALO + tt - (CONV_WIDTH - 1):HALO + tt, :]

    n_sh = tt + HALO - 8
    sh[0, :, :] = ext[...]
    for s in range(1, 8):
        sh[s, 0:n_sh, :] = ext[s:s + n_sh, :]

    off0 = HALO - (CONV_WIDTH - 1)
    lane_chunk = 512
    for c in range(CONV_CH // lane_chunk):
        cl = c * lane_chunk

        def body(rc, carry, cl=cl):
            r0 = pl.multiple_of(rc * tr, 8)
            acc = jnp.zeros((tr, lane_chunk), F32)
            for k in range(CONV_WIDTH):
                o = k + off0
                a, s = divmod(o, 8)
                win = sh[s, pl.ds(pl.multiple_of(r0 + 8 * a, 8), tr), cl:cl + lane_chunk]
                acc = acc + win * wdw[k:k + 1, cl:cl + lane_chunk]
            conv[pl.ds(r0, tr), cl:cl + lane_chunk] = acc + bdw[:, cl:cl + lane_chunk]
            return carry

        jax.lax.fori_loop(0, tt // tr, body, 0)

    zn = _layernorm(conv[...], cng[...], cnb[...])
    cm_out[0, :, 0:CONV_CH] = (_silu(zn) * _silu(gate[0])).astype(BF16)
    cm_out[0, :, CONV_CH:D_MODEL] = _mem_attn(qm[0], gm[0], mem).astype(BF16)


def _a_mix(h, conv_prev, mem, layer, w_dw, b_dw, cn_g, cn_b):
    b, t, _ = h.shape
    tt = min(t, 256)
    tr = min(tt, 32)
    nt = t // tt
    has_halo = nt > 1
    cprev = jnp.pad(conv_prev, ((0, 0), (HALO - (CONV_WIDTH - 1), 0), (0, 0)))
    c3 = CONV_CH
    qcol = 3 * CONV_CH // MEM_WIDTH

    in_specs = [pl.BlockSpec((1, tt, c3), lambda bi, i: (bi, i, 0)),
                pl.BlockSpec((1, tt, c3), lambda bi, i: (bi, i, 1)),
                pl.BlockSpec((1, tt, c3), lambda bi, i: (bi, i, 2)),
                pl.BlockSpec((1, tt, MEM_WIDTH), lambda bi, i: (bi, i, qcol)),
                pl.BlockSpec((1, tt, MEM_WIDTH), lambda bi, i: (bi, i, qcol + 1))]
    args = [h, h, h, h, h]
    if has_halo:
        per = tt // HALO
        in_specs += [pl.BlockSpec((1, HALO, c3), lambda bi, i: (bi, jnp.maximum(i * per - 1, 0), 0)),
                     pl.BlockSpec((1, HALO, c3), lambda bi, i: (bi, jnp.maximum(i * per - 1, 0), 1))]
        args += [h, h]
    in_specs += [pl.BlockSpec((1, HALO, c3), lambda bi, i: (bi, 0, 0)),
                 pl.BlockSpec((None, 1, N_MEM * KV_SLOTS, HEAD_DIM), lambda bi, i: (layer, bi, 0, 0)),
                 pl.BlockSpec((CONV_WIDTH, c3), lambda bi, i: (0, 0)),
                 pl.BlockSpec((1, c3), lambda bi, i: (0, 0)),
                 pl.BlockSpec((1, c3), lambda bi, i: (0, 0)),
                 pl.BlockSpec((1, c3), lambda bi, i: (0, 0))]
    args += [cprev, mem, w_dw, b_dw.reshape(1, c3), cn_g.reshape(1, c3), cn_b.reshape(1, c3)]

    return pl.pallas_call(
        functools.partial(_a_mix_kernel, tt=tt, tr=tr, has_halo=has_halo),
        grid=(b, nt),
        in_specs=in_specs,
        out_specs=[pl.BlockSpec((1, tt, D_MODEL), lambda bi, i: (bi, i, 0)),
                   pl.BlockSpec((1, CONV_WIDTH - 1, c3), lambda bi, i: (bi, 0, 0))],
        out_shape=[jax.ShapeDtypeStruct((b, t, D_MODEL), BF16),
                   jax.ShapeDtypeStruct((b, CONV_WIDTH - 1, c3), F32)],
        scratch_shapes=[pltpu.VMEM((tt + HALO, c3), F32),
                        pltpu.VMEM((8, tt + HALO, c3), F32),
                        pltpu.VMEM((tt, c3), F32)],
        compiler_params=_params("parallel", "arbitrary"),
        name="a_mix",
    )(*args)


def _rows(ref, start, size, stride):
    if stride == 1:
        return ref[0, start:start + size, :]
    return ref[0, pl.ds(start, size, stride=stride), :]


def _b_attn_kernel(q0, q1, q2, k0, k1, k2, v0, v1, v2, hk0, hk1, hk2, hv0, hv1, hv2, bias_ref, gate_ref,
                   o_ref, num_scr, m_scr, s_scr):
    first = jnp.minimum(pl.program_id(2), 1)
    qs, ks, vs = (q0, q1, q2), (k0, k1, k2), (v0, v1, v2)
    hks, hvs = (hk0, hk1, hk2), (hv0, hv1, hv2)
    dn = (((1,), (1,)), ((), ()))
    for g, (_, d) in enumerate(B_GROUPS):
        for r in range(d):
            for s in range(ATTN_TOKENS // (BAND * d)):
                start = r + d * BAND * s
                q = _rows(qs[g], start, BAND, d)
                if s == 0:
                    k = jnp.concatenate([_rows(hks[g], r, BAND, d), _rows(ks[g], r, BAND, d)], 0)
                    v = jnp.concatenate([_rows(hvs[g], r, BAND, d), _rows(vs[g], r, BAND, d)], 0)
                    bias = bias_ref[g, first, 0]
                else:
                    k = _rows(ks[g], start - d * BAND, 2 * BAND, d)
                    v = _rows(vs[g], start - d * BAND, 2 * BAND, d)
                    bias = bias_ref[g, 1, 0]
                logits = jax.lax.dot_general(q.astype(BF16), k.astype(BF16), dn, preferred_element_type=F32)
                logits = logits * SCALE + bias
                m = jnp.max(logits, -1, keepdims=True)
                e = jnp.exp(logits - m)
                ssum = jnp.sum(e, -1, keepdims=True)
                num = jnp.dot(e.astype(BF16), v.astype(BF16), preferred_element_type=F32)
                if d == 1:
                    dst = pl.ds(start, BAND)
                else:
                    dst = pl.ds(start, BAND, stride=d)
                num_scr[g, dst, :] = num
                m_scr[g, dst, :] = jnp.broadcast_to(m, (BAND, HEAD_DIM))
                s_scr[g, dst, :] = jnp.broadcast_to(ssum, (BAND, HEAD_DIM))

    def merge(c, carry):
        sl = pl.ds(pl.multiple_of(c * BAND, BAND), BAND)
        ms = [m_scr[g, sl, :] for g in range(N_GROUPS)]
        m_all = jnp.maximum(jnp.maximum(ms[0], ms[1]), ms[2])
        coef = [jnp.exp(m - m_all) for m in ms]
        num = coef[0] * num_scr[0, sl, :]
        den = coef[0] * s_scr[0, sl, :]
        for g in range(1, N_GROUPS):
            num = num + coef[g] * num_scr[g, sl, :]
            den = den + coef[g] * s_scr[g, sl, :]
        o_ref[0, sl, :] = (num / den * _silu(gate_ref[0, sl, :])).astype(BF16)
        return carry

    jax.lax.fori_loop(0, ATTN_TOKENS // BAND, merge, 0)


def _b_attn(h, kv, band_bias):
    b, t, _ = h.shape
    tb = ATTN_TOKENS
    blk = (1, tb, HEAD_DIM)

    def cur(col0):
        return lambda bi, hi, i: (bi, i, col0 + hi)

    def halo_spec(d, col0):
        rows = BAND * d
        per = tb // rows
        return pl.BlockSpec((1, rows, HEAD_DIM), lambda bi, hi, i: (bi, jnp.maximum(i * per - 1, 0), col0 + hi))

    hpg = HEADS_PER_GROUP
    in_specs = ([pl.BlockSpec(blk, cur(hpg * g)) for g in range(N_GROUPS)]
                + [pl.BlockSpec(blk, cur(hpg * g)) for g in range(N_GROUPS)]
                + [pl.BlockSpec(blk, cur(B_HEADS + hpg * g)) for g in range(N_GROUPS)]
                + [halo_spec(d, hpg * g) for g, (_, d) in enumerate(B_GROUPS)]
                + [halo_spec(d, B_HEADS + hpg * g) for g, (_, d) in enumerate(B_GROUPS)]
                + [pl.BlockSpec((N_GROUPS, 2, 1, BAND, 2 * BAND), lambda bi, hi, i: (0, 0, hi, 0, 0)),
                   pl.BlockSpec(blk, cur(B_HEADS))])
    return pl.pallas_call(
        _b_attn_kernel,
        grid=(b, hpg, t // tb),
        in_specs=in_specs,
        out_specs=pl.BlockSpec(blk, lambda bi, hi, i: (bi, i, hi)),
        out_shape=jax.ShapeDtypeStruct((b, t, B_OUT), BF16),
        scratch_shapes=[pltpu.VMEM((N_GROUPS, tb, HEAD_DIM), F32)] * 3,
        compiler_params=_params("parallel", "parallel", "arbitrary"),
        name="b_attn",
    )(h, h, h, kv, kv, kv, kv, kv, kv, kv, kv, kv, kv, kv, kv, band_bias, h)


def _b_out_kernel(o_ref, qm_ref, gm_ref, mem_ref, w_ref, x_ref, g_ref, b_ref, y_ref, wb_ref):
    @pl.when((pl.program_id(0) == 0) & (pl.program_id(1) == 0))
    def _():
        wb_ref[...] = w_ref[...].astype(BF16)

    mbranch = _mem_attn(qm_ref[0], gm_ref[0], mem_ref)
    y = jnp.dot(o_ref[0], wb_ref[0:B_OUT, :], preferred_element_type=F32)
    y = y + jnp.dot(mbranch.astype(BF16), wb_ref[B_OUT:B_OUT + MEM_WIDTH, :], preferred_element_type=F32)
    y_ref[0] = _layernorm(ALPHA * x_ref[0] + y, g_ref[...], b_ref[...])


def _b_out(o, h, mem, layer, w, wl, x, g, b):
    bsz, t, d = x.shape
    tt = min(t, 512)
    kk = w.shape[1]
    qcol = (MIX_WIDTH + B_OUT) // MEM_WIDTH
    return pl.pallas_call(
        _b_out_kernel,
        grid=(bsz, t // tt),
        in_specs=[pl.BlockSpec((1, tt, B_OUT), lambda bi, i: (bi, i, 0)),
                  pl.BlockSpec((1, tt, MEM_WIDTH), lambda bi, i: (bi, i, qcol)),
                  pl.BlockSpec((1, tt, MEM_WIDTH), lambda bi, i: (bi, i, qcol + 1)),
                  pl.BlockSpec((None, 1, N_MEM * KV_SLOTS, HEAD_DIM), lambda bi, i: (layer, bi, 0, 0)),
                  pl.BlockSpec((None, kk, d), lambda bi, i: (wl, 0, 0), pipeline_mode=pl.Buffered(1)),
                  pl.BlockSpec((1, tt, d), lambda bi, i: (bi, i, 0)),
                  pl.BlockSpec((1, d), lambda bi, i: (0, 0)),
                  pl.BlockSpec((1, d), lambda bi, i: (0, 0))],
        out_specs=pl.BlockSpec((1, tt, d), lambda bi, i: (bi, i, 0)),
        out_shape=jax.ShapeDtypeStruct((bsz, t, d), F32),
        scratch_shapes=[pltpu.VMEM((kk, d), BF16)],
        compiler_params=_params("arbitrary", "arbitrary"),
        name="b_out",
    )(o, h, h, mem, w, x, g.reshape(1, d), b.reshape(1, d))


def _samp_attn_kernel(q_ref, st_ref, kn_ref, vn_ref, bs_ref, bn_ref, num_ref, m_ref, s_ref, *, nq, w):
    dn = (((1,), (1,)), ((), ()))
    for h in range(HEADS_PER_GROUP):
        lo = h * HEAD_DIM
        q = q_ref[0, :, lo:lo + HEAD_DIM].astype(BF16)
        ks = _kv_rows(st_ref, (0,), h, w).astype(BF16)
        vs = _kv_rows(st_ref, (0,), HEADS_PER_GROUP + h, w).astype(BF16)
        kn = kn_ref[0, :, lo:lo + HEAD_DIM].astype(BF16)
        vn = vn_ref[0, :, lo:lo + HEAD_DIM].astype(BF16)
        ls = jax.lax.dot_general(q, ks, dn, preferred_element_type=F32) * SCALE + bs_ref[h]
        ln = jax.lax.dot_general(q, kn, dn, preferred_element_type=F32) * SCALE + bn_ref[h]
        m = jnp.maximum(jnp.max(ls, -1, keepdims=True), jnp.max(ln, -1, keepdims=True))
        es = jnp.exp(ls - m)
        en = jnp.exp(ln - m)
        ssum = jnp.sum(es, -1, keepdims=True) + jnp.sum(en, -1, keepdims=True)
        num_ref[0, :, lo:lo + HEAD_DIM] = (jnp.dot(es.astype(BF16), vs, preferred_element_type=F32)
                                           + jnp.dot(en.astype(BF16), vn, preferred_element_type=F32))
        m_ref[0, :, lo:lo + HEAD_DIM] = jnp.broadcast_to(m, (nq, HEAD_DIM))
        s_ref[0, :, lo:lo + HEAD_DIM] = jnp.broadcast_to(ssum, (nq, HEAD_DIM))


def _samp_attn(h, qcol, state, kv, kcol, vcol, bias_state, bias_new):
    b, nq, _ = h.shape
    w = state.shape[1] // KV_SLOTS
    blk = (1, nq, GROUP_WIDTH)
    out = jax.ShapeDtypeStruct((b, nq, GROUP_WIDTH), F32)
    return pl.pallas_call(
        functools.partial(_samp_attn_kernel, nq=nq, w=w),
        grid=(b,),
        in_specs=[pl.BlockSpec(blk, lambda i: (i, 0, qcol)),
                  pl.BlockSpec((1, w * KV_SLOTS, HEAD_DIM), lambda i: (i, 0, 0)),
                  pl.BlockSpec(blk, lambda i: (i, 0, kcol)),
                  pl.BlockSpec(blk, lambda i: (i, 0, vcol)),
                  pl.BlockSpec((HEADS_PER_GROUP, nq, w), lambda i: (0, 0, 0)),
                  pl.BlockSpec((HEADS_PER_GROUP, nq, nq), lambda i: (0, 0, 0))],
        out_specs=[pl.BlockSpec(blk, lambda i: (i, 0, 0))] * 3,
        out_shape=[out] * 3,
        compiler_params=_params("parallel"),
        name="samp_attn",
    )(h, state, kv, kv, bias_state, bias_new)


def _samp_merge_kernel(n0, m0, s0, n1, m1, s1, n2, m2, s2, gate, o_out):
    ms = [m0[0], m1[0], m2[0]]
    m_all = jnp.maximum(jnp.maximum(ms[0], ms[1]), ms[2])
    coef = [jnp.exp(m - m_all) for m in ms]
    num = coef[0] * n0[0] + coef[1] * n1[0] + coef[2] * n2[0]
    den = coef[0] * s0[0] + coef[1] * s1[0] + coef[2] * s2[0]
    o_out[0] = (num / den * _silu(gate[0])).astype(BF16)


def _samp_merge(parts, h):
    b, nq, _ = h.shape
    spec = pl.BlockSpec((1, nq, GROUP_WIDTH), lambda i: (i, 0, 0))
    flat = [a for p in parts for a in p]
    return pl.pallas_call(
        _samp_merge_kernel,
        grid=(b,),
        in_specs=[spec] * 9 + [pl.BlockSpec((1, nq, GROUP_WIDTH), lambda i: (i, 0, MIX_WIDTH // GROUP_WIDTH))],
        out_specs=spec,
        out_shape=jax.ShapeDtypeStruct((b, nq, GROUP_WIDTH), BF16),
        compiler_params=_params("parallel"),
        name="samp_merge",
    )(*flat, h)


def _shift_state_kernel(main_ref, next_ref, new_ref, o_ref, *, rows, shift):
    last = pl.program_id(1) == pl.num_programs(1) - 1
    if rows > shift:
        o_ref[0, 0:rows - shift, :] = main_ref[0, shift:rows, :]
    o_ref[0, rows - shift:rows, :] = jnp.where(last, new_ref[0], next_ref[0])


def _shift_state(state, new):
    b, total, lanes = state.shape
    shift = new.shape[1]
    rows = min(total, 4096)
    nblk = total // rows
    per = rows // shift
    return pl.pallas_call(
        functools.partial(_shift_state_kernel, rows=rows, shift=shift),
        grid=(b, nblk),
        in_specs=[pl.BlockSpec((1, rows, lanes), lambda bi, i: (bi, i, 0)),
                  pl.BlockSpec((1, shift, lanes), lambda bi, i: (bi, jnp.minimum((i + 1) * per, nblk * per - 1), 0)),
                  pl.BlockSpec((1, shift, lanes), lambda bi, i: (bi, 0, 0))],
        out_specs=pl.BlockSpec((1, rows, lanes), lambda bi, i: (bi, i, 0)),
        out_shape=jax.ShapeDtypeStruct(state.shape, state.dtype),
        compiler_params=_params("parallel", "arbitrary"),
        name="shift_state",
    )(state, state, new)


def _t5_bucket(dist):
    max_exact = REL_BUCKETS // 2
    safe = jnp.maximum(dist, 1).astype(F32)
    large = max_exact + (jnp.log(safe / max_exact) / math.log(REL_MAX_DIST / max_exact)
                         * (REL_BUCKETS - max_exact)).astype(jnp.int32)
    large = jnp.minimum(large, REL_BUCKETS - 1)
    return jnp.where(dist < max_exact, dist, large)


def _group_bias(rel_bias, g):
    w, d = B_GROUPS[g]
    dist = d * jnp.arange(w // d + 1, dtype=jnp.int32)
    tab = rel_bias[_t5_bucket(dist)]
    return tab[:, g * HEADS_PER_GROUP:(g + 1) * HEADS_PER_GROUP].T.astype(F32)


def _toeplitz(vec, rows, cols):
    hh, p = vec.shape
    flat = jnp.tile(vec, (1, rows))[:, :rows * (p - 1)]
    return flat.reshape(hh, rows, p - 1)[:, :, :cols]


def _band_bias(bias):
    period = 3 * BAND
    vec = jnp.full((bias.shape[0], period), NEG_INF, F32)
    vec = jax.lax.dynamic_update_slice(vec, bias[:, ::-1], (0, 0))
    mat = _toeplitz(vec, BAND, 2 * BAND)
    c = jnp.arange(2 * BAND, dtype=jnp.int32)[None, None, :]
    first = jnp.where(c >= BAND, mat, NEG_INF)
    return jnp.stack([first, mat])


def _sample_bias(bias, w, d, nq):
    hh = bias.shape[0]
    neg = jnp.full_like(bias, NEG_INF)
    by_dist = jnp.stack([bias] + [neg] * (d - 1), -1).reshape(hh, -1)
    by_dist = jnp.concatenate([by_dist[:, :w + 1], jnp.full((hh, nq), NEG_INF, F32)], 1)
    rev = by_dist[:, ::-1]
    bs = jnp.stack([rev[:, nq - n:nq - n + w] for n in range(nq)], 1)
    small = jnp.concatenate([jnp.full((hh, nq), NEG_INF, F32), by_dist[:, :nq]], 1)[:, ::-1]
    bn = jnp.stack([small[:, nq - 1 - n:2 * nq - 1 - n] for n in range(nq)], 1)
    return bs, bn


def _trunk(x, conv_prev, mem_kv, kv_state, a_w_in, a_w_dw, a_b_dw, a_cn_g, a_cn_b, a_w_out,
           b_w_in, b_w_out, w_kv_shared, rel_bias, ln_g, ln_b):
    b, t, d_model = x.shape
    m = b * t
    x2 = x.reshape(m, d_model)
    new_conv = []
    for l in range(N_A_LAYERS):
        h = _mm(x2, a_w_in, l).reshape(b, t, -1)
        cm, nc = _a_mix(h, conv_prev[l], mem_kv, l, a_w_dw[l], a_b_dw[l], a_cn_g[l], a_cn_b[l])
        new_conv.append(nc)
        x2 = _out_ln(cm.reshape(m, d_model), a_w_out, l, x2, ln_g[l], ln_b[l])

    kv = _mm(x2, w_kv_shared[None], 0).reshape(b, t, 2 * MIX_WIDTH)
    vcol0 = MIX_WIDTH // GROUP_WIDTH
    biases = [_group_bias(rel_bias, g) for g in range(N_GROUPS)]
    kv_new = [jnp.concatenate([kv[:, t - min(w, t):, g * GROUP_WIDTH:(g + 1) * GROUP_WIDTH],
                               kv[:, t - min(w, t):, MIX_WIDTH + g * GROUP_WIDTH:MIX_WIDTH + (g + 1) * GROUP_WIDTH]], -1)
              for g, (w, _) in enumerate(B_GROUPS)]
    if kv_state is None:
        new_bufs = kv_new
        band_bias = jnp.stack([_band_bias(bi) for bi in biases])
    else:
        new_bufs = [_shift_state(st, kn.reshape(b, t * KV_SLOTS, HEAD_DIM)) for st, kn in zip(kv_state, kv_new)]
        samp_bias = [_sample_bias(bi, w, d, t) for bi, (w, d) in zip(biases, B_GROUPS)]
    new_bufs = [nb.reshape(b, -1, 2, HEADS_PER_GROUP, HEAD_DIM) for nb in new_bufs]

    x3 = x2.reshape(b, t, d_model)
    for i in range(DEPTH - N_A_LAYERS):
        l = N_A_LAYERS + i
        h = _mm(x3.reshape(m, d_model), b_w_in, i).reshape(b, t, -1)
        if kv_state is None:
            o = _b_attn(h, kv, band_bias)
        else:
            parts = [_samp_attn(h, g, kv_state[g], kv, g, vcol0 + g, *samp_bias[g]) for g in range(N_GROUPS)]
            o = _samp_merge(parts, h)
        x3 = _b_out(o, h, mem_kv, l, b_w_out, i, x3, ln_g[l], ln_b[l])
    return x3, jnp.stack(new_conv), new_bufs


def kernel(x_prompt, x_sample, state_conv, state_kv_g0, state_kv_g1, state_kv_g2, cache_mem_kv, mem_prompt,
           a_w_in, a_w_dw, a_b_dw, a_cn_g, a_cn_b, a_w_out, b_w_in, b_w_out, w_kv_shared, w_mem_kv,
           rel_bias, ln_g, ln_b):
    bp = x_prompt.shape[0]
    bs = x_sample.shape[0]
    new_mem_kv = _mm(mem_prompt.reshape(bp * N_MEM, D_MODEL), w_mem_kv).reshape(
        DEPTH, bp, N_MEM, 2, MEM_HEADS, HEAD_DIM)
    mem_p = new_mem_kv.reshape(DEPTH, bp, N_MEM * KV_SLOTS, HEAD_DIM)
    conv_zero = jnp.zeros((N_A_LAYERS, bp, CONV_WIDTH - 1, CONV_CH), x_prompt.dtype)
    weights = (a_w_in, a_w_dw, a_b_dw, a_cn_g, a_cn_b, a_w_out, b_w_in, b_w_out, w_kv_shared, rel_bias, ln_g, ln_b)

    y_p, conv_p, bufs_p = _trunk(x_prompt, conv_zero, mem_p, None, *weights)
    kv_state = [s.reshape(bs, s.shape[1] * KV_SLOTS, HEAD_DIM) for s in (state_kv_g0, state_kv_g1, state_kv_g2)]
    mem_s = cache_mem_kv.reshape(DEPTH, bs, N_MEM * KV_SLOTS, HEAD_DIM)
    y_s, conv_s, bufs_s = _trunk(x_sample, state_conv, mem_s, kv_state, *weights)

    return (y_p, y_s, conv_p, conv_s, bufs_p[0], bufs_s[0], bufs_p[1], bufs_s[1], bufs_p[2], bufs_s[2], new_mem_kv)
```

```python
import functools
import math

import jax
import jax.numpy as jnp
from jax.experimental import pallas as pl
from jax.experimental.pallas import tpu as pltpu

D_MODEL = 2048
DEPTH = 4
HEAD_DIM = 128
MEM_HEADS = 4
MEM_WIDTH = MEM_HEADS * HEAD_DIM
N_MEM = 256
MIX_WIDTH = D_MODEL - MEM_WIDTH
N_A_LAYERS = DEPTH // 2
CONV_CH = MIX_WIDTH
CONV_WIDTH = 31
B_GROUPS = ((128, 1), (512, 4), (2048, 16))
N_GROUPS = len(B_GROUPS)
HEADS_PER_GROUP = 4
B_HEADS = N_GROUPS * HEADS_PER_GROUP
GROUP_WIDTH = HEADS_PER_GROUP * HEAD_DIM
KV_SLOTS = 2 * HEADS_PER_GROUP
B_OUT = GROUP_WIDTH
REL_BUCKETS = 32
REL_MAX_DIST = 2048
BAND = 128
ATTN_TOKENS = BAND * max(d for _, d in B_GROUPS)
LN_EPS = 1e-5
ALPHA = (2 * DEPTH) ** 0.25
NEG_INF = -1e30
SCALE = HEAD_DIM ** -0.5

HALO = 32
VMEM_LIMIT = 56 * 1024 * 1024
BF16 = jnp.bfloat16
F32 = jnp.float32


def _params(*sem):
    return pltpu.CompilerParams(dimension_semantics=sem, vmem_limit_bytes=VMEM_LIMIT)


def _sigmoid(x):
    return 1.0 / (1.0 + jnp.exp(-x))


def _silu(x):
    return x * _sigmoid(x)


def _layernorm(z, g, b):
    mu = jnp.mean(z, -1, keepdims=True)
    zc = z - mu
    var = jnp.mean(zc * zc, -1, keepdims=True)
    return zc * jax.lax.rsqrt(var + LN_EPS) * g + b


def _lhs_bf16(x_ref, scratch, col_axis):
    if not scratch:
        return x_ref[...]
    xb_ref, = scratch

    @pl.when(pl.program_id(col_axis) == 0)
    def _():
        xb_ref[...] = x_ref[...].astype(BF16)

    return xb_ref[...]


def _row_tile(x):
    m, k = x.shape
    if x.dtype == BF16:
        return min(m, 2048), []
    return min(m, 1024), [pltpu.VMEM((min(m, 1024), k), BF16)]


def _mm_kernel(x_ref, w_ref, o_ref, *scratch):
    xb = _lhs_bf16(x_ref, scratch, 2)
    o_ref[...] = jnp.dot(xb, w_ref[...].astype(BF16), preferred_element_type=F32)


def _mm(x, w, layer=None, *, tn=512):
    m, k = x.shape
    n = w.shape[2]
    tm, scratch = _row_tile(x)
    l0, nl = (0, w.shape[0]) if layer is None else (layer, 1)
    out = pl.pallas_call(
        _mm_kernel,
        grid=(nl, m // tm, n // tn),
        in_specs=[pl.BlockSpec((tm, k), lambda l, i, j: (i, 0)),
                  pl.BlockSpec((None, k, tn), lambda l, i, j: (l0 + l, 0, j))],
        out_specs=pl.BlockSpec((None, tm, tn), lambda l, i, j: (l, i, j)),
        out_shape=jax.ShapeDtypeStruct((nl, m, n), F32),
        scratch_shapes=scratch,
        compiler_params=_params("arbitrary", "arbitrary", "arbitrary"),
        name="mm",
    )(x, w)
    return out if layer is None else out.reshape(m, n)


A_TN = 256
N_GLU = CONV_CH // A_TN
N_QM = MEM_WIDTH // A_TN


def _a_proj_kernel(x_ref, wa_ref, wb_ref, u_ref, g_ref, *scratch):
    j = pl.program_id(1)
    xb = _lhs_bf16(x_ref, scratch, 1)

    def proj(w_ref):
        return jnp.dot(xb, w_ref[...].astype(BF16), preferred_element_type=F32)

    @pl.when(j < N_GLU)
    def _():
        u_ref[...] = proj(wa_ref) * _sigmoid(proj(wb_ref))

    is_qm = (j >= 2 * N_GLU) & (j < 2 * N_GLU + N_QM)

    @pl.when(is_qm)
    def _():
        g_ref[...] = proj(wa_ref).astype(BF16)

    @pl.when((j >= N_GLU) & jnp.logical_not(is_qm))
    def _():
        g_ref[...] = _silu(proj(wa_ref)).astype(BF16)


def _a_proj(x, w, layer):
    m, k = x.shape
    tm, scratch = _row_tile(x)
    n_steps = 2 * N_GLU + 2 * N_QM
    return pl.pallas_call(
        _a_proj_kernel,
        grid=(m // tm, n_steps),
        in_specs=[pl.BlockSpec((tm, k), lambda i, j: (i, 0)),
                  pl.BlockSpec((None, k, A_TN), lambda i, j: (layer, 0, jnp.where(j < N_GLU, j, j + N_GLU))),
                  pl.BlockSpec((None, k, A_TN), lambda i, j: (layer, 0, jnp.minimum(j + N_GLU, 2 * N_GLU - 1)))],
        out_specs=[pl.BlockSpec((tm, A_TN), lambda i, j: (i, jnp.minimum(j, N_GLU - 1))),
                   pl.BlockSpec((tm, A_TN), lambda i, j: (i, jnp.maximum(j - N_GLU, 0)))],
        out_shape=[jax.ShapeDtypeStruct((m, CONV_CH), F32),
                   jax.ShapeDtypeStruct((m, CONV_CH + 2 * MEM_WIDTH), BF16)],
        scratch_shapes=scratch,
        compiler_params=_params("arbitrary", "arbitrary"),
        name="a_proj",
    )(x, w, w)


def _out_ln_kernel(cm_ref, w_ref, x_ref, g_ref, b_ref, o_ref, ob_ref, wb_ref):
    @pl.when(pl.program_id(0) == 0)
    def _():
        wb_ref[...] = w_ref[...].astype(BF16)

    y = jnp.dot(cm_ref[...], wb_ref[...], preferred_element_type=F32)
    out = _layernorm(ALPHA * x_ref[...] + y, g_ref[...], b_ref[...])
    o_ref[...] = out
    ob_ref[...] = out.astype(BF16)


def _out_ln(cm, w, layer, x, g, b):
    m, kk = cm.shape
    d = w.shape[2]
    tm = min(m, 512)
    row = pl.BlockSpec((tm, d), lambda i: (i, 0))
    return pl.pallas_call(
        _out_ln_kernel,
        grid=(m // tm,),
        in_specs=[pl.BlockSpec((tm, kk), lambda i: (i, 0)),
                  pl.BlockSpec((None, kk, d), lambda i: (layer, 0, 0), pipeline_mode=pl.Buffered(1)),
                  row,
                  pl.BlockSpec((1, d), lambda i: (0, 0)),
                  pl.BlockSpec((1, d), lambda i: (0, 0))],
        out_specs=[row, row],
        out_shape=[jax.ShapeDtypeStruct((m, d), F32), jax.ShapeDtypeStruct((m, d), BF16)],
        scratch_shapes=[pltpu.VMEM((kk, d), BF16)],
        compiler_params=_params("arbitrary"),
        name="out_ln",
    )(cm, w, x, g.reshape(1, d), b.reshape(1, d))


def _kv_rows(ref, lead, slot, n):
    return ref[lead + (pl.ds(slot, n, stride=KV_SLOTS), slice(None))]


def _mem_attn(qm, mem_ref):
    outs = []
    for h in range(MEM_HEADS):
        lo = h * HEAD_DIM
        q = qm[:, lo:lo + HEAD_DIM]
        k = _kv_rows(mem_ref, (0,), h, N_MEM).astype(BF16)
        v = _kv_rows(mem_ref, (0,), MEM_HEADS + h, N_MEM).astype(BF16)
        s = jax.lax.dot_general(q, k, (((1,), (1,)), ((), ())), preferred_element_type=F32) * SCALE
        s = s - jnp.max(s, -1, keepdims=True)
        e = jnp.exp(s)
        p = e / jnp.sum(e, -1, keepdims=True)
        outs.append(jnp.dot(p.astype(BF16), v, preferred_element_type=F32))
    return jnp.concatenate(outs, -1)


def _a_mix_kernel(*refs, tt, tr, has_halo):
    if has_halo:
        (u_ref, sgate, qm, sgm, u_prev, cprev, mem, wdw, bdw, cng, cnb,
         cm_out, nc_out, ext, sh, conv) = refs
    else:
        (u_ref, sgate, qm, sgm, cprev, mem, wdw, bdw, cng, cnb,
         cm_out, nc_out, ext, sh, conv) = refs
    i = pl.program_id(1)

    if has_halo:
        halo = jnp.where(i == 0, cprev[0], u_prev[0])
    else:
        halo = cprev[0]
    ext[0:HALO, :] = halo
    ext[HALO:HALO + tt, :] = u_ref[0]

    nc_out[0] = ext[HALO + tt - (CONV_WIDTH - 1):HALO + tt, :]

    n_sh = tt + HALO - 8
    sh[0, :, :] = ext[...]
    for s in range(1, 8):
        sh[s, 0:n_sh, :] = ext[s:s + n_sh, :]

    off0 = HALO - (CONV_WIDTH - 1)
    lane_chunk = 256
    for c in range(CONV_CH // lane_chunk):
        cl = c * lane_chunk

        def body(rc, carry, cl=cl):
            r0 = pl.multiple_of(rc * tr, 8)
            acc = jnp.zeros((tr, lane_chunk), F32)
            for k in range(CONV_WIDTH):
                o = k + off0
                a, s = divmod(o, 8)
                win = sh[s, pl.ds(pl.multiple_of(r0 + 8 * a, 8), tr), cl:cl + lane_chunk]
                acc = acc + win * wdw[k:k + 1, cl:cl + lane_chunk]
            conv[pl.ds(r0, tr), cl:cl + lane_chunk] = acc + bdw[:, cl:cl + lane_chunk]
            return carry

        jax.lax.fori_loop(0, tt // tr, body, 0)

    zn = _layernorm(conv[...], cng[...], cnb[...])
    cm_out[0, :, 0:CONV_CH] = (_silu(zn) * sgate[0].astype(F32)).astype(BF16)
    cm_out[0, :, CONV_CH:D_MODEL] = (_mem_attn(qm[0], mem) * sgm[0].astype(F32)).astype(BF16)


def _a_mix(u, gates, conv_prev, mem, layer, w_dw, b_dw, cn_g, cn_b):
    b, t, _ = u.shape
    tt = min(t, 256)
    tr = min(tt, 32)
    nt = t // tt
    has_halo = nt > 1
    cprev = jnp.pad(conv_prev, ((0, 0), (HALO - (CONV_WIDTH - 1), 0), (0, 0)))
    c3 = CONV_CH
    qcol = CONV_CH // MEM_WIDTH

    in_specs = [pl.BlockSpec((1, tt, c3), lambda bi, i: (bi, i, 0)),
                pl.BlockSpec((1, tt, c3), lambda bi, i: (bi, i, 0)),
                pl.BlockSpec((1, tt, MEM_WIDTH), lambda bi, i: (bi, i, qcol)),
                pl.BlockSpec((1, tt, MEM_WIDTH), lambda bi, i: (bi, i, qcol + 1))]
    args = [u, gates, gates, gates]
    if has_halo:
        per = tt // HALO
        in_specs += [pl.BlockSpec((1, HALO, c3), lambda bi, i: (bi, jnp.maximum(i * per - 1, 0), 0))]
        args += [u]
    in_specs += [pl.BlockSpec((1, HALO, c3), lambda bi, i: (bi, 0, 0)),
                 pl.BlockSpec((None, 1, N_MEM * KV_SLOTS, HEAD_DIM), lambda bi, i: (layer, bi, 0, 0)),
                 pl.BlockSpec((CONV_WIDTH, c3), lambda bi, i: (0, 0)),
                 pl.BlockSpec((1, c3), lambda bi, i: (0, 0)),
                 pl.BlockSpec((1, c3), lambda bi, i: (0, 0)),
                 pl.BlockSpec((1, c3), lambda bi, i: (0, 0))]
    args += [cprev, mem, w_dw, b_dw.reshape(1, c3), cn_g.reshape(1, c3), cn_b.reshape(1, c3)]

    return pl.pallas_call(
        functools.partial(_a_mix_kernel, tt=tt, tr=tr, has_halo=has_halo),
        grid=(b, nt),
        in_specs=in_specs,
        out_specs=[pl.BlockSpec((1, tt, D_MODEL), lambda bi, i: (bi, i, 0)),
                   pl.BlockSpec((1, CONV_WIDTH - 1, c3), lambda bi, i: (bi, 0, 0))],
        out_shape=[jax.ShapeDtypeStruct((b, t, D_MODEL), BF16),
                   jax.ShapeDtypeStruct((b, CONV_WIDTH - 1, c3), F32)],
        scratch_shapes=[pltpu.VMEM((tt + HALO, c3), F32),
                        pltpu.VMEM((8, tt + HALO, c3), F32),
                        pltpu.VMEM((tt, c3), F32)],
        compiler_params=_params("parallel", "arbitrary"),
        name="a_mix",
    )(*args)


def _rows(ref, start, size, stride):
    if stride == 1:
        return ref[0, start:start + size, :]
    return ref[0, pl.ds(start, size, stride=stride), :]


def _b_attn_kernel(q0, q1, q2, k0, k1, k2, v0, v1, v2, hk0, hk1, hk2, hv0, hv1, hv2, bias_ref, gate_ref,
                   o_ref, num_scr, m_scr, s_scr):
    first = jnp.minimum(pl.program_id(2), 1)
    qs, ks, vs = (q0, q1, q2), (k0, k1, k2), (v0, v1, v2)
    hks, hvs = (hk0, hk1, hk2), (hv0, hv1, hv2)
    dn = (((1,), (1,)), ((), ()))
    for g, (_, d) in enumerate(B_GROUPS):
        for r in range(d):
            for s in range(ATTN_TOKENS // (BAND * d)):
                start = r + d * BAND * s
                q = _rows(qs[g], start, BAND, d)
                if s == 0:
                    k = jnp.concatenate([_rows(hks[g], r, BAND, d), _rows(ks[g], r, BAND, d)], 0)
                    v = jnp.concatenate([_rows(hvs[g], r, BAND, d), _rows(vs[g], r, BAND, d)], 0)
                    bias = bias_ref[g, first, 0]
                else:
                    k = _rows(ks[g], start - d * BAND, 2 * BAND, d)
                    v = _rows(vs[g], start - d * BAND, 2 * BAND, d)
                    bias = bias_ref[g, 1, 0]
                logits = jax.lax.dot_general(q.astype(BF16), k.astype(BF16), dn, preferred_element_type=F32)
                logits = logits * SCALE + bias
                m = jnp.max(logits, -1, keepdims=True)
                e = jnp.exp(logits - m)
                ssum = jnp.sum(e, -1, keepdims=True)
                num = jnp.dot(e.astype(BF16), v.astype(BF16), preferred_element_type=F32)
                if d == 1:
                    dst = pl.ds(start, BAND)
                else:
                    dst = pl.ds(start, BAND, stride=d)
                num_scr[g, dst, :] = num
                m_scr[g, dst, :] = jnp.broadcast_to(m, (BAND, HEAD_DIM))
                s_scr[g, dst, :] = jnp.broadcast_to(ssum, (BAND, HEAD_DIM))

    def merge(c, carry):
        sl = pl.ds(pl.multiple_of(c * BAND, BAND), BAND)
        ms = [m_scr[g, sl, :] for g in range(N_GROUPS)]
        m_all = jnp.maximum(jnp.maximum(ms[0], ms[1]), ms[2])
        coef = [jnp.exp(m - m_all) for m in ms]
        num = coef[0] * num_scr[0, sl, :]
        den = coef[0] * s_scr[0, sl, :]
        for g in range(1, N_GROUPS):
            num = num + coef[g] * num_scr[g, sl, :]
            den = den + coef[g] * s_scr[g, sl, :]
        o_ref[0, sl, :] = (num / den * _silu(gate_ref[0, sl, :])).astype(BF16)
        return carry

    jax.lax.fori_loop(0, ATTN_TOKENS // BAND, merge, 0)


def _b_attn(h, kv, band_bias):
    b, t, _ = h.shape
    tb = ATTN_TOKENS
    blk = (1, tb, HEAD_DIM)

    def cur(col0):
        return lambda bi, hi, i: (bi, i, col0 + hi)

    def halo_spec(d, col0):
        rows = BAND * d
        per = tb // rows
        return pl.BlockSpec((1, rows, HEAD_DIM), lambda bi, hi, i: (bi, jnp.maximum(i * per - 1, 0), col0 + hi))

    hpg = HEADS_PER_GROUP
    in_specs = ([pl.BlockSpec(blk, cur(hpg * g)) for g in range(N_GROUPS)]
                + [pl.BlockSpec(blk, cur(hpg * g)) for g in range(N_GROUPS)]
                + [pl.BlockSpec(blk, cur(B_HEADS + hpg * g)) for g in range(N_GROUPS)]
                + [halo_spec(d, hpg * g) for g, (_, d) in enumerate(B_GROUPS)]
                + [halo_spec(d, B_HEADS + hpg * g) for g, (_, d) in enumerate(B_GROUPS)]
                + [pl.BlockSpec((N_GROUPS, 2, 1, BAND, 2 * BAND), lambda bi, hi, i: (0, 0, hi, 0, 0)),
                   pl.BlockSpec(blk, cur(B_HEADS))])
    return pl.pallas_call(
        _b_attn_kernel,
        grid=(b, hpg, t // tb),
        in_specs=in_specs,
        out_specs=pl.BlockSpec(blk, lambda bi, hi, i: (bi, i, hi)),
        out_shape=jax.ShapeDtypeStruct((b, t, B_OUT), BF16),
        scratch_shapes=[pltpu.VMEM((N_GROUPS, tb, HEAD_DIM), F32)] * 3,
        compiler_params=_params("parallel", "parallel", "arbitrary"),
        name="b_attn",
    )(h, h, h, kv, kv, kv, kv, kv, kv, kv, kv, kv, kv, kv, kv, band_bias, h)


def _b_out_kernel(o_ref, qm_ref, gm_ref, mem_ref, w_ref, x_ref, g_ref, b_ref, y_ref, yb_ref, wb_ref):
    @pl.when((pl.program_id(0) == 0) & (pl.program_id(1) == 0))
    def _():
        wb_ref[...] = w_ref[...].astype(BF16)

    mbranch = _mem_attn(qm_ref[0].astype(BF16), mem_ref) * _silu(gm_ref[0])
    y = jnp.dot(o_ref[0], wb_ref[0:B_OUT, :], preferred_element_type=F32)
    y = y + jnp.dot(mbranch.astype(BF16), wb_ref[B_OUT:B_OUT + MEM_WIDTH, :], preferred_element_type=F32)
    out = _layernorm(ALPHA * x_ref[0] + y, g_ref[...], b_ref[...])
    y_ref[0] = out
    yb_ref[0] = out.astype(BF16)


def _b_out(o, h, mem, layer, w, wl, x, g, b):
    bsz, t, d = x.shape
    tt = min(t, 512)
    kk = w.shape[1]
    qcol = (MIX_WIDTH + B_OUT) // MEM_WIDTH
    row = pl.BlockSpec((1, tt, d), lambda bi, i: (bi, i, 0))
    return pl.pallas_call(
        _b_out_kernel,
        grid=(bsz, t // tt),
        in_specs=[pl.BlockSpec((1, tt, B_OUT), lambda bi, i: (bi, i, 0)),
                  pl.BlockSpec((1, tt, MEM_WIDTH), lambda bi, i: (bi, i, qcol)),
                  pl.BlockSpec((1, tt, MEM_WIDTH), lambda bi, i: (bi, i, qcol + 1)),
                  pl.BlockSpec((None, 1, N_MEM * KV_SLOTS, HEAD_DIM), lambda bi, i: (layer, bi, 0, 0)),
                  pl.BlockSpec((None, kk, d), lambda bi, i: (wl, 0, 0), pipeline_mode=pl.Buffered(1)),
                  row,
                  pl.BlockSpec((1, d), lambda bi, i: (0, 0)),
                  pl.BlockSpec((1, d), lambda bi, i: (0, 0))],
        out_specs=[row, row],
        out_shape=[jax.ShapeDtypeStruct((bsz, t, d), F32), jax.ShapeDtypeStruct((bsz, t, d), BF16)],
        scratch_shapes=[pltpu.VMEM((kk, d), BF16)],
        compiler_params=_params("arbitrary", "arbitrary"),
        name="b_out",
    )(o, h, h, mem, w, x, g.reshape(1, d), b.reshape(1, d))


def _samp_attn_kernel(q_ref, st_ref, kn_ref, vn_ref, bs_ref, bn_ref, num_ref, m_ref, s_ref, *, nq, w):
    dn = (((1,), (1,)), ((), ()))
    for h in range(HEADS_PER_GROUP):
        lo = h * HEAD_DIM
        q = q_ref[0, :, lo:lo + HEAD_DIM].astype(BF16)
        ks = _kv_rows(st_ref, (0,), h, w).astype(BF16)
        vs = _kv_rows(st_ref, (0,), HEADS_PER_GROUP + h, w).astype(BF16)
        kn = kn_ref[0, :, lo:lo + HEAD_DIM].astype(BF16)
        vn = vn_ref[0, :, lo:lo + HEAD_DIM].astype(BF16)
        ls = jax.lax.dot_general(q, ks, dn, preferred_element_type=F32) * SCALE + bs_ref[h]
        ln = jax.lax.dot_general(q, kn, dn, preferred_element_type=F32) * SCALE + bn_ref[h]
        m = jnp.maximum(jnp.max(ls, -1, keepdims=True), jnp.max(ln, -1, keepdims=True))
        es = jnp.exp(ls - m)
        en = jnp.exp(ln - m)
        ssum = jnp.sum(es, -1, keepdims=True) + jnp.sum(en, -1, keepdims=True)
        num_ref[0, :, lo:lo + HEAD_DIM] = (jnp.dot(es.astype(BF16), vs, preferred_element_type=F32)
                                           + jnp.dot(en.astype(BF16), vn, preferred_element_type=F32))
        m_ref[0, :, lo:lo + HEAD_DIM] = jnp.broadcast_to(m, (nq, HEAD_DIM))
        s_ref[0, :, lo:lo + HEAD_DIM] = jnp.broadcast_to(ssum, (nq, HEAD_DIM))


def _samp_attn(h, qcol, state, kv, kcol, vcol, bias_state, bias_new):
    b, nq, _ = h.shape
    w = state.shape[1] // KV_SLOTS
    blk = (1, nq, GROUP_WIDTH)
    out = jax.ShapeDtypeStruct((b, nq, GROUP_WIDTH), F32)
    return pl.pallas_call(
        functools.partial(_samp_attn_kernel, nq=nq, w=w),
        grid=(b,),
        in_specs=[pl.BlockSpec(blk, lambda i: (i, 0, qcol)),
                  pl.BlockSpec((1, w * KV_SLOTS, HEAD_DIM), lambda i: (i, 0, 0)),
                  pl.BlockSpec(blk, lambda i: (i, 0, kcol)),
                  pl.BlockSpec(blk, lambda i: (i, 0, vcol)),
                  pl.BlockSpec((HEADS_PER_GROUP, nq, w), lambda i: (0, 0, 0)),
                  pl.BlockSpec((HEADS_PER_GROUP, nq, nq), lambda i: (0, 0, 0))],
        out_specs=[pl.BlockSpec(blk, lambda i: (i, 0, 0))] * 3,
        out_shape=[out] * 3,
        compiler_params=_params("parallel"),
        name="samp_attn",
    )(h, state, kv, kv, bias_state, bias_new)


def _samp_merge_kernel(n0, m0, s0, n1, m1, s1, n2, m2, s2, gate, o_out):
    ms = [m0[0], m1[0], m2[0]]
    m_all = jnp.maximum(jnp.maximum(ms[0], ms[1]), ms[2])
    coef = [jnp.exp(m - m_all) for m in ms]
    num = coef[0] * n0[0] + coef[1] * n1[0] + coef[2] * n2[0]
    den = coef[0] * s0[0] + coef[1] * s1[0] + coef[2] * s2[0]
    o_out[0] = (num / den * _silu(gate[0])).astype(BF16)


def _samp_merge(parts, h):
    b, nq, _ = h.shape
    spec = pl.BlockSpec((1, nq, GROUP_WIDTH), lambda i: (i, 0, 0))
    flat = [a for p in parts for a in p]
    return pl.pallas_call(
        _samp_merge_kernel,
        grid=(b,),
        in_specs=[spec] * 9 + [pl.BlockSpec((1, nq, GROUP_WIDTH), lambda i: (i, 0, MIX_WIDTH // GROUP_WIDTH))],
        out_specs=spec,
        out_shape=jax.ShapeDtypeStruct((b, nq, GROUP_WIDTH), BF16),
        compiler_params=_params("parallel"),
        name="samp_merge",
    )(*flat, h)


def _shift_state_kernel(main_ref, next_ref, new_ref, o_ref, *, rows, shift):
    last = pl.program_id(1) == pl.num_programs(1) - 1
    if rows > shift:
        o_ref[0, 0:rows - shift, :] = main_ref[0, shift:rows, :]
    o_ref[0, rows - shift:rows, :] = jnp.where(last, new_ref[0], next_ref[0])


def _shift_state(state, new):
    b, total, lanes = state.shape
    shift = new.shape[1]
    rows = min(total, 4096)
    nblk = total // rows
    per = rows // shift
    return pl.pallas_call(
        functools.partial(_shift_state_kernel, rows=rows, shift=shift),
        grid=(b, nblk),
        in_specs=[pl.BlockSpec((1, rows, lanes), lambda bi, i: (bi, i, 0)),
                  pl.BlockSpec((1, shift, lanes), lambda bi, i: (bi, jnp.minimum((i + 1) * per, nblk * per - 1), 0)),
                  pl.BlockSpec((1, shift, lanes), lambda bi, i: (bi, 0, 0))],
        out_specs=pl.BlockSpec((1, rows, lanes), lambda bi, i: (bi, i, 0)),
        out_shape=jax.ShapeDtypeStruct(state.shape, state.dtype),
        compiler_params=_params("parallel", "arbitrary"),
        name="shift_state",
    )(state, state, new)


def _t5_bucket(dist):
    max_exact = REL_BUCKETS // 2
    safe = jnp.maximum(dist, 1).astype(F32)
    large = max_exact + (jnp.log(safe / max_exact) / math.log(REL_MAX_DIST / max_exact)
                         * (REL_BUCKETS - max_exact)).astype(jnp.int32)
    large = jnp.minimum(large, REL_BUCKETS - 1)
    return jnp.where(dist < max_exact, dist, large)


def _group_bias(rel_bias, g):
    w, d = B_GROUPS[g]
    dist = d * jnp.arange(w // d + 1, dtype=jnp.int32)
    tab = rel_bias[_t5_bucket(dist)]
    return tab[:, g * HEADS_PER_GROUP:(g + 1) * HEADS_PER_GROUP].T.astype(F32)


def _toeplitz(vec, rows, cols):
    hh, p = vec.shape
    flat = jnp.tile(vec, (1, rows))[:, :rows * (p - 1)]
    return flat.reshape(hh, rows, p - 1)[:, :, :cols]


def _band_bias(bias):
    period = 3 * BAND
    vec = jnp.full((bias.shape[0], period), NEG_INF, F32)
    vec = jax.lax.dynamic_update_slice(vec, bias[:, ::-1], (0, 0))
    mat = _toeplitz(vec, BAND, 2 * BAND)
    c = jnp.arange(2 * BAND, dtype=jnp.int32)[None, None, :]
    first = jnp.where(c >= BAND, mat, NEG_INF)
    return jnp.stack([first, mat])


def _sample_bias(bias, w, d, nq):
    hh = bias.shape[0]
    neg = jnp.full_like(bias, NEG_INF)
    by_dist = jnp.stack([bias] + [neg] * (d - 1), -1).reshape(hh, -1)
    by_dist = jnp.concatenate([by_dist[:, :w + 1], jnp.full((hh, nq), NEG_INF, F32)], 1)
    rev = by_dist[:, ::-1]
    bs = jnp.stack([rev[:, nq - n:nq - n + w] for n in range(nq)], 1)
    small = jnp.concatenate([jnp.full((hh, nq), NEG_INF, F32), by_dist[:, :nq]], 1)[:, ::-1]
    bn = jnp.stack([small[:, nq - 1 - n:2 * nq - 1 - n] for n in range(nq)], 1)
    return bs, bn


def _trunk(x, conv_prev, mem_kv, kv_state, a_w_in, a_w_dw, a_b_dw, a_cn_g, a_cn_b, a_w_out,
           b_w_in, b_w_out, w_kv_shared, rel_bias, ln_g, ln_b):
    b, t, d_model = x.shape
    m = b * t
    x2 = x.reshape(m, d_model)
    xb = x2
    new_conv = []
    for l in range(N_A_LAYERS):
        u, gates = _a_proj(xb, a_w_in, l)
        cm, nc = _a_mix(u.reshape(b, t, -1), gates.reshape(b, t, -1), conv_prev[l], mem_kv, l,
                        a_w_dw[l], a_b_dw[l], a_cn_g[l], a_cn_b[l])
        new_conv.append(nc)
        x2, xb = _out_ln(cm.reshape(m, d_model), a_w_out, l, x2, ln_g[l], ln_b[l])

    kv = _mm(xb, w_kv_shared[None], 0).reshape(b, t, 2 * MIX_WIDTH)
    vcol0 = MIX_WIDTH // GROUP_WIDTH
    biases = [_group_bias(rel_bias, g) for g in range(N_GROUPS)]
    kv_new = [jnp.concatenate([kv[:, t - min(w, t):, g * GROUP_WIDTH:(g + 1) * GROUP_WIDTH],
                               kv[:, t - min(w, t):, MIX_WIDTH + g * GROUP_WIDTH:MIX_WIDTH + (g + 1) * GROUP_WIDTH]], -1)
              for g, (w, _) in enumerate(B_GROUPS)]
    if kv_state is None:
        new_bufs = kv_new
        band_bias = jnp.stack([_band_bias(bi) for bi in biases])
    else:
        new_bufs = [_shift_state(st, kn.reshape(b, t * KV_SLOTS, HEAD_DIM)) for st, kn in zip(kv_state, kv_new)]
        samp_bias = [_sample_bias(bi, w, d, t) for bi, (w, d) in zip(biases, B_GROUPS)]
    new_bufs = [nb.reshape(b, -1, 2, HEADS_PER_GROUP, HEAD_DIM) for nb in new_bufs]

    x3 = x2.reshape(b, t, d_model)
    for i in range(DEPTH - N_A_LAYERS):
        l = N_A_LAYERS + i
        h = _mm(xb.reshape(m, d_model), b_w_in, i).reshape(b, t, -1)
        if kv_state is None:
            o = _b_attn(h, kv, band_bias)
        else:
            parts = [_samp_attn(h, g, kv_state[g], kv, g, vcol0 + g, *samp_bias[g]) for g in range(N_GROUPS)]
            o = _samp_merge(parts, h)
        x3, xb = _b_out(o, h, mem_kv, l, b_w_out, i, x3, ln_g[l], ln_b[l])
    return x3, jnp.stack(new_conv), new_bufs


def kernel(x_prompt, x_sample, state_conv, state_kv_g0, state_kv_g1, state_kv_g2, cache_mem_kv, mem_prompt,
           a_w_in, a_w_dw, a_b_dw, a_cn_g, a_cn_b, a_w_out, b_w_in, b_w_out, w_kv_shared, w_mem_kv,
           rel_bias, ln_g, ln_b):
    bp = x_prompt.shape[0]
    bs = x_sample.shape[0]
    new_mem_kv = _mm(mem_prompt.reshape(bp * N_MEM, D_MODEL), w_mem_kv).reshape(
        DEPTH, bp, N_MEM, 2, MEM_HEADS, HEAD_DIM)
    mem_p = new_mem_kv.reshape(DEPTH, bp, N_MEM * KV_SLOTS, HEAD_DIM)
    conv_zero = jnp.zeros((N_A_LAYERS, bp, CONV_WIDTH - 1, CONV_CH), x_prompt.dtype)
    weights = (a_w_in, a_w_dw, a_b_dw, a_cn_g, a_cn_b, a_w_out, b_w_in, b_w_out, w_kv_shared, rel_bias, ln_g, ln_b)

    y_p, conv_p, bufs_p = _trunk(x_prompt, conv_zero, mem_p, None, *weights)
    kv_state = [s.reshape(bs, s.shape[1] * KV_SLOTS, HEAD_DIM) for s in (state_kv_g0, state_kv_g1, state_kv_g2)]
    mem_s = cache_mem_kv.reshape(DEPTH, bs, N_MEM * KV_SLOTS, HEAD_DIM)
    y_s, conv_s, bufs_s = _trunk(x_sample, state_conv, mem_s, kv_state, *weights)

    return (y_p, y_s, conv_p, conv_s, bufs_p[0], bufs_s[0], bufs_p[1], bufs_s[1], bufs_p[2], bufs_s[2], new_mem_kv)
```

```python
import functools
import math

import jax
import jax.numpy as jnp
from jax.experimental import pallas as pl
from jax.experimental.pallas import tpu as pltpu

D_MODEL = 2048
DEPTH = 4
HEAD_DIM = 128
MEM_HEADS = 4
MEM_WIDTH = MEM_HEADS * HEAD_DIM
N_MEM = 256
MIX_WIDTH = D_MODEL - MEM_WIDTH
N_A_LAYERS = DEPTH // 2
CONV_CH = MIX_WIDTH
CONV_WIDTH = 31
B_GROUPS = ((128, 1), (512, 4), (2048, 16))
N_GROUPS = len(B_GROUPS)
HEADS_PER_GROUP = 4
B_HEADS = N_GROUPS * HEADS_PER_GROUP
GROUP_WIDTH = HEADS_PER_GROUP * HEAD_DIM
KV_SLOTS = 2 * HEADS_PER_GROUP
B_OUT = GROUP_WIDTH
REL_BUCKETS = 32
REL_MAX_DIST = 2048
BAND = 128
ATTN_TOKENS = BAND * max(d for _, d in B_GROUPS)
LN_EPS = 1e-5
ALPHA = (2 * DEPTH) ** 0.25
NEG_INF = -1e30
SCALE = HEAD_DIM ** -0.5

HALO = 32
VMEM_LIMIT = 56 * 1024 * 1024
BF16 = jnp.bfloat16
F32 = jnp.float32


def _params(*sem):
    return pltpu.CompilerParams(dimension_semantics=sem, vmem_limit_bytes=VMEM_LIMIT)


def _sigmoid(x):
    return 1.0 / (1.0 + jnp.exp(-x))


def _silu(x):
    return x * _sigmoid(x)


def _layernorm(z, g, b):
    mu = jnp.mean(z, -1, keepdims=True)
    zc = z - mu
    var = jnp.mean(zc * zc, -1, keepdims=True)
    return zc * jax.lax.rsqrt(var + LN_EPS) * g + b


def _lhs_bf16(x_ref, scratch, col_axis):
    if not scratch:
        return x_ref[...]
    xb_ref, = scratch

    @pl.when(pl.program_id(col_axis) == 0)
    def _():
        xb_ref[...] = x_ref[...].astype(BF16)

    return xb_ref[...]


def _row_tile(x):
    m, k = x.shape
    if x.dtype == BF16:
        return min(m, 2048), []
    return min(m, 1024), [pltpu.VMEM((min(m, 1024), k), BF16)]


def _mm_kernel(x_ref, w_ref, o_ref, *scratch):
    xb = _lhs_bf16(x_ref, scratch, 2)
    o_ref[...] = jnp.dot(xb, w_ref[...].astype(BF16), preferred_element_type=F32)


def _mm(x, w, layer=None, *, tn=512):
    m, k = x.shape
    n = w.shape[2]
    tm, scratch = _row_tile(x)
    l0, nl = (0, w.shape[0]) if layer is None else (layer, 1)
    out = pl.pallas_call(
        _mm_kernel,
        grid=(nl, m // tm, n // tn),
        in_specs=[pl.BlockSpec((tm, k), lambda l, i, j: (i, 0)),
                  pl.BlockSpec((None, k, tn), lambda l, i, j: (l0 + l, 0, j))],
        out_specs=pl.BlockSpec((None, tm, tn), lambda l, i, j: (l, i, j)),
        out_shape=jax.ShapeDtypeStruct((nl, m, n), F32),
        scratch_shapes=scratch,
        compiler_params=_params("arbitrary", "arbitrary", "arbitrary"),
        name="mm",
    )(x, w)
    return out if layer is None else out.reshape(m, n)


A_TN = 256
N_GLU = CONV_CH // A_TN
N_QM = MEM_WIDTH // A_TN


def _a_proj_kernel(x_ref, wa_ref, wb_ref, u_ref, g_ref, *scratch):
    j = pl.program_id(1)
    xb = _lhs_bf16(x_ref, scratch, 1)

    def proj(w_ref):
        return jnp.dot(xb, w_ref[...].astype(BF16), preferred_element_type=F32)

    @pl.when(j < N_GLU)
    def _():
        u_ref[...] = proj(wa_ref) * _sigmoid(proj(wb_ref))

    is_qm = (j >= 2 * N_GLU) & (j < 2 * N_GLU + N_QM)

    @pl.when(is_qm)
    def _():
        g_ref[...] = proj(wa_ref).astype(BF16)

    @pl.when((j >= N_GLU) & jnp.logical_not(is_qm))
    def _():
        g_ref[...] = _silu(proj(wa_ref)).astype(BF16)


def _a_proj(x, w, layer):
    m, k = x.shape
    tm, scratch = _row_tile(x)
    n_steps = 2 * N_GLU + 2 * N_QM
    return pl.pallas_call(
        _a_proj_kernel,
        grid=(m // tm, n_steps),
        in_specs=[pl.BlockSpec((tm, k), lambda i, j: (i, 0)),
                  pl.BlockSpec((None, k, A_TN), lambda i, j: (layer, 0, jnp.where(j < N_GLU, j, j + N_GLU))),
                  pl.BlockSpec((None, k, A_TN), lambda i, j: (layer, 0, jnp.minimum(j + N_GLU, 2 * N_GLU - 1)))],
        out_specs=[pl.BlockSpec((tm, A_TN), lambda i, j: (i, jnp.minimum(j, N_GLU - 1))),
                   pl.BlockSpec((tm, A_TN), lambda i, j: (i, jnp.maximum(j - N_GLU, 0)))],
        out_shape=[jax.ShapeDtypeStruct((m, CONV_CH), F32),
                   jax.ShapeDtypeStruct((m, CONV_CH + 2 * MEM_WIDTH), BF16)],
        scratch_shapes=scratch,
        compiler_params=_params("arbitrary", "arbitrary"),
        name="a_proj",
    )(x, w, w)


def _out_ln_kernel(cm_ref, w_ref, x_ref, g_ref, b_ref, o_ref, ob_ref, wb_ref):
    @pl.when(pl.program_id(0) == 0)
    def _():
        wb_ref[...] = w_ref[...].astype(BF16)

    y = jnp.dot(cm_ref[...], wb_ref[...], preferred_element_type=F32)
    out = _layernorm(ALPHA * x_ref[...] + y, g_ref[...], b_ref[...])
    o_ref[...] = out
    ob_ref[...] = out.astype(BF16)


def _out_ln(cm, w, layer, x, g, b):
    m, kk = cm.shape
    d = w.shape[2]
    tm = min(m, 512)
    row = pl.BlockSpec((tm, d), lambda i: (i, 0))
    return pl.pallas_call(
        _out_ln_kernel,
        grid=(m // tm,),
        in_specs=[pl.BlockSpec((tm, kk), lambda i: (i, 0)),
                  pl.BlockSpec((None, kk, d), lambda i: (layer, 0, 0), pipeline_mode=pl.Buffered(1)),
                  row,
                  pl.BlockSpec((1, d), lambda i: (0, 0)),
                  pl.BlockSpec((1, d), lambda i: (0, 0))],
        out_specs=[row, row],
        out_shape=[jax.ShapeDtypeStruct((m, d), F32), jax.ShapeDtypeStruct((m, d), BF16)],
        scratch_shapes=[pltpu.VMEM((kk, d), BF16)],
        compiler_params=_params("arbitrary"),
        name="out_ln",
    )(cm, w, x, g.reshape(1, d), b.reshape(1, d))


def _kv_rows(ref, lead, slot, n):
    return ref[lead + (pl.ds(slot, n, stride=KV_SLOTS), slice(None))]


def _mem_attn(qm, mem_ref):
    outs = []
    for h in range(MEM_HEADS):
        lo = h * HEAD_DIM
        q = qm[:, lo:lo + HEAD_DIM]
        k = _kv_rows(mem_ref, (0,), h, N_MEM).astype(BF16)
        v = _kv_rows(mem_ref, (0,), MEM_HEADS + h, N_MEM).astype(BF16)
        s = jax.lax.dot_general(q, k, (((1,), (1,)), ((), ())), preferred_element_type=F32) * SCALE
        s = s - jnp.max(s, -1, keepdims=True)
        e = jnp.exp(s)
        p = e / jnp.sum(e, -1, keepdims=True)
        outs.append(jnp.dot(p.astype(BF16), v, preferred_element_type=F32))
    return jnp.concatenate(outs, -1)


def _a_mix_kernel(*refs, tt, tr, has_halo):
    if has_halo:
        (u_ref, sgate, qm, sgm, u_prev, cprev, mem, wdw, bdw, cng, cnb,
         cm_out, nc_out, ext, wrep, conv) = refs
    else:
        (u_ref, sgate, qm, sgm, cprev, mem, wdw, bdw, cng, cnb,
         cm_out, nc_out, ext, wrep, conv) = refs
    i = pl.program_id(1)
    n_blk = CONV_CH // HEAD_DIM

    @pl.when((pl.program_id(0) == 0) & (i == 0))
    def _():
        for k in range(CONV_WIDTH):
            wrep[k] = jnp.broadcast_to(wdw[k:k + 1, :], (8, CONV_CH))

    for c in range(n_blk):
        lanes = slice(c * HEAD_DIM, (c + 1) * HEAD_DIM)
        if has_halo:
            ext[c, 0:HALO, :] = jnp.where(i == 0, cprev[0, :, lanes], u_prev[0, :, lanes])
        else:
            ext[c, 0:HALO, :] = cprev[0, :, lanes]
        ext[c, HALO:HALO + tt, :] = u_ref[0, :, lanes]
        nc_out[0, :, lanes] = ext[c, HALO + tt - (CONV_WIDTH - 1):HALO + tt, :]

    off0 = HALO - (CONV_WIDTH - 1)
    for c in range(n_blk):
        lanes = slice(c * HEAD_DIM, (c + 1) * HEAD_DIM)

        def body(rc, carry, c=c, lanes=lanes):
            r0 = pl.multiple_of(rc * tr, 8)
            acc = jnp.zeros((tr // 8, 8, HEAD_DIM), F32)
            for k in range(CONV_WIDTH):
                win = ext[c, pl.ds(r0 + k + off0, tr), :].reshape(tr // 8, 8, HEAD_DIM)
                acc = acc + win * wrep[k, :, lanes][None]
            conv[pl.ds(r0, tr), lanes] = acc.reshape(tr, HEAD_DIM) + bdw[:, lanes]
            return carry

        jax.lax.fori_loop(0, tt // tr, body, 0)

    zn = _layernorm(conv[...], cng[...], cnb[...])
    cm_out[0, :, 0:CONV_CH] = (_silu(zn) * sgate[0].astype(F32)).astype(BF16)
    cm_out[0, :, CONV_CH:D_MODEL] = (_mem_attn(qm[0], mem) * sgm[0].astype(F32)).astype(BF16)


def _a_mix(u, gates, conv_prev, mem, layer, w_dw, b_dw, cn_g, cn_b):
    b, t, _ = u.shape
    tt = min(t, 256)
    tr = min(tt, 64)
    nt = t // tt
    has_halo = nt > 1
    cprev = jnp.pad(conv_prev, ((0, 0), (HALO - (CONV_WIDTH - 1), 0), (0, 0)))
    c3 = CONV_CH
    qcol = CONV_CH // MEM_WIDTH

    in_specs = [pl.BlockSpec((1, tt, c3), lambda bi, i: (bi, i, 0)),
                pl.BlockSpec((1, tt, c3), lambda bi, i: (bi, i, 0)),
                pl.BlockSpec((1, tt, MEM_WIDTH), lambda bi, i: (bi, i, qcol)),
                pl.BlockSpec((1, tt, MEM_WIDTH), lambda bi, i: (bi, i, qcol + 1))]
    args = [u, gates, gates, gates]
    if has_halo:
        per = tt // HALO
        in_specs += [pl.BlockSpec((1, HALO, c3), lambda bi, i: (bi, jnp.maximum(i * per - 1, 0), 0))]
        args += [u]
    in_specs += [pl.BlockSpec((1, HALO, c3), lambda bi, i: (bi, 0, 0)),
                 pl.BlockSpec((None, 1, N_MEM * KV_SLOTS, HEAD_DIM), lambda bi, i: (layer, bi, 0, 0)),
                 pl.BlockSpec((CONV_WIDTH, c3), lambda bi, i: (0, 0)),
                 pl.BlockSpec((1, c3), lambda bi, i: (0, 0)),
                 pl.BlockSpec((1, c3), lambda bi, i: (0, 0)),
                 pl.BlockSpec((1, c3), lambda bi, i: (0, 0))]
    args += [cprev, mem, w_dw, b_dw.reshape(1, c3), cn_g.reshape(1, c3), cn_b.reshape(1, c3)]

    return pl.pallas_call(
        functools.partial(_a_mix_kernel, tt=tt, tr=tr, has_halo=has_halo),
        grid=(b, nt),
        in_specs=in_specs,
        out_specs=[pl.BlockSpec((1, tt, D_MODEL), lambda bi, i: (bi, i, 0)),
                   pl.BlockSpec((1, CONV_WIDTH - 1, c3), lambda bi, i: (bi, 0, 0))],
        out_shape=[jax.ShapeDtypeStruct((b, t, D_MODEL), BF16),
                   jax.ShapeDtypeStruct((b, CONV_WIDTH - 1, c3), F32)],
        scratch_shapes=[pltpu.VMEM((c3 // HEAD_DIM, tt + HALO, HEAD_DIM), F32),
                        pltpu.VMEM((CONV_WIDTH, 8, c3), F32),
                        pltpu.VMEM((tt, c3), F32)],
        compiler_params=_params("arbitrary", "arbitrary"),
        name="a_mix",
    )(*args)


def _rows(ref, start, size, stride):
    if stride == 1:
        return ref[0, start:start + size, :]
    return ref[0, pl.ds(start, size, stride=stride), :]


def _b_attn_kernel(q0, q1, q2, k0, k1, k2, v0, v1, v2, hk0, hk1, hk2, hv0, hv1, hv2, bias_ref, gate_ref,
                   o_ref, num_scr, m_scr, s_scr):
    first = jnp.minimum(pl.program_id(2), 1)
    qs, ks, vs = (q0, q1, q2), (k0, k1, k2), (v0, v1, v2)
    hks, hvs = (hk0, hk1, hk2), (hv0, hv1, hv2)
    dn = (((1,), (1,)), ((), ()))
    for g, (_, d) in enumerate(B_GROUPS):
        for r in range(d):
            for s in range(ATTN_TOKENS // (BAND * d)):
                start = r + d * BAND * s
                q = _rows(qs[g], start, BAND, d)
                if s == 0:
                    k = jnp.concatenate([_rows(hks[g], r, BAND, d), _rows(ks[g], r, BAND, d)], 0)
                    v = jnp.concatenate([_rows(hvs[g], r, BAND, d), _rows(vs[g], r, BAND, d)], 0)
                    bias = bias_ref[g, first, 0]
                else:
                    k = _rows(ks[g], start - d * BAND, 2 * BAND, d)
                    v = _rows(vs[g], start - d * BAND, 2 * BAND, d)
                    bias = bias_ref[g, 1, 0]
                logits = jax.lax.dot_general(q.astype(BF16), k.astype(BF16), dn, preferred_element_type=F32)
                logits = logits * SCALE + bias
                m = jnp.max(logits, -1, keepdims=True)
                e = jnp.exp(logits - m)
                ssum = jnp.sum(e, -1, keepdims=True)
                num = jnp.dot(e.astype(BF16), v.astype(BF16), preferred_element_type=F32)
                if d == 1:
                    dst = pl.ds(start, BAND)
                else:
                    dst = pl.ds(start, BAND, stride=d)
                num_scr[g, dst, :] = num
                m_scr[g, dst, :] = jnp.broadcast_to(m, (BAND, HEAD_DIM))
                s_scr[g, dst, :] = jnp.broadcast_to(ssum, (BAND, HEAD_DIM))

    def merge(c, carry):
        sl = pl.ds(pl.multiple_of(c * BAND, BAND), BAND)
        ms = [m_scr[g, sl, :] for g in range(N_GROUPS)]
        m_all = jnp.maximum(jnp.maximum(ms[0], ms[1]), ms[2])
        coef = [jnp.exp(m - m_all) for m in ms]
        num = coef[0] * num_scr[0, sl, :]
        den = coef[0] * s_scr[0, sl, :]
        for g in range(1, N_GROUPS):
            num = num + coef[g] * num_scr[g, sl, :]
            den = den + coef[g] * s_scr[g, sl, :]
        o_ref[0, sl, :] = (num / den * _silu(gate_ref[0, sl, :])).astype(BF16)
        return carry

    jax.lax.fori_loop(0, ATTN_TOKENS // BAND, merge, 0)


def _b_attn(h, kv, band_bias):
    b, t, _ = h.shape
    tb = ATTN_TOKENS
    blk = (1, tb, HEAD_DIM)

    def cur(col0):
        return lambda bi, hi, i: (bi, i, col0 + hi)

    def halo_spec(d, col0):
        rows = BAND * d
        per = tb // rows
        return pl.BlockSpec((1, rows, HEAD_DIM), lambda bi, hi, i: (bi, jnp.maximum(i * per - 1, 0), col0 + hi))

    hpg = HEADS_PER_GROUP
    in_specs = ([pl.BlockSpec(blk, cur(hpg * g)) for g in range(N_GROUPS)]
                + [pl.BlockSpec(blk, cur(hpg * g)) for g in range(N_GROUPS)]
                + [pl.BlockSpec(blk, cur(B_HEADS + hpg * g)) for g in range(N_GROUPS)]
                + [halo_spec(d, hpg * g) for g, (_, d) in enumerate(B_GROUPS)]
                + [halo_spec(d, B_HEADS + hpg * g) for g, (_, d) in enumerate(B_GROUPS)]
                + [pl.BlockSpec((N_GROUPS, 2, 1, BAND, 2 * BAND), lambda bi, hi, i: (0, 0, hi, 0, 0)),
                   pl.BlockSpec(blk, cur(B_HEADS))])
    return pl.pallas_call(
        _b_attn_kernel,
        grid=(b, hpg, t // tb),
        in_specs=in_specs,
        out_specs=pl.BlockSpec(blk, lambda bi, hi, i: (bi, i, hi)),
        out_shape=jax.ShapeDtypeStruct((b, t, B_OUT), BF16),
        scratch_shapes=[pltpu.VMEM((N_GROUPS, tb, HEAD_DIM), F32)] * 3,
        compiler_params=_params("parallel", "parallel", "arbitrary"),
        name="b_attn",
    )(h, h, h, kv, kv, kv, kv, kv, kv, kv, kv, kv, kv, kv, kv, band_bias, h)


def _b_out_kernel(o_ref, qm_ref, gm_ref, mem_ref, w_ref, x_ref, g_ref, b_ref, y_ref, yb_ref, wb_ref):
    @pl.when((pl.program_id(0) == 0) & (pl.program_id(1) == 0))
    def _():
        wb_ref[...] = w_ref[...].astype(BF16)

    mbranch = _mem_attn(qm_ref[0].astype(BF16), mem_ref) * _silu(gm_ref[0])
    y = jnp.dot(o_ref[0], wb_ref[0:B_OUT, :], preferred_element_type=F32)
    y = y + jnp.dot(mbranch.astype(BF16), wb_ref[B_OUT:B_OUT + MEM_WIDTH, :], preferred_element_type=F32)
    out = _layernorm(ALPHA * x_ref[0] + y, g_ref[...], b_ref[...])
    y_ref[0] = out
    yb_ref[0] = out.astype(BF16)


def _b_out(o, h, mem, layer, w, wl, x, g, b):
    bsz, t, d = x.shape
    tt = min(t, 512)
    kk = w.shape[1]
    qcol = (MIX_WIDTH + B_OUT) // MEM_WIDTH
    row = pl.BlockSpec((1, tt, d), lambda bi, i: (bi, i, 0))
    return pl.pallas_call(
        _b_out_kernel,
        grid=(bsz, t // tt),
        in_specs=[pl.BlockSpec((1, tt, B_OUT), lambda bi, i: (bi, i, 0)),
                  pl.BlockSpec((1, tt, MEM_WIDTH), lambda bi, i: (bi, i, qcol)),
                  pl.BlockSpec((1, tt, MEM_WIDTH), lambda bi, i: (bi, i, qcol + 1)),
                  pl.BlockSpec((None, 1, N_MEM * KV_SLOTS, HEAD_DIM), lambda bi, i: (layer, bi, 0, 0)),
                  pl.BlockSpec((None, kk, d), lambda bi, i: (wl, 0, 0), pipeline_mode=pl.Buffered(1)),
                  row,
                  pl.BlockSpec((1, d), lambda bi, i: (0, 0)),
                  pl.BlockSpec((1, d), lambda bi, i: (0, 0))],
        out_specs=[row, row],
        out_shape=[jax.ShapeDtypeStruct((bsz, t, d), F32), jax.ShapeDtypeStruct((bsz, t, d), BF16)],
        scratch_shapes=[pltpu.VMEM((kk, d), BF16)],
        compiler_params=_params("arbitrary", "arbitrary"),
        name="b_out",
    )(o, h, h, mem, w, x, g.reshape(1, d), b.reshape(1, d))


def _samp_attn_kernel(q_ref, st_ref, kn_ref, vn_ref, bs_ref, bn_ref, num_ref, m_ref, s_ref, *, nq, w):
    dn = (((1,), (1,)), ((), ()))
    for h in range(HEADS_PER_GROUP):
        lo = h * HEAD_DIM
        q = q_ref[0, :, lo:lo + HEAD_DIM].astype(BF16)
        ks = _kv_rows(st_ref, (0,), h, w).astype(BF16)
        vs = _kv_rows(st_ref, (0,), HEADS_PER_GROUP + h, w).astype(BF16)
        kn = kn_ref[0, :, lo:lo + HEAD_DIM].astype(BF16)
        vn = vn_ref[0, :, lo:lo + HEAD_DIM].astype(BF16)
        ls = jax.lax.dot_general(q, ks, dn, preferred_element_type=F32) * SCALE + bs_ref[h]
        ln = jax.lax.dot_general(q, kn, dn, preferred_element_type=F32) * SCALE + bn_ref[h]
        m = jnp.maximum(jnp.max(ls, -1, keepdims=True), jnp.max(ln, -1, keepdims=True))
        es = jnp.exp(ls - m)
        en = jnp.exp(ln - m)
        ssum = jnp.sum(es, -1, keepdims=True) + jnp.sum(en, -1, keepdims=True)
        num_ref[0, :, lo:lo + HEAD_DIM] = (jnp.dot(es.astype(BF16), vs, preferred_element_type=F32)
                                           + jnp.dot(en.astype(BF16), vn, preferred_element_type=F32))
        m_ref[0, :, lo:lo + HEAD_DIM] = jnp.broadcast_to(m, (nq, HEAD_DIM))
        s_ref[0, :, lo:lo + HEAD_DIM] = jnp.broadcast_to(ssum, (nq, HEAD_DIM))


def _samp_attn(h, qcol, state, kv, kcol, vcol, bias_state, bias_new):
    b, nq, _ = h.shape
    w = state.shape[1] // KV_SLOTS
    blk = (1, nq, GROUP_WIDTH)
    out = jax.ShapeDtypeStruct((b, nq, GROUP_WIDTH), F32)
    return pl.pallas_call(
        functools.partial(_samp_attn_kernel, nq=nq, w=w),
        grid=(b,),
        in_specs=[pl.BlockSpec(blk, lambda i: (i, 0, qcol)),
                  pl.BlockSpec((1, w * KV_SLOTS, HEAD_DIM), lambda i: (i, 0, 0)),
                  pl.BlockSpec(blk, lambda i: (i, 0, kcol)),
                  pl.BlockSpec(blk, lambda i: (i, 0, vcol)),
                  pl.BlockSpec((HEADS_PER_GROUP, nq, w), lambda i: (0, 0, 0)),
                  pl.BlockSpec((HEADS_PER_GROUP, nq, nq), lambda i: (0, 0, 0))],
        out_specs=[pl.BlockSpec(blk, lambda i: (i, 0, 0))] * 3,
        out_shape=[out] * 3,
        compiler_params=_params("parallel"),
        name="samp_attn",
    )(h, state, kv, kv, bias_state, bias_new)


def _samp_merge_kernel(n0, m0, s0, n1, m1, s1, n2, m2, s2, gate, o_out):
    ms = [m0[0], m1[0], m2[0]]
    m_all = jnp.maximum(jnp.maximum(ms[0], ms[1]), ms[2])
    coef = [jnp.exp(m - m_all) for m in ms]
    num = coef[0] * n0[0] + coef[1] * n1[0] + coef[2] * n2[0]
    den = coef[0] * s0[0] + coef[1] * s1[0] + coef[2] * s2[0]
    o_out[0] = (num / den * _silu(gate[0])).astype(BF16)


def _samp_merge(parts, h):
    b, nq, _ = h.shape
    spec = pl.BlockSpec((1, nq, GROUP_WIDTH), lambda i: (i, 0, 0))
    flat = [a for p in parts for a in p]
    return pl.pallas_call(
        _samp_merge_kernel,
        grid=(b,),
        in_specs=[spec] * 9 + [pl.BlockSpec((1, nq, GROUP_WIDTH), lambda i: (i, 0, MIX_WIDTH // GROUP_WIDTH))],
        out_specs=spec,
        out_shape=jax.ShapeDtypeStruct((b, nq, GROUP_WIDTH), BF16),
        compiler_params=_params("parallel"),
        name="samp_merge",
    )(*flat, h)


def _shift_state_kernel(main_ref, next_ref, new_ref, o_ref, *, rows, shift):
    last = pl.program_id(1) == pl.num_programs(1) - 1
    if rows > shift:
        o_ref[0, 0:rows - shift, :] = main_ref[0, shift:rows, :]
    o_ref[0, rows - shift:rows, :] = jnp.where(last, new_ref[0], next_ref[0])


def _shift_state(state, new):
    b, total, lanes = state.shape
    shift = new.shape[1]
    rows = min(total, 4096)
    nblk = total // rows
    per = rows // shift
    return pl.pallas_call(
        functools.partial(_shift_state_kernel, rows=rows, shift=shift),
        grid=(b, nblk),
        in_specs=[pl.BlockSpec((1, rows, lanes), lambda bi, i: (bi, i, 0)),
                  pl.BlockSpec((1, shift, lanes), lambda bi, i: (bi, jnp.minimum((i + 1) * per, nblk * per - 1), 0)),
                  pl.BlockSpec((1, shift, lanes), lambda bi, i: (bi, 0, 0))],
        out_specs=pl.BlockSpec((1, rows, lanes), lambda bi, i: (bi, i, 0)),
        out_shape=jax.ShapeDtypeStruct(state.shape, state.dtype),
        compiler_params=_params("parallel", "arbitrary"),
        name="shift_state",
    )(state, state, new)


def _t5_bucket(dist):
    max_exact = REL_BUCKETS // 2
    safe = jnp.maximum(dist, 1).astype(F32)
    large = max_exact + (jnp.log(safe / max_exact) / math.log(REL_MAX_DIST / max_exact)
                         * (REL_BUCKETS - max_exact)).astype(jnp.int32)
    large = jnp.minimum(large, REL_BUCKETS - 1)
    return jnp.where(dist < max_exact, dist, large)


def _group_bias(rel_bias, g):
    w, d = B_GROUPS[g]
    dist = d * jnp.arange(w // d + 1, dtype=jnp.int32)
    tab = rel_bias[_t5_bucket(dist)]
    return tab[:, g * HEADS_PER_GROUP:(g + 1) * HEADS_PER_GROUP].T.astype(F32)


def _toeplitz(vec, rows, cols):
    hh, p = vec.shape
    flat = jnp.tile(vec, (1, rows))[:, :rows * (p - 1)]
    return flat.reshape(hh, rows, p - 1)[:, :, :cols]


def _band_bias(bias):
    period = 3 * BAND
    vec = jnp.full((bias.shape[0], period), NEG_INF, F32)
    vec = jax.lax.dynamic_update_slice(vec, bias[:, ::-1], (0, 0))
    mat = _toeplitz(vec, BAND, 2 * BAND)
    c = jnp.arange(2 * BAND, dtype=jnp.int32)[None, None, :]
    first = jnp.where(c >= BAND, mat, NEG_INF)
    return jnp.stack([first, mat])


def _sample_bias(bias, w, d, nq):
    hh = bias.shape[0]
    neg = jnp.full_like(bias, NEG_INF)
    by_dist = jnp.stack([bias] + [neg] * (d - 1), -1).reshape(hh, -1)
    by_dist = jnp.concatenate([by_dist[:, :w + 1], jnp.full((hh, nq), NEG_INF, F32)], 1)
    rev = by_dist[:, ::-1]
    bs = jnp.stack([rev[:, nq - n:nq - n + w] for n in range(nq)], 1)
    small = jnp.concatenate([jnp.full((hh, nq), NEG_INF, F32), by_dist[:, :nq]], 1)[:, ::-1]
    bn = jnp.stack([small[:, nq - 1 - n:2 * nq - 1 - n] for n in range(nq)], 1)
    return bs, bn


def _trunk(x, conv_prev, mem_kv, kv_state, a_w_in, a_w_dw, a_b_dw, a_cn_g, a_cn_b, a_w_out,
           b_w_in, b_w_out, w_kv_shared, rel_bias, ln_g, ln_b):
    b, t, d_model = x.shape
    m = b * t
    x2 = x.reshape(m, d_model)
    xb = x2
    new_conv = []
    for l in range(N_A_LAYERS):
        u, gates = _a_proj(xb, a_w_in, l)
        cm, nc = _a_mix(u.reshape(b, t, -1), gates.reshape(b, t, -1), conv_prev[l], mem_kv, l,
                        a_w_dw[l], a_b_dw[l], a_cn_g[l], a_cn_b[l])
        new_conv.append(nc)
        x2, xb = _out_ln(cm.reshape(m, d_model), a_w_out, l, x2, ln_g[l], ln_b[l])

    kv = _mm(xb, w_kv_shared[None], 0).reshape(b, t, 2 * MIX_WIDTH)
    vcol0 = MIX_WIDTH // GROUP_WIDTH
    biases = [_group_bias(rel_bias, g) for g in range(N_GROUPS)]
    kv_new = [jnp.concatenate([kv[:, t - min(w, t):, g * GROUP_WIDTH:(g + 1) * GROUP_WIDTH],
                               kv[:, t - min(w, t):, MIX_WIDTH + g * GROUP_WIDTH:MIX_WIDTH + (g + 1) * GROUP_WIDTH]], -1)
              for g, (w, _) in enumerate(B_GROUPS)]
    if kv_state is None:
        new_bufs = kv_new
        band_bias = jnp.stack([_band_bias(bi) for bi in biases])
    else:
        new_bufs = [_shift_state(st, kn.reshape(b, t * KV_SLOTS, HEAD_DIM)) for st, kn in zip(kv_state, kv_new)]
        samp_bias = [_sample_bias(bi, w, d, t) for bi, (w, d) in zip(biases, B_GROUPS)]
    new_bufs = [nb.reshape(b, -1, 2, HEADS_PER_GROUP, HEAD_DIM) for nb in new_bufs]

    x3 = x2.reshape(b, t, d_model)
    for i in range(DEPTH - N_A_LAYERS):
        l = N_A_LAYERS + i
        h = _mm(xb.reshape(m, d_model), b_w_in, i).reshape(b, t, -1)
        if kv_state is None:
            o = _b_attn(h, kv, band_bias)
        else:
            parts = [_samp_attn(h, g, kv_state[g], kv, g, vcol0 + g, *samp_bias[g]) for g in range(N_GROUPS)]
            o = _samp_merge(parts, h)
        x3, xb = _b_out(o, h, mem_kv, l, b_w_out, i, x3, ln_g[l], ln_b[l])
    return x3, jnp.stack(new_conv), new_bufs


def kernel(x_prompt, x_sample, state_conv, state_kv_g0, state_kv_g1, state_kv_g2, cache_mem_kv, mem_prompt,
           a_w_in, a_w_dw, a_b_dw, a_cn_g, a_cn_b, a_w_out, b_w_in, b_w_out, w_kv_shared, w_mem_kv,
           rel_bias, ln_g, ln_b):
    bp = x_prompt.shape[0]
    bs = x_sample.shape[0]
    new_mem_kv = _mm(mem_prompt.reshape(bp * N_MEM, D_MODEL), w_mem_kv).reshape(
        DEPTH, bp, N_MEM, 2, MEM_HEADS, HEAD_DIM)
    mem_p = new_mem_kv.reshape(DEPTH, bp, N_MEM * KV_SLOTS, HEAD_DIM)
    conv_zero = jnp.zeros((N_A_LAYERS, bp, CONV_WIDTH - 1, CONV_CH), x_prompt.dtype)
    weights = (a_w_in, a_w_dw, a_b_dw, a_cn_g, a_cn_b, a_w_out, b_w_in, b_w_out, w_kv_shared, rel_bias, ln_g, ln_b)

    y_p, conv_p, bufs_p = _trunk(x_prompt, conv_zero, mem_p, None, *weights)
    kv_state = [s.reshape(bs, s.shape[1] * KV_SLOTS, HEAD_DIM) for s in (state_kv_g0, state_kv_g1, state_kv_g2)]
    mem_s = cache_mem_kv.reshape(DEPTH, bs, N_MEM * KV_SLOTS, HEAD_DIM)
    y_s, conv_s, bufs_s = _trunk(x_sample, state_conv, mem_s, kv_state, *weights)

    return (y_p, y_s, conv_p, conv_s, bufs_p[0], bufs_s[0], bufs_p[1], bufs_s[1], bufs_p[2], bufs_s[2], new_mem_kv)
```

```python
import functools
import math

import jax
import jax.numpy as jnp
from jax.experimental import pallas as pl
from jax.experimental.pallas import tpu as pltpu

D_MODEL = 2048
DEPTH = 4
HEAD_DIM = 128
MEM_HEADS = 4
MEM_WIDTH = MEM_HEADS * HEAD_DIM
N_MEM = 256
MIX_WIDTH = D_MODEL - MEM_WIDTH
N_A_LAYERS = DEPTH // 2
CONV_CH = MIX_WIDTH
CONV_WIDTH = 31
B_GROUPS = ((128, 1), (512, 4), (2048, 16))
N_GROUPS = len(B_GROUPS)
HEADS_PER_GROUP = 4
B_HEADS = N_GROUPS * HEADS_PER_GROUP
GROUP_WIDTH = HEADS_PER_GROUP * HEAD_DIM
KV_SLOTS = 2 * HEADS_PER_GROUP
B_OUT = GROUP_WIDTH
REL_BUCKETS = 32
REL_MAX_DIST = 2048
BAND = 128
ATTN_TOKENS = BAND * max(d for _, d in B_GROUPS)
LN_EPS = 1e-5
ALPHA = (2 * DEPTH) ** 0.25
NEG_INF = -1e30
SCALE = HEAD_DIM ** -0.5

HALO = 32
VMEM_LIMIT = 56 * 1024 * 1024
BF16 = jnp.bfloat16
F32 = jnp.float32


def _params(*sem):
    return pltpu.CompilerParams(dimension_semantics=sem, vmem_limit_bytes=VMEM_LIMIT)


def _sigmoid(x):
    return 1.0 / (1.0 + jnp.exp(-x))


def _silu(x):
    return x * _sigmoid(x)


def _layernorm(z, g, b):
    mu = jnp.mean(z, -1, keepdims=True)
    zc = z - mu
    var = jnp.mean(zc * zc, -1, keepdims=True)
    return zc * jax.lax.rsqrt(var + LN_EPS) * g + b


def _lhs_bf16(x_ref, scratch, col_axis):
    if not scratch:
        return x_ref[...]
    xb_ref, = scratch

    @pl.when(pl.program_id(col_axis) == 0)
    def _():
        xb_ref[...] = x_ref[...].astype(BF16)

    return xb_ref[...]


def _row_tile(x):
    m, k = x.shape
    if x.dtype == BF16:
        return min(m, 2048), []
    return min(m, 1024), [pltpu.VMEM((min(m, 1024), k), BF16)]


def _mm_kernel(x_ref, w_ref, o_ref, *scratch):
    xb = _lhs_bf16(x_ref, scratch, 2)
    o_ref[...] = jnp.dot(xb, w_ref[...].astype(BF16), preferred_element_type=F32)


def _mm(x, w, layer=None, *, tn=512):
    m, k = x.shape
    n = w.shape[2]
    tm, scratch = _row_tile(x)
    l0, nl = (0, w.shape[0]) if layer is None else (layer, 1)
    out = pl.pallas_call(
        _mm_kernel,
        grid=(nl, m // tm, n // tn),
        in_specs=[pl.BlockSpec((tm, k), lambda l, i, j: (i, 0)),
                  pl.BlockSpec((None, k, tn), lambda l, i, j: (l0 + l, 0, j))],
        out_specs=pl.BlockSpec((None, tm, tn), lambda l, i, j: (l, i, j)),
        out_shape=jax.ShapeDtypeStruct((nl, m, n), F32),
        scratch_shapes=scratch,
        compiler_params=_params("arbitrary", "arbitrary", "arbitrary"),
        name="mm",
    )(x, w)
    return out if layer is None else out.reshape(m, n)


def _store_kv_slots(o_ref, lead, slot0, vals):
    rows = vals.shape[0]
    for j in range(vals.shape[1] // HEAD_DIM):
        o_ref[lead + (pl.ds(slot0 + j, rows, stride=KV_SLOTS), slice(None))] = vals[:, j * HEAD_DIM:(j + 1) * HEAD_DIM]


def _mem_proj_kernel(x_ref, w_ref, o_ref, xb_ref):
    @pl.when(pl.program_id(0) == 0)
    def _():
        xb_ref[...] = x_ref[...].astype(BF16)

    _store_kv_slots(o_ref, (), 0, jnp.dot(xb_ref[...], w_ref[...].astype(BF16), preferred_element_type=F32))


def _mem_proj(x, w):
    m, k = x.shape
    nl, _, n = w.shape
    return pl.pallas_call(
        _mem_proj_kernel,
        grid=(nl,),
        in_specs=[pl.BlockSpec((m, k), lambda l: (0, 0)),
                  pl.BlockSpec((None, k, n), lambda l: (l, 0, 0))],
        out_specs=pl.BlockSpec((None, m * KV_SLOTS, HEAD_DIM), lambda l: (l, 0, 0)),
        out_shape=jax.ShapeDtypeStruct((nl, m * KV_SLOTS, HEAD_DIM), F32),
        scratch_shapes=[pltpu.VMEM((m, k), BF16)],
        compiler_params=_params("arbitrary"),
        name="mem_proj",
    )(x, w)


def _kv_window_kernel(k_ref, v_ref, o_ref):
    _store_kv_slots(o_ref, (0,), 0, k_ref[0])
    _store_kv_slots(o_ref, (0,), HEADS_PER_GROUP, v_ref[0])


def _kv_window(kv, g, w):
    b, t, _ = kv.shape
    rows = min(w, 512)
    first = (t - w) // rows
    return pl.pallas_call(
        _kv_window_kernel,
        grid=(b, w // rows),
        in_specs=[pl.BlockSpec((1, rows, GROUP_WIDTH), lambda bi, i: (bi, first + i, g)),
                  pl.BlockSpec((1, rows, GROUP_WIDTH), lambda bi, i: (bi, first + i, N_GROUPS + g))],
        out_specs=pl.BlockSpec((1, rows * KV_SLOTS, HEAD_DIM), lambda bi, i: (bi, i, 0)),
        out_shape=jax.ShapeDtypeStruct((b, w * KV_SLOTS, HEAD_DIM), F32),
        compiler_params=_params("arbitrary", "arbitrary"),
        name="kv_window",
    )(kv, kv)


A_TN = 256
N_GLU = CONV_CH // A_TN
N_QM = MEM_WIDTH // A_TN


def _a_proj_kernel(x_ref, wa_ref, wb_ref, u_ref, g_ref, *scratch):
    j = pl.program_id(1)
    xb = _lhs_bf16(x_ref, scratch, 1)

    def proj(w_ref):
        return jnp.dot(xb, w_ref[...].astype(BF16), preferred_element_type=F32)

    @pl.when(j < N_GLU)
    def _():
        u_ref[...] = proj(wa_ref) * _sigmoid(proj(wb_ref))

    is_qm = (j >= 2 * N_GLU) & (j < 2 * N_GLU + N_QM)

    @pl.when(is_qm)
    def _():
        g_ref[...] = proj(wa_ref).astype(BF16)

    @pl.when((j >= N_GLU) & jnp.logical_not(is_qm))
    def _():
        g_ref[...] = _silu(proj(wa_ref)).astype(BF16)


def _a_proj(x, w, layer):
    m, k = x.shape
    tm, scratch = _row_tile(x)
    n_steps = 2 * N_GLU + 2 * N_QM
    return pl.pallas_call(
        _a_proj_kernel,
        grid=(m // tm, n_steps),
        in_specs=[pl.BlockSpec((tm, k), lambda i, j: (i, 0)),
                  pl.BlockSpec((None, k, A_TN), lambda i, j: (layer, 0, jnp.where(j < N_GLU, j, j + N_GLU))),
                  pl.BlockSpec((None, k, A_TN), lambda i, j: (layer, 0, jnp.minimum(j + N_GLU, 2 * N_GLU - 1)))],
        out_specs=[pl.BlockSpec((tm, A_TN), lambda i, j: (i, jnp.minimum(j, N_GLU - 1))),
                   pl.BlockSpec((tm, A_TN), lambda i, j: (i, jnp.maximum(j - N_GLU, 0)))],
        out_shape=[jax.ShapeDtypeStruct((m, CONV_CH), F32),
                   jax.ShapeDtypeStruct((m, CONV_CH + 2 * MEM_WIDTH), BF16)],
        scratch_shapes=scratch,
        compiler_params=_params("arbitrary", "arbitrary"),
        name="a_proj",
    )(x, w, w)


def _out_ln_kernel(cm_ref, w_ref, x_ref, g_ref, b_ref, o_ref, ob_ref, wb_ref):
    @pl.when(pl.program_id(0) == 0)
    def _():
        wb_ref[...] = w_ref[...].astype(BF16)

    y = jnp.dot(cm_ref[...], wb_ref[...], preferred_element_type=F32)
    out = _layernorm(ALPHA * x_ref[...] + y, g_ref[...], b_ref[...])
    o_ref[...] = out
    ob_ref[...] = out.astype(BF16)


def _out_ln(cm, w, layer, x, g, b):
    m, kk = cm.shape
    d = w.shape[2]
    tm = min(m, 512)
    row = pl.BlockSpec((tm, d), lambda i: (i, 0))
    return pl.pallas_call(
        _out_ln_kernel,
        grid=(m // tm,),
        in_specs=[pl.BlockSpec((tm, kk), lambda i: (i, 0)),
                  pl.BlockSpec((None, kk, d), lambda i: (layer, 0, 0), pipeline_mode=pl.Buffered(1)),
                  row,
                  pl.BlockSpec((1, d), lambda i: (0, 0)),
                  pl.BlockSpec((1, d), lambda i: (0, 0))],
        out_specs=[row, row],
        out_shape=[jax.ShapeDtypeStruct((m, d), F32), jax.ShapeDtypeStruct((m, d), BF16)],
        scratch_shapes=[pltpu.VMEM((kk, d), BF16)],
        compiler_params=_params("arbitrary"),
        name="out_ln",
    )(cm, w, x, g.reshape(1, d), b.reshape(1, d))


def _kv_rows(ref, lead, slot, n):
    return ref[lead + (pl.ds(slot, n, stride=KV_SLOTS), slice(None))]


def _mem_attn(qm, mem_ref):
    outs = []
    for h in range(MEM_HEADS):
        lo = h * HEAD_DIM
        q = qm[:, lo:lo + HEAD_DIM]
        k = _kv_rows(mem_ref, (0,), h, N_MEM).astype(BF16)
        v = _kv_rows(mem_ref, (0,), MEM_HEADS + h, N_MEM).astype(BF16)
        s = jax.lax.dot_general(q, k, (((1,), (1,)), ((), ())), preferred_element_type=F32) * SCALE
        s = s - jnp.max(s, -1, keepdims=True)
        e = jnp.exp(s)
        p = e / jnp.sum(e, -1, keepdims=True)
        outs.append(jnp.dot(p.astype(BF16), v, preferred_element_type=F32))
    return jnp.concatenate(outs, -1)


def _a_mix_kernel(*refs, tt, tr, has_halo):
    if has_halo:
        (u_ref, sgate, qm, sgm, u_prev, cprev, mem, wdw, bdw, cng, cnb,
         cm_out, nc_out, ext, wrep, conv) = refs
    else:
        (u_ref, sgate, qm, sgm, cprev, mem, wdw, bdw, cng, cnb,
         cm_out, nc_out, ext, wrep, conv) = refs
    i = pl.program_id(1)
    n_blk = CONV_CH // HEAD_DIM

    @pl.when((pl.program_id(0) == 0) & (i == 0))
    def _():
        for k in range(CONV_WIDTH):
            wrep[k] = jnp.broadcast_to(wdw[k:k + 1, :], (8, CONV_CH))

    for c in range(n_blk):
        lanes = slice(c * HEAD_DIM, (c + 1) * HEAD_DIM)
        if has_halo:
            ext[c, 0:HALO, :] = jnp.where(i == 0, cprev[0, :, lanes], u_prev[0, :, lanes])
        else:
            ext[c, 0:HALO, :] = cprev[0, :, lanes]
        ext[c, HALO:HALO + tt, :] = u_ref[0, :, lanes]
        nc_out[0, :, lanes] = ext[c, HALO + tt - (CONV_WIDTH - 1):HALO + tt, :]

    off0 = HALO - (CONV_WIDTH - 1)
    for c in range(n_blk):
        lanes = slice(c * HEAD_DIM, (c + 1) * HEAD_DIM)

        def body(rc, carry, c=c, lanes=lanes):
            r0 = pl.multiple_of(rc * tr, 8)
            acc = jnp.zeros((tr // 8, 8, HEAD_DIM), F32)
            for k in range(CONV_WIDTH):
                win = ext[c, pl.ds(r0 + k + off0, tr), :].reshape(tr // 8, 8, HEAD_DIM)
                acc = acc + win * wrep[k, :, lanes][None]
            conv[pl.ds(r0, tr), lanes] = acc.reshape(tr, HEAD_DIM) + bdw[:, lanes]
            return carry

        jax.lax.fori_loop(0, tt // tr, body, 0)

    zn = _layernorm(conv[...], cng[...], cnb[...])
    cm_out[0, :, 0:CONV_CH] = (_silu(zn) * sgate[0].astype(F32)).astype(BF16)
    cm_out[0, :, CONV_CH:D_MODEL] = (_mem_attn(qm[0], mem) * sgm[0].astype(F32)).astype(BF16)


def _a_mix(u, gates, conv_prev, mem, layer, w_dw, b_dw, cn_g, cn_b):
    b, t, _ = u.shape
    tt = min(t, 256)
    tr = min(tt, 128)
    nt = t // tt
    has_halo = nt > 1
    cprev = jnp.pad(conv_prev, ((0, 0), (HALO - (CONV_WIDTH - 1), 0), (0, 0)))
    c3 = CONV_CH
    qcol = CONV_CH // MEM_WIDTH

    in_specs = [pl.BlockSpec((1, tt, c3), lambda bi, i: (bi, i, 0)),
                pl.BlockSpec((1, tt, c3), lambda bi, i: (bi, i, 0)),
                pl.BlockSpec((1, tt, MEM_WIDTH), lambda bi, i: (bi, i, qcol)),
                pl.BlockSpec((1, tt, MEM_WIDTH), lambda bi, i: (bi, i, qcol + 1))]
    args = [u, gates, gates, gates]
    if has_halo:
        per = tt // HALO
        in_specs += [pl.BlockSpec((1, HALO, c3), lambda bi, i: (bi, jnp.maximum(i * per - 1, 0), 0))]
        args += [u]
    in_specs += [pl.BlockSpec((1, HALO, c3), lambda bi, i: (bi, 0, 0)),
                 pl.BlockSpec((None, 1, N_MEM * KV_SLOTS, HEAD_DIM), lambda bi, i: (layer, bi, 0, 0)),
                 pl.BlockSpec((CONV_WIDTH, c3), lambda bi, i: (0, 0)),
                 pl.BlockSpec((1, c3), lambda bi, i: (0, 0)),
                 pl.BlockSpec((1, c3), lambda bi, i: (0, 0)),
                 pl.BlockSpec((1, c3), lambda bi, i: (0, 0))]
    args += [cprev, mem, w_dw, b_dw.reshape(1, c3), cn_g.reshape(1, c3), cn_b.reshape(1, c3)]

    return pl.pallas_call(
        functools.partial(_a_mix_kernel, tt=tt, tr=tr, has_halo=has_halo),
        grid=(b, nt),
        in_specs=in_specs,
        out_specs=[pl.BlockSpec((1, tt, D_MODEL), lambda bi, i: (bi, i, 0)),
                   pl.BlockSpec((1, CONV_WIDTH - 1, c3), lambda bi, i: (bi, 0, 0))],
        out_shape=[jax.ShapeDtypeStruct((b, t, D_MODEL), BF16),
                   jax.ShapeDtypeStruct((b, CONV_WIDTH - 1, c3), F32)],
        scratch_shapes=[pltpu.VMEM((c3 // HEAD_DIM, tt + HALO, HEAD_DIM), F32),
                        pltpu.VMEM((CONV_WIDTH, 8, c3), F32),
                        pltpu.VMEM((tt, c3), F32)],
        compiler_params=_params("arbitrary", "arbitrary"),
        name="a_mix",
    )(*args)


def _rows(ref, start, size, stride):
    if stride == 1:
        return ref[0, start:start + size, :]
    return ref[0, pl.ds(start, size, stride=stride), :]


def _b_attn_kernel(q0, q1, q2, k0, k1, k2, v0, v1, v2, hk0, hk1, hk2, hv0, hv1, hv2, bias_ref, gate_ref,
                   o_ref, num_scr, m_scr, s_scr):
    first = jnp.minimum(pl.program_id(2), 1)
    qs, ks, vs = (q0, q1, q2), (k0, k1, k2), (v0, v1, v2)
    hks, hvs = (hk0, hk1, hk2), (hv0, hv1, hv2)
    dn = (((1,), (1,)), ((), ()))
    for g, (_, d) in enumerate(B_GROUPS):
        for r in range(d):
            for s in range(ATTN_TOKENS // (BAND * d)):
                start = r + d * BAND * s
                q = _rows(qs[g], start, BAND, d)
                if s == 0:
                    k = jnp.concatenate([_rows(hks[g], r, BAND, d), _rows(ks[g], r, BAND, d)], 0)
                    v = jnp.concatenate([_rows(hvs[g], r, BAND, d), _rows(vs[g], r, BAND, d)], 0)
                    bias = bias_ref[g, first, 0]
                else:
                    k = _rows(ks[g], start - d * BAND, 2 * BAND, d)
                    v = _rows(vs[g], start - d * BAND, 2 * BAND, d)
                    bias = bias_ref[g, 1, 0]
                logits = jax.lax.dot_general(q.astype(BF16), k.astype(BF16), dn, preferred_element_type=F32)
                logits = logits * SCALE + bias
                m = jnp.max(logits, -1, keepdims=True)
                e = jnp.exp(logits - m)
                ssum = jnp.sum(e, -1, keepdims=True)
                num = jnp.dot(e.astype(BF16), v.astype(BF16), preferred_element_type=F32)
                if d == 1:
                    dst = pl.ds(start, BAND)
                else:
                    dst = pl.ds(start, BAND, stride=d)
                num_scr[g, dst, :] = num
                m_scr[g, dst, :] = jnp.broadcast_to(m, (BAND, HEAD_DIM))
                s_scr[g, dst, :] = jnp.broadcast_to(ssum, (BAND, HEAD_DIM))

    def merge(c, carry):
        sl = pl.ds(pl.multiple_of(c * BAND, BAND), BAND)
        ms = [m_scr[g, sl, :] for g in range(N_GROUPS)]
        m_all = jnp.maximum(jnp.maximum(ms[0], ms[1]), ms[2])
        coef = [jnp.exp(m - m_all) for m in ms]
        num = coef[0] * num_scr[0, sl, :]
        den = coef[0] * s_scr[0, sl, :]
        for g in range(1, N_GROUPS):
            num = num + coef[g] * num_scr[g, sl, :]
            den = den + coef[g] * s_scr[g, sl, :]
        o_ref[0, sl, :] = (num / den * _silu(gate_ref[0, sl, :])).astype(BF16)
        return carry

    jax.lax.fori_loop(0, ATTN_TOKENS // BAND, merge, 0)


def _b_attn(h, kv, band_bias):
    b, t, _ = h.shape
    tb = ATTN_TOKENS
    blk = (1, tb, HEAD_DIM)

    def cur(col0):
        return lambda bi, hi, i: (bi, i, col0 + hi)

    def halo_spec(d, col0):
        rows = BAND * d
        per = tb // rows
        return pl.BlockSpec((1, rows, HEAD_DIM), lambda bi, hi, i: (bi, jnp.maximum(i * per - 1, 0), col0 + hi))

    hpg = HEADS_PER_GROUP
    in_specs = ([pl.BlockSpec(blk, cur(hpg * g)) for g in range(N_GROUPS)]
                + [pl.BlockSpec(blk, cur(hpg * g)) for g in range(N_GROUPS)]
                + [pl.BlockSpec(blk, cur(B_HEADS + hpg * g)) for g in range(N_GROUPS)]
                + [halo_spec(d, hpg * g) for g, (_, d) in enumerate(B_GROUPS)]
                + [halo_spec(d, B_HEADS + hpg * g) for g, (_, d) in enumerate(B_GROUPS)]
                + [pl.BlockSpec((N_GROUPS, 2, 1, BAND, 2 * BAND), lambda bi, hi, i: (0, 0, hi, 0, 0)),
                   pl.BlockSpec(blk, cur(B_HEADS))])
    return pl.pallas_call(
        _b_attn_kernel,
        grid=(b, hpg, t // tb),
        in_specs=in_specs,
        out_specs=pl.BlockSpec(blk, lambda bi, hi, i: (bi, i, hi)),
        out_shape=jax.ShapeDtypeStruct((b, t, B_OUT), BF16),
        scratch_shapes=[pltpu.VMEM((N_GROUPS, tb, HEAD_DIM), F32)] * 3,
        compiler_params=_params("parallel", "parallel", "arbitrary"),
        name="b_attn",
    )(h, h, h, kv, kv, kv, kv, kv, kv, kv, kv, kv, kv, kv, kv, band_bias, h)


def _b_out_kernel(o_ref, qm_ref, gm_ref, mem_ref, w_ref, x_ref, g_ref, b_ref, y_ref, yb_ref, wb_ref):
    @pl.when((pl.program_id(0) == 0) & (pl.program_id(1) == 0))
    def _():
        wb_ref[...] = w_ref[...].astype(BF16)

    mbranch = _mem_attn(qm_ref[0].astype(BF16), mem_ref) * _silu(gm_ref[0])
    y = jnp.dot(o_ref[0], wb_ref[0:B_OUT, :], preferred_element_type=F32)
    y = y + jnp.dot(mbranch.astype(BF16), wb_ref[B_OUT:B_OUT + MEM_WIDTH, :], preferred_element_type=F32)
    out = _layernorm(ALPHA * x_ref[0] + y, g_ref[...], b_ref[...])
    y_ref[0] = out
    yb_ref[0] = out.astype(BF16)


def _b_out(o, h, mem, layer, w, wl, x, g, b):
    bsz, t, d = x.shape
    tt = min(t, 512)
    kk = w.shape[1]
    qcol = (MIX_WIDTH + B_OUT) // MEM_WIDTH
    row = pl.BlockSpec((1, tt, d), lambda bi, i: (bi, i, 0))
    return pl.pallas_call(
        _b_out_kernel,
        grid=(bsz, t // tt),
        in_specs=[pl.BlockSpec((1, tt, B_OUT), lambda bi, i: (bi, i, 0)),
                  pl.BlockSpec((1, tt, MEM_WIDTH), lambda bi, i: (bi, i, qcol)),
                  pl.BlockSpec((1, tt, MEM_WIDTH), lambda bi, i: (bi, i, qcol + 1)),
                  pl.BlockSpec((None, 1, N_MEM * KV_SLOTS, HEAD_DIM), lambda bi, i: (layer, bi, 0, 0)),
                  pl.BlockSpec((None, kk, d), lambda bi, i: (wl, 0, 0), pipeline_mode=pl.Buffered(1)),
                  row,
                  pl.BlockSpec((1, d), lambda bi, i: (0, 0)),
                  pl.BlockSpec((1, d), lambda bi, i: (0, 0))],
        out_specs=[row, row],
        out_shape=[jax.ShapeDtypeStruct((bsz, t, d), F32), jax.ShapeDtypeStruct((bsz, t, d), BF16)],
        scratch_shapes=[pltpu.VMEM((kk, d), BF16)],
        compiler_params=_params("arbitrary", "arbitrary"),
        name="b_out",
    )(o, h, h, mem, w, x, g.reshape(1, d), b.reshape(1, d))


def _samp_attn_kernel(q_ref, st_ref, kn_ref, vn_ref, bs_ref, bn_ref, num_ref, m_ref, s_ref, *, nq, w):
    dn = (((1,), (1,)), ((), ()))
    for h in range(HEADS_PER_GROUP):
        lo = h * HEAD_DIM
        q = q_ref[0, :, lo:lo + HEAD_DIM].astype(BF16)
        ks = _kv_rows(st_ref, (0,), h, w).astype(BF16)
        vs = _kv_rows(st_ref, (0,), HEADS_PER_GROUP + h, w).astype(BF16)
        kn = kn_ref[0, :, lo:lo + HEAD_DIM].astype(BF16)
        vn = vn_ref[0, :, lo:lo + HEAD_DIM].astype(BF16)
        ls = jax.lax.dot_general(q, ks, dn, preferred_element_type=F32) * SCALE + bs_ref[h]
        ln = jax.lax.dot_general(q, kn, dn, preferred_element_type=F32) * SCALE + bn_ref[h]
        m = jnp.maximum(jnp.max(ls, -1, keepdims=True), jnp.max(ln, -1, keepdims=True))
        es = jnp.exp(ls - m)
        en = jnp.exp(ln - m)
        ssum = jnp.sum(es, -1, keepdims=True) + jnp.sum(en, -1, keepdims=True)
        num_ref[0, :, lo:lo + HEAD_DIM] = (jnp.dot(es.astype(BF16), vs, preferred_element_type=F32)
                                           + jnp.dot(en.astype(BF16), vn, preferred_element_type=F32))
        m_ref[0, :, lo:lo + HEAD_DIM] = jnp.broadcast_to(m, (nq, HEAD_DIM))
        s_ref[0, :, lo:lo + HEAD_DIM] = jnp.broadcast_to(ssum, (nq, HEAD_DIM))


def _samp_attn(h, qcol, state, kv, kcol, vcol, bias_state, bias_new):
    b, nq, _ = h.shape
    w = state.shape[1] // KV_SLOTS
    blk = (1, nq, GROUP_WIDTH)
    out = jax.ShapeDtypeStruct((b, nq, GROUP_WIDTH), F32)
    return pl.pallas_call(
        functools.partial(_samp_attn_kernel, nq=nq, w=w),
        grid=(b,),
        in_specs=[pl.BlockSpec(blk, lambda i: (i, 0, qcol)),
                  pl.BlockSpec((1, w * KV_SLOTS, HEAD_DIM), lambda i: (i, 0, 0)),
                  pl.BlockSpec(blk, lambda i: (i, 0, kcol)),
                  pl.BlockSpec(blk, lambda i: (i, 0, vcol)),
                  pl.BlockSpec((HEADS_PER_GROUP, nq, w), lambda i: (0, 0, 0)),
                  pl.BlockSpec((HEADS_PER_GROUP, nq, nq), lambda i: (0, 0, 0))],
        out_specs=[pl.BlockSpec(blk, lambda i: (i, 0, 0))] * 3,
        out_shape=[out] * 3,
        compiler_params=_params("parallel"),
        name="samp_attn",
    )(h, state, kv, kv, bias_state, bias_new)


def _samp_merge_kernel(n0, m0, s0, n1, m1, s1, n2, m2, s2, gate, o_out):
    ms = [m0[0], m1[0], m2[0]]
    m_all = jnp.maximum(jnp.maximum(ms[0], ms[1]), ms[2])
    coef = [jnp.exp(m - m_all) for m in ms]
    num = coef[0] * n0[0] + coef[1] * n1[0] + coef[2] * n2[0]
    den = coef[0] * s0[0] + coef[1] * s1[0] + coef[2] * s2[0]
    o_out[0] = (num / den * _silu(gate[0])).astype(BF16)


def _samp_merge(parts, h):
    b, nq, _ = h.shape
    spec = pl.BlockSpec((1, nq, GROUP_WIDTH), lambda i: (i, 0, 0))
    flat = [a for p in parts for a in p]
    return pl.pallas_call(
        _samp_merge_kernel,
        grid=(b,),
        in_specs=[spec] * 9 + [pl.BlockSpec((1, nq, GROUP_WIDTH), lambda i: (i, 0, MIX_WIDTH // GROUP_WIDTH))],
        out_specs=spec,
        out_shape=jax.ShapeDtypeStruct((b, nq, GROUP_WIDTH), BF16),
        compiler_params=_params("parallel"),
        name="samp_merge",
    )(*flat, h)


def _shift_state_kernel(main_ref, next_ref, new_ref, o_ref, *, rows, shift):
    last = pl.program_id(1) == pl.num_programs(1) - 1
    if rows > shift:
        o_ref[0, 0:rows - shift, :] = main_ref[0, shift:rows, :]
    o_ref[0, rows - shift:rows, :] = jnp.where(last, new_ref[0], next_ref[0])


def _shift_state(state, new):
    b, total, lanes = state.shape
    shift = new.shape[1]
    rows = min(total, 4096)
    nblk = total // rows
    per = rows // shift
    return pl.pallas_call(
        functools.partial(_shift_state_kernel, rows=rows, shift=shift),
        grid=(b, nblk),
        in_specs=[pl.BlockSpec((1, rows, lanes), lambda bi, i: (bi, i, 0)),
                  pl.BlockSpec((1, shift, lanes), lambda bi, i: (bi, jnp.minimum((i + 1) * per, nblk * per - 1), 0)),
                  pl.BlockSpec((1, shift, lanes), lambda bi, i: (bi, 0, 0))],
        out_specs=pl.BlockSpec((1, rows, lanes), lambda bi, i: (bi, i, 0)),
        out_shape=jax.ShapeDtypeStruct(state.shape, state.dtype),
        compiler_params=_params("parallel", "arbitrary"),
        name="shift_state",
    )(state, state, new)


def _t5_bucket(dist):
    max_exact = REL_BUCKETS // 2
    safe = jnp.maximum(dist, 1).astype(F32)
    large = max_exact + (jnp.log(safe / max_exact) / math.log(REL_MAX_DIST / max_exact)
                         * (REL_BUCKETS - max_exact)).astype(jnp.int32)
    large = jnp.minimum(large, REL_BUCKETS - 1)
    return jnp.where(dist < max_exact, dist, large)


def _group_bias(rel_bias, g):
    w, d = B_GROUPS[g]
    dist = d * jnp.arange(w // d + 1, dtype=jnp.int32)
    tab = rel_bias[_t5_bucket(dist)]
    return tab[:, g * HEADS_PER_GROUP:(g + 1) * HEADS_PER_GROUP].T.astype(F32)


def _toeplitz(vec, rows, cols):
    hh, p = vec.shape
    flat = jnp.tile(vec, (1, rows))[:, :rows * (p - 1)]
    return flat.reshape(hh, rows, p - 1)[:, :, :cols]


def _band_bias(bias):
    period = 3 * BAND
    vec = jnp.full((bias.shape[0], period), NEG_INF, F32)
    vec = jax.lax.dynamic_update_slice(vec, bias[:, ::-1], (0, 0))
    mat = _toeplitz(vec, BAND, 2 * BAND)
    c = jnp.arange(2 * BAND, dtype=jnp.int32)[None, None, :]
    first = jnp.where(c >= BAND, mat, NEG_INF)
    return jnp.stack([first, mat])


def _sample_bias(bias, w, d, nq):
    hh = bias.shape[0]
    neg = jnp.full_like(bias, NEG_INF)
    by_dist = jnp.stack([bias] + [neg] * (d - 1), -1).reshape(hh, -1)
    by_dist = jnp.concatenate([by_dist[:, :w + 1], jnp.full((hh, nq), NEG_INF, F32)], 1)
    rev = by_dist[:, ::-1]
    bs = jnp.stack([rev[:, nq - n:nq - n + w] for n in range(nq)], 1)
    small = jnp.concatenate([jnp.full((hh, nq), NEG_INF, F32), by_dist[:, :nq]], 1)[:, ::-1]
    bn = jnp.stack([small[:, nq - 1 - n:2 * nq - 1 - n] for n in range(nq)], 1)
    return bs, bn


def _trunk(x, conv_prev, mem_kv, kv_state, a_w_in, a_w_dw, a_b_dw, a_cn_g, a_cn_b, a_w_out,
           b_w_in, b_w_out, w_kv_shared, rel_bias, ln_g, ln_b):
    b, t, d_model = x.shape
    m = b * t
    x2 = x.reshape(m, d_model)
    xb = x2
    new_conv = []
    for l in range(N_A_LAYERS):
        u, gates = _a_proj(xb, a_w_in, l)
        cm, nc = _a_mix(u.reshape(b, t, -1), gates.reshape(b, t, -1), conv_prev[l], mem_kv, l,
                        a_w_dw[l], a_b_dw[l], a_cn_g[l], a_cn_b[l])
        new_conv.append(nc)
        x2, xb = _out_ln(cm.reshape(m, d_model), a_w_out, l, x2, ln_g[l], ln_b[l])

    kv = _mm(xb, w_kv_shared[None], 0).reshape(b, t, 2 * MIX_WIDTH)
    vcol0 = MIX_WIDTH // GROUP_WIDTH
    biases = [_group_bias(rel_bias, g) for g in range(N_GROUPS)]
    kv_new = [_kv_window(kv, g, min(w, t)) for g, (w, _) in enumerate(B_GROUPS)]
    if kv_state is None:
        new_bufs = kv_new
        band_bias = jnp.stack([_band_bias(bi) for bi in biases])
    else:
        new_bufs = [_shift_state(st, kn) for st, kn in zip(kv_state, kv_new)]
        samp_bias = [_sample_bias(bi, w, d, t) for bi, (w, d) in zip(biases, B_GROUPS)]
    new_bufs = [nb.reshape(b, -1, 2, HEADS_PER_GROUP, HEAD_DIM) for nb in new_bufs]

    x3 = x2.reshape(b, t, d_model)
    for i in range(DEPTH - N_A_LAYERS):
        l = N_A_LAYERS + i
        h = _mm(xb.reshape(m, d_model), b_w_in, i).reshape(b, t, -1)
        if kv_state is None:
            o = _b_attn(h, kv, band_bias)
        else:
            parts = [_samp_attn(h, g, kv_state[g], kv, g, vcol0 + g, *samp_bias[g]) for g in range(N_GROUPS)]
            o = _samp_merge(parts, h)
        x3, xb = _b_out(o, h, mem_kv, l, b_w_out, i, x3, ln_g[l], ln_b[l])
    return x3, jnp.stack(new_conv), new_bufs


def kernel(x_prompt, x_sample, state_conv, state_kv_g0, state_kv_g1, state_kv_g2, cache_mem_kv, mem_prompt,
           a_w_in, a_w_dw, a_b_dw, a_cn_g, a_cn_b, a_w_out, b_w_in, b_w_out, w_kv_shared, w_mem_kv,
           rel_bias, ln_g, ln_b):
    bp = x_prompt.shape[0]
    bs = x_sample.shape[0]
    mem_p = _mem_proj(mem_prompt.reshape(bp * N_MEM, D_MODEL), w_mem_kv).reshape(
        DEPTH, bp, N_MEM * KV_SLOTS, HEAD_DIM)
    new_mem_kv = mem_p.reshape(DEPTH, bp, N_MEM, 2, MEM_HEADS, HEAD_DIM)
    conv_zero = jnp.zeros((N_A_LAYERS, bp, CONV_WIDTH - 1, CONV_CH), x_prompt.dtype)
    weights = (a_w_in, a_w_dw, a_b_dw, a_cn_g, a_cn_b, a_w_out, b_w_in, b_w_out, w_kv_shared, rel_bias, ln_g, ln_b)

    y_p, conv_p, bufs_p = _trunk(x_prompt, conv_zero, mem_p, None, *weights)
    kv_state = [s.reshape(bs, s.shape[1] * KV_SLOTS, HEAD_DIM) for s in (state_kv_g0, state_kv_g1, state_kv_g2)]
    mem_s = cache_mem_kv.reshape(DEPTH, bs, N_MEM * KV_SLOTS, HEAD_DIM)
    y_s, conv_s, bufs_s = _trunk(x_sample, state_conv, mem_s, kv_state, *weights)

    return (y_p, y_s, conv_p, conv_s, bufs_p[0], bufs_s[0], bufs_p[1], bufs_s[1], bufs_p[2], bufs_s[2], new_mem_kv)
```

```python
import functools
import math

import jax
import jax.numpy as jnp
from jax.experimental import pallas as pl
from jax.experimental.pallas import tpu as pltpu

D_MODEL = 2048
DEPTH = 4
HEAD_DIM = 128
MEM_HEADS = 4
MEM_WIDTH = MEM_HEADS * HEAD_DIM
N_MEM = 256
MIX_WIDTH = D_MODEL - MEM_WIDTH
N_A_LAYERS = DEPTH // 2
CONV_CH = MIX_WIDTH
CONV_WIDTH = 31
B_GROUPS = ((128, 1), (512, 4), (2048, 16))
N_GROUPS = len(B_GROUPS)
HEADS_PER_GROUP = 4
B_HEADS = N_GROUPS * HEADS_PER_GROUP
GROUP_WIDTH = HEADS_PER_GROUP * HEAD_DIM
KV_SLOTS = 2 * HEADS_PER_GROUP
B_OUT = GROUP_WIDTH
REL_BUCKETS = 32
REL_MAX_DIST = 2048
BAND = 128
ATTN_TOKENS = BAND * max(d for _, d in B_GROUPS)
LN_EPS = 1e-5
ALPHA = (2 * DEPTH) ** 0.25
NEG_INF = -1e30
SCALE = HEAD_DIM ** -0.5

HALO = 32
VMEM_LIMIT = 56 * 1024 * 1024
BF16 = jnp.bfloat16
F32 = jnp.float32


def _params(*sem):
    return pltpu.CompilerParams(dimension_semantics=sem, vmem_limit_bytes=VMEM_LIMIT)


def _sigmoid(x):
    return 1.0 / (1.0 + jnp.exp(-x))


def _silu(x):
    return x * _sigmoid(x)


def _layernorm(z, g, b):
    mu = jnp.mean(z, -1, keepdims=True)
    zc = z - mu
    var = jnp.mean(zc * zc, -1, keepdims=True)
    return zc * jax.lax.rsqrt(var + LN_EPS) * g + b


def _lhs_bf16(x_ref, scratch, col_axis):
    if not scratch:
        return x_ref[...]
    xb_ref, = scratch

    @pl.when(pl.program_id(col_axis) == 0)
    def _():
        xb_ref[...] = x_ref[...].astype(BF16)

    return xb_ref[...]


def _row_tile(x):
    m, k = x.shape
    if x.dtype == BF16:
        return min(m, 2048), []
    return min(m, 1024), [pltpu.VMEM((min(m, 1024), k), BF16)]


def _mm_kernel(x_ref, w_ref, o_ref, *scratch):
    xb = _lhs_bf16(x_ref, scratch, 2)
    o_ref[...] = jnp.dot(xb, w_ref[...].astype(BF16), preferred_element_type=F32)


def _mm(x, w, layer=None, *, tn=512):
    m, k = x.shape
    n = w.shape[2]
    tm, scratch = _row_tile(x)
    l0, nl = (0, w.shape[0]) if layer is None else (layer, 1)
    out = pl.pallas_call(
        _mm_kernel,
        grid=(nl, m // tm, n // tn),
        in_specs=[pl.BlockSpec((tm, k), lambda l, i, j: (i, 0)),
                  pl.BlockSpec((None, k, tn), lambda l, i, j: (l0 + l, 0, j))],
        out_specs=pl.BlockSpec((None, tm, tn), lambda l, i, j: (l, i, j)),
        out_shape=jax.ShapeDtypeStruct((nl, m, n), F32),
        scratch_shapes=scratch,
        compiler_params=_params("arbitrary", "arbitrary", "arbitrary"),
        name="mm",
    )(x, w)
    return out if layer is None else out.reshape(m, n)


def _store_kv_slots(o_ref, lead, slot0, vals):
    rows = vals.shape[0]
    for j in range(vals.shape[1] // HEAD_DIM):
        o_ref[lead + (pl.ds(slot0 + j, rows, stride=KV_SLOTS), slice(None))] = vals[:, j * HEAD_DIM:(j + 1) * HEAD_DIM]


def _mem_proj_kernel(x_ref, w_ref, o_ref, xb_ref):
    @pl.when(pl.program_id(0) == 0)
    def _():
        xb_ref[...] = x_ref[...].astype(BF16)

    _store_kv_slots(o_ref, (), 0, jnp.dot(xb_ref[...], w_ref[...].astype(BF16), preferred_element_type=F32))


def _mem_proj(x, w):
    m, k = x.shape
    nl, _, n = w.shape
    return pl.pallas_call(
        _mem_proj_kernel,
        grid=(nl,),
        in_specs=[pl.BlockSpec((m, k), lambda l: (0, 0)),
                  pl.BlockSpec((None, k, n), lambda l: (l, 0, 0))],
        out_specs=pl.BlockSpec((None, m * KV_SLOTS, HEAD_DIM), lambda l: (l, 0, 0)),
        out_shape=jax.ShapeDtypeStruct((nl, m * KV_SLOTS, HEAD_DIM), F32),
        scratch_shapes=[pltpu.VMEM((m, k), BF16)],
        compiler_params=_params("arbitrary"),
        name="mem_proj",
    )(x, w)


def _kv_window_kernel(k_ref, v_ref, o_ref):
    _store_kv_slots(o_ref, (0,), 0, k_ref[0])
    _store_kv_slots(o_ref, (0,), HEADS_PER_GROUP, v_ref[0])


def _kv_window(kv, g, w):
    b, t, _ = kv.shape
    rows = min(w, 512)
    first = (t - w) // rows
    return pl.pallas_call(
        _kv_window_kernel,
        grid=(b, w // rows),
        in_specs=[pl.BlockSpec((1, rows, GROUP_WIDTH), lambda bi, i: (bi, first + i, g)),
                  pl.BlockSpec((1, rows, GROUP_WIDTH), lambda bi, i: (bi, first + i, N_GROUPS + g))],
        out_specs=pl.BlockSpec((1, rows * KV_SLOTS, HEAD_DIM), lambda bi, i: (bi, i, 0)),
        out_shape=jax.ShapeDtypeStruct((b, w * KV_SLOTS, HEAD_DIM), F32),
        compiler_params=_params("arbitrary", "arbitrary"),
        name="kv_window",
    )(kv, kv)


A_TN = 512
N_GLU = CONV_CH // A_TN
QM_STEP = 3 * N_GLU


def _a_proj_kernel(x_ref, w_ref, u_ref, g_ref, ga_ref, *scratch):
    j = pl.program_id(1)
    xb = _lhs_bf16(x_ref, scratch, 1)

    def proj():
        return jnp.dot(xb, w_ref[...].astype(BF16), preferred_element_type=F32)

    is_glu = j < 2 * N_GLU

    @pl.when(is_glu & (j % 2 == 0))
    def _():
        ga_ref[...] = proj()

    @pl.when(is_glu & (j % 2 == 1))
    def _():
        u_ref[...] = ga_ref[...] * _sigmoid(proj())

    @pl.when(j == QM_STEP)
    def _():
        g_ref[...] = proj().astype(BF16)

    @pl.when(jnp.logical_not(is_glu) & (j != QM_STEP))
    def _():
        g_ref[...] = _silu(proj()).astype(BF16)


def _a_proj(x, w, layer):
    m, k = x.shape
    tm, scratch = _row_tile(x)
    n_steps = 3 * N_GLU + 2 * MEM_WIDTH // A_TN

    def w_col(j):
        return jnp.where(j < 2 * N_GLU, (j % 2) * N_GLU + j // 2, j)

    return pl.pallas_call(
        _a_proj_kernel,
        grid=(m // tm, n_steps),
        in_specs=[pl.BlockSpec((tm, k), lambda i, j: (i, 0)),
                  pl.BlockSpec((None, k, A_TN), lambda i, j: (layer, 0, w_col(j)))],
        out_specs=[pl.BlockSpec((tm, A_TN), lambda i, j: (i, jnp.minimum(j // 2, N_GLU - 1))),
                   pl.BlockSpec((tm, A_TN), lambda i, j: (i, jnp.maximum(j - 2 * N_GLU, 0)))],
        out_shape=[jax.ShapeDtypeStruct((m, CONV_CH), F32),
                   jax.ShapeDtypeStruct((m, CONV_CH + 2 * MEM_WIDTH), BF16)],
        scratch_shapes=[pltpu.VMEM((tm, A_TN), F32)] + scratch,
        compiler_params=_params("arbitrary", "arbitrary"),
        name="a_proj",
    )(x, w)


def _out_ln_kernel(cm_ref, w_ref, x_ref, g_ref, b_ref, o_ref, ob_ref, wb_ref):
    @pl.when(pl.program_id(0) == 0)
    def _():
        wb_ref[...] = w_ref[...].astype(BF16)

    y = jnp.dot(cm_ref[...], wb_ref[...], preferred_element_type=F32)
    out = _layernorm(ALPHA * x_ref[...] + y, g_ref[...], b_ref[...])
    o_ref[...] = out
    ob_ref[...] = out.astype(BF16)


def _out_ln(cm, w, layer, x, g, b):
    m, kk = cm.shape
    d = w.shape[2]
    tm = min(m, 512)
    row = pl.BlockSpec((tm, d), lambda i: (i, 0))
    return pl.pallas_call(
        _out_ln_kernel,
        grid=(m // tm,),
        in_specs=[pl.BlockSpec((tm, kk), lambda i: (i, 0)),
                  pl.BlockSpec((None, kk, d), lambda i: (layer, 0, 0), pipeline_mode=pl.Buffered(1)),
                  row,
                  pl.BlockSpec((1, d), lambda i: (0, 0)),
                  pl.BlockSpec((1, d), lambda i: (0, 0))],
        out_specs=[row, row],
        out_shape=[jax.ShapeDtypeStruct((m, d), F32), jax.ShapeDtypeStruct((m, d), BF16)],
        scratch_shapes=[pltpu.VMEM((kk, d), BF16)],
        compiler_params=_params("arbitrary"),
        name="out_ln",
    )(cm, w, x, g.reshape(1, d), b.reshape(1, d))


def _kv_rows(ref, lead, slot, n):
    return ref[lead + (pl.ds(slot, n, stride=KV_SLOTS), slice(None))]


def _mem_attn(qm, mem_ref):
    outs = []
    for h in range(MEM_HEADS):
        lo = h * HEAD_DIM
        q = qm[:, lo:lo + HEAD_DIM]
        k = _kv_rows(mem_ref, (0,), h, N_MEM).astype(BF16)
        v = _kv_rows(mem_ref, (0,), MEM_HEADS + h, N_MEM).astype(BF16)
        s = jax.lax.dot_general(q, k, (((1,), (1,)), ((), ())), preferred_element_type=F32) * SCALE
        s = s - jnp.max(s, -1, keepdims=True)
        e = jnp.exp(s)
        p = e / jnp.sum(e, -1, keepdims=True)
        outs.append(jnp.dot(p.astype(BF16), v, preferred_element_type=F32))
    return jnp.concatenate(outs, -1)


def _a_mix_kernel(*refs, tt, tr, has_halo):
    if has_halo:
        (u_ref, sgate, qm, sgm, u_prev, cprev, mem, wdw, bdw, cng, cnb,
         cm_out, nc_out, ext, wrep, conv) = refs
    else:
        (u_ref, sgate, qm, sgm, cprev, mem, wdw, bdw, cng, cnb,
         cm_out, nc_out, ext, wrep, conv) = refs
    i = pl.program_id(1)
    n_blk = CONV_CH // HEAD_DIM

    @pl.when((pl.program_id(0) == 0) & (i == 0))
    def _():
        for k in range(CONV_WIDTH):
            wrep[k] = jnp.broadcast_to(wdw[k:k + 1, :], (8, CONV_CH))

    for c in range(n_blk):
        lanes = slice(c * HEAD_DIM, (c + 1) * HEAD_DIM)
        if has_halo:
            ext[c, 0:HALO, :] = jnp.where(i == 0, cprev[0, :, lanes], u_prev[0, :, lanes])
        else:
            ext[c, 0:HALO, :] = cprev[0, :, lanes]
        ext[c, HALO:HALO + tt, :] = u_ref[0, :, lanes]
        nc_out[0, :, lanes] = ext[c, HALO + tt - (CONV_WIDTH - 1):HALO + tt, :]

    off0 = HALO - (CONV_WIDTH - 1)
    for c in range(n_blk):
        lanes = slice(c * HEAD_DIM, (c + 1) * HEAD_DIM)

        def body(rc, carry, c=c, lanes=lanes):
            r0 = pl.multiple_of(rc * tr, 8)
            acc = jnp.zeros((tr // 8, 8, HEAD_DIM), F32)
            for k in range(CONV_WIDTH):
                win = ext[c, pl.ds(r0 + k + off0, tr), :].reshape(tr // 8, 8, HEAD_DIM)
                acc = acc + win * wrep[k, :, lanes][None]
            conv[pl.ds(r0, tr), lanes] = acc.reshape(tr, HEAD_DIM) + bdw[:, lanes]
            return carry

        jax.lax.fori_loop(0, tt // tr, body, 0)

    zn = _layernorm(conv[...], cng[...], cnb[...])
    cm_out[0, :, 0:CONV_CH] = (_silu(zn) * sgate[0].astype(F32)).astype(BF16)
    cm_out[0, :, CONV_CH:D_MODEL] = (_mem_attn(qm[0], mem) * sgm[0].astype(F32)).astype(BF16)


def _a_mix(u, gates, conv_prev, mem, layer, w_dw, b_dw, cn_g, cn_b):
    b, t, _ = u.shape
    tt = min(t, 256)
    tr = min(tt, 128)
    nt = t // tt
    has_halo = nt > 1
    cprev = jnp.pad(conv_prev, ((0, 0), (HALO - (CONV_WIDTH - 1), 0), (0, 0)))
    c3 = CONV_CH
    qcol = CONV_CH // MEM_WIDTH

    in_specs = [pl.BlockSpec((1, tt, c3), lambda bi, i: (bi, i, 0)),
                pl.BlockSpec((1, tt, c3), lambda bi, i: (bi, i, 0)),
                pl.BlockSpec((1, tt, MEM_WIDTH), lambda bi, i: (bi, i, qcol)),
                pl.BlockSpec((1, tt, MEM_WIDTH), lambda bi, i: (bi, i, qcol + 1))]
    args = [u, gates, gates, gates]
    if has_halo:
        per = tt // HALO
        in_specs += [pl.BlockSpec((1, HALO, c3), lambda bi, i: (bi, jnp.maximum(i * per - 1, 0), 0))]
        args += [u]
    in_specs += [pl.BlockSpec((1, HALO, c3), lambda bi, i: (bi, 0, 0)),
                 pl.BlockSpec((None, 1, N_MEM * KV_SLOTS, HEAD_DIM), lambda bi, i: (layer, bi, 0, 0)),
                 pl.BlockSpec((CONV_WIDTH, c3), lambda bi, i: (0, 0)),
                 pl.BlockSpec((1, c3), lambda bi, i: (0, 0)),
                 pl.BlockSpec((1, c3), lambda bi, i: (0, 0)),
                 pl.BlockSpec((1, c3), lambda bi, i: (0, 0))]
    args += [cprev, mem, w_dw, b_dw.reshape(1, c3), cn_g.reshape(1, c3), cn_b.reshape(1, c3)]

    return pl.pallas_call(
        functools.partial(_a_mix_kernel, tt=tt, tr=tr, has_halo=has_halo),
        grid=(b, nt),
        in_specs=in_specs,
        out_specs=[pl.BlockSpec((1, tt, D_MODEL), lambda bi, i: (bi, i, 0)),
                   pl.BlockSpec((1, CONV_WIDTH - 1, c3), lambda bi, i: (bi, 0, 0))],
        out_shape=[jax.ShapeDtypeStruct((b, t, D_MODEL), BF16),
                   jax.ShapeDtypeStruct((b, CONV_WIDTH - 1, c3), F32)],
        scratch_shapes=[pltpu.VMEM((c3 // HEAD_DIM, tt + HALO, HEAD_DIM), F32),
                        pltpu.VMEM((CONV_WIDTH, 8, c3), F32),
                        pltpu.VMEM((tt, c3), F32)],
        compiler_params=_params("arbitrary", "arbitrary"),
        name="a_mix",
    )(*args)


def _rows(ref, start, size, stride):
    if stride == 1:
        return ref[0, start:start + size, :]
    return ref[0, pl.ds(start, size, stride=stride), :]


def _b_attn_kernel(q0, q1, q2, k0, k1, k2, v0, v1, v2, hk0, hk1, hk2, hv0, hv1, hv2, bias_ref, gate_ref,
                   o_ref, num_scr, m_scr, s_scr):
    first = jnp.minimum(pl.program_id(2), 1)
    qs, ks, vs = (q0, q1, q2), (k0, k1, k2), (v0, v1, v2)
    hks, hvs = (hk0, hk1, hk2), (hv0, hv1, hv2)
    dn = (((1,), (1,)), ((), ()))
    for g, (_, d) in enumerate(B_GROUPS):
        for r in range(d):
            for s in range(ATTN_TOKENS // (BAND * d)):
                start = r + d * BAND * s
                q = _rows(qs[g], start, BAND, d)
                if s == 0:
                    k = jnp.concatenate([_rows(hks[g], r, BAND, d), _rows(ks[g], r, BAND, d)], 0)
                    v = jnp.concatenate([_rows(hvs[g], r, BAND, d), _rows(vs[g], r, BAND, d)], 0)
                    bias = bias_ref[g, first, 0]
                else:
                    k = _rows(ks[g], start - d * BAND, 2 * BAND, d)
                    v = _rows(vs[g], start - d * BAND, 2 * BAND, d)
                    bias = bias_ref[g, 1, 0]
                logits = jax.lax.dot_general(q.astype(BF16), k.astype(BF16), dn, preferred_element_type=F32)
                logits = logits * SCALE + bias
                m = jnp.max(logits, -1, keepdims=True)
                e = jnp.exp(logits - m)
                ssum = jnp.sum(e, -1, keepdims=True)
                num = jnp.dot(e.astype(BF16), v.astype(BF16), preferred_element_type=F32)
                if d == 1:
                    dst = pl.ds(start, BAND)
                else:
                    dst = pl.ds(start, BAND, stride=d)
                num_scr[g, dst, :] = num
                m_scr[g, dst, :] = jnp.broadcast_to(m, (BAND, HEAD_DIM))
                s_scr[g, dst, :] = jnp.broadcast_to(ssum, (BAND, HEAD_DIM))

    def merge(c, carry):
        sl = pl.ds(pl.multiple_of(c * BAND, BAND), BAND)
        ms = [m_scr[g, sl, :] for g in range(N_GROUPS)]
        m_all = jnp.maximum(jnp.maximum(ms[0], ms[1]), ms[2])
        coef = [jnp.exp(m - m_all) for m in ms]
        num = coef[0] * num_scr[0, sl, :]
        den = coef[0] * s_scr[0, sl, :]
        for g in range(1, N_GROUPS):
            num = num + coef[g] * num_scr[g, sl, :]
            den = den + coef[g] * s_scr[g, sl, :]
        o_ref[0, sl, :] = (num / den * _silu(gate_ref[0, sl, :])).astype(BF16)
        return carry

    jax.lax.fori_loop(0, ATTN_TOKENS // BAND, merge, 0)


def _b_attn(h, kv, band_bias):
    b, t, _ = h.shape
    tb = ATTN_TOKENS
    blk = (1, tb, HEAD_DIM)

    def cur(col0):
        return lambda bi, hi, i: (bi, i, col0 + hi)

    def halo_spec(d, col0):
        rows = BAND * d
        per = tb // rows
        return pl.BlockSpec((1, rows, HEAD_DIM), lambda bi, hi, i: (bi, jnp.maximum(i * per - 1, 0), col0 + hi))

    hpg = HEADS_PER_GROUP
    in_specs = ([pl.BlockSpec(blk, cur(hpg * g)) for g in range(N_GROUPS)]
                + [pl.BlockSpec(blk, cur(hpg * g)) for g in range(N_GROUPS)]
                + [pl.BlockSpec(blk, cur(B_HEADS + hpg * g)) for g in range(N_GROUPS)]
                + [halo_spec(d, hpg * g) for g, (_, d) in enumerate(B_GROUPS)]
                + [halo_spec(d, B_HEADS + hpg * g) for g, (_, d) in enumerate(B_GROUPS)]
                + [pl.BlockSpec((N_GROUPS, 2, 1, BAND, 2 * BAND), lambda bi, hi, i: (0, 0, hi, 0, 0)),
                   pl.BlockSpec(blk, cur(B_HEADS))])
    return pl.pallas_call(
        _b_attn_kernel,
        grid=(b, hpg, t // tb),
        in_specs=in_specs,
        out_specs=pl.BlockSpec(blk, lambda bi, hi, i: (bi, i, hi)),
        out_shape=jax.ShapeDtypeStruct((b, t, B_OUT), BF16),
        scratch_shapes=[pltpu.VMEM((N_GROUPS, tb, HEAD_DIM), F32)] * 3,
        compiler_params=_params("parallel", "parallel", "arbitrary"),
        name="b_attn",
    )(h, h, h, kv, kv, kv, kv, kv, kv, kv, kv, kv, kv, kv, kv, band_bias, h)


def _b_out_kernel(o_ref, qm_ref, gm_ref, mem_ref, w_ref, x_ref, g_ref, b_ref, y_ref, yb_ref, wb_ref):
    @pl.when((pl.program_id(0) == 0) & (pl.program_id(1) == 0))
    def _():
        wb_ref[...] = w_ref[...].astype(BF16)

    mbranch = _mem_attn(qm_ref[0].astype(BF16), mem_ref) * _silu(gm_ref[0])
    y = jnp.dot(o_ref[0], wb_ref[0:B_OUT, :], preferred_element_type=F32)
    y = y + jnp.dot(mbranch.astype(BF16), wb_ref[B_OUT:B_OUT + MEM_WIDTH, :], preferred_element_type=F32)
    out = _layernorm(ALPHA * x_ref[0] + y, g_ref[...], b_ref[...])
    y_ref[0] = out
    yb_ref[0] = out.astype(BF16)


def _b_out(o, h, mem, layer, w, wl, x, g, b):
    bsz, t, d = x.shape
    tt = min(t, 512)
    kk = w.shape[1]
    qcol = (MIX_WIDTH + B_OUT) // MEM_WIDTH
    row = pl.BlockSpec((1, tt, d), lambda bi, i: (bi, i, 0))
    return pl.pallas_call(
        _b_out_kernel,
        grid=(bsz, t // tt),
        in_specs=[pl.BlockSpec((1, tt, B_OUT), lambda bi, i: (bi, i, 0)),
                  pl.BlockSpec((1, tt, MEM_WIDTH), lambda bi, i: (bi, i, qcol)),
                  pl.BlockSpec((1, tt, MEM_WIDTH), lambda bi, i: (bi, i, qcol + 1)),
                  pl.BlockSpec((None, 1, N_MEM * KV_SLOTS, HEAD_DIM), lambda bi, i: (layer, bi, 0, 0)),
                  pl.BlockSpec((None, kk, d), lambda bi, i: (wl, 0, 0), pipeline_mode=pl.Buffered(1)),
                  row,
                  pl.BlockSpec((1, d), lambda bi, i: (0, 0)),
                  pl.BlockSpec((1, d), lambda bi, i: (0, 0))],
        out_specs=[row, row],
        out_shape=[jax.ShapeDtypeStruct((bsz, t, d), F32), jax.ShapeDtypeStruct((bsz, t, d), BF16)],
        scratch_shapes=[pltpu.VMEM((kk, d), BF16)],
        compiler_params=_params("arbitrary", "arbitrary"),
        name="b_out",
    )(o, h, h, mem, w, x, g.reshape(1, d), b.reshape(1, d))


def _samp_attn_kernel(q_ref, st_ref, kn_ref, vn_ref, bs_ref, bn_ref, num_ref, m_ref, s_ref, *, nq, w):
    dn = (((1,), (1,)), ((), ()))
    for h in range(HEADS_PER_GROUP):
        lo = h * HEAD_DIM
        q = q_ref[0, :, lo:lo + HEAD_DIM].astype(BF16)
        ks = _kv_rows(st_ref, (0,), h, w).astype(BF16)
        vs = _kv_rows(st_ref, (0,), HEADS_PER_GROUP + h, w).astype(BF16)
        kn = kn_ref[0, :, lo:lo + HEAD_DIM].astype(BF16)
        vn = vn_ref[0, :, lo:lo + HEAD_DIM].astype(BF16)
        ls = jax.lax.dot_general(q, ks, dn, preferred_element_type=F32) * SCALE + bs_ref[h]
        ln = jax.lax.dot_general(q, kn, dn, preferred_element_type=F32) * SCALE + bn_ref[h]
        m = jnp.maximum(jnp.max(ls, -1, keepdims=True), jnp.max(ln, -1, keepdims=True))
        es = jnp.exp(ls - m)
        en = jnp.exp(ln - m)
        ssum = jnp.sum(es, -1, keepdims=True) + jnp.sum(en, -1, keepdims=True)
        num_ref[0, :, lo:lo + HEAD_DIM] = (jnp.dot(es.astype(BF16), vs, preferred_element_type=F32)
                                           + jnp.dot(en.astype(BF16), vn, preferred_element_type=F32))
        m_ref[0, :, lo:lo + HEAD_DIM] = jnp.broadcast_to(m, (nq, HEAD_DIM))
        s_ref[0, :, lo:lo + HEAD_DIM] = jnp.broadcast_to(ssum, (nq, HEAD_DIM))


def _samp_attn(h, qcol, state, kv, kcol, vcol, bias_state, bias_new):
    b, nq, _ = h.shape
    w = state.shape[1] // KV_SLOTS
    blk = (1, nq, GROUP_WIDTH)
    out = jax.ShapeDtypeStruct((b, nq, GROUP_WIDTH), F32)
    return pl.pallas_call(
        functools.partial(_samp_attn_kernel, nq=nq, w=w),
        grid=(b,),
        in_specs=[pl.BlockSpec(blk, lambda i: (i, 0, qcol)),
                  pl.BlockSpec((1, w * KV_SLOTS, HEAD_DIM), lambda i: (i, 0, 0)),
                  pl.BlockSpec(blk, lambda i: (i, 0, kcol)),
                  pl.BlockSpec(blk, lambda i: (i, 0, vcol)),
                  pl.BlockSpec((HEADS_PER_GROUP, nq, w), lambda i: (0, 0, 0)),
                  pl.BlockSpec((HEADS_PER_GROUP, nq, nq), lambda i: (0, 0, 0))],
        out_specs=[pl.BlockSpec(blk, lambda i: (i, 0, 0))] * 3,
        out_shape=[out] * 3,
        compiler_params=_params("parallel"),
        name="samp_attn",
    )(h, state, kv, kv, bias_state, bias_new)


def _samp_merge_kernel(n0, m0, s0, n1, m1, s1, n2, m2, s2, gate, o_out):
    ms = [m0[0], m1[0], m2[0]]
    m_all = jnp.maximum(jnp.maximum(ms[0], ms[1]), ms[2])
    coef = [jnp.exp(m - m_all) for m in ms]
    num = coef[0] * n0[0] + coef[1] * n1[0] + coef[2] * n2[0]
    den = coef[0] * s0[0] + coef[1] * s1[0] + coef[2] * s2[0]
    o_out[0] = (num / den * _silu(gate[0])).astype(BF16)


def _samp_merge(parts, h):
    b, nq, _ = h.shape
    spec = pl.BlockSpec((1, nq, GROUP_WIDTH), lambda i: (i, 0, 0))
    flat = [a for p in parts for a in p]
    return pl.pallas_call(
        _samp_merge_kernel,
        grid=(b,),
        in_specs=[spec] * 9 + [pl.BlockSpec((1, nq, GROUP_WIDTH), lambda i: (i, 0, MIX_WIDTH // GROUP_WIDTH))],
        out_specs=spec,
        out_shape=jax.ShapeDtypeStruct((b, nq, GROUP_WIDTH), BF16),
        compiler_params=_params("parallel"),
        name="samp_merge",
    )(*flat, h)


def _shift_state_kernel(main_ref, next_ref, new_ref, o_ref, *, rows, shift):
    last = pl.program_id(1) == pl.num_programs(1) - 1
    if rows > shift:
        o_ref[0, 0:rows - shift, :] = main_ref[0, shift:rows, :]
    o_ref[0, rows - shift:rows, :] = jnp.where(last, new_ref[0], next_ref[0])


def _shift_state(state, new):
    b, total, lanes = state.shape
    shift = new.shape[1]
    rows = min(total, 4096)
    nblk = total // rows
    per = rows // shift
    return pl.pallas_call(
        functools.partial(_shift_state_kernel, rows=rows, shift=shift),
        grid=(b, nblk),
        in_specs=[pl.BlockSpec((1, rows, lanes), lambda bi, i: (bi, i, 0)),
                  pl.BlockSpec((1, shift, lanes), lambda bi, i: (bi, jnp.minimum((i + 1) * per, nblk * per - 1), 0)),
                  pl.BlockSpec((1, shift, lanes), lambda bi, i: (bi, 0, 0))],
        out_specs=pl.BlockSpec((1, rows, lanes), lambda bi, i: (bi, i, 0)),
        out_shape=jax.ShapeDtypeStruct(state.shape, state.dtype),
        compiler_params=_params("parallel", "arbitrary"),
        name="shift_state",
    )(state, state, new)


def _t5_bucket(dist):
    max_exact = REL_BUCKETS // 2
    safe = jnp.maximum(dist, 1).astype(F32)
    large = max_exact + (jnp.log(safe / max_exact) / math.log(REL_MAX_DIST / max_exact)
                         * (REL_BUCKETS - max_exact)).astype(jnp.int32)
    large = jnp.minimum(large, REL_BUCKETS - 1)
    return jnp.where(dist < max_exact, dist, large)


def _group_bias(rel_bias, g):
    w, d = B_GROUPS[g]
    dist = d * jnp.arange(w // d + 1, dtype=jnp.int32)
    tab = rel_bias[_t5_bucket(dist)]
    return tab[:, g * HEADS_PER_GROUP:(g + 1) * HEADS_PER_GROUP].T.astype(F32)


def _toeplitz(vec, rows, cols):
    hh, p = vec.shape
    flat = jnp.tile(vec, (1, rows))[:, :rows * (p - 1)]
    return flat.reshape(hh, rows, p - 1)[:, :, :cols]


def _band_bias(bias):
    period = 3 * BAND
    vec = jnp.full((bias.shape[0], period), NEG_INF, F32)
    vec = jax.lax.dynamic_update_slice(vec, bias[:, ::-1], (0, 0))
    mat = _toeplitz(vec, BAND, 2 * BAND)
    c = jnp.arange(2 * BAND, dtype=jnp.int32)[None, None, :]
    first = jnp.where(c >= BAND, mat, NEG_INF)
    return jnp.stack([first, mat])


def _sample_bias(bias, w, d, nq):
    hh = bias.shape[0]
    neg = jnp.full_like(bias, NEG_INF)
    by_dist = jnp.stack([bias] + [neg] * (d - 1), -1).reshape(hh, -1)
    by_dist = jnp.concatenate([by_dist[:, :w + 1], jnp.full((hh, nq), NEG_INF, F32)], 1)
    rev = by_dist[:, ::-1]
    bs = jnp.stack([rev[:, nq - n:nq - n + w] for n in range(nq)], 1)
    small = jnp.concatenate([jnp.full((hh, nq), NEG_INF, F32), by_dist[:, :nq]], 1)[:, ::-1]
    bn = jnp.stack([small[:, nq - 1 - n:2 * nq - 1 - n] for n in range(nq)], 1)
    return bs, bn


def _trunk(x, conv_prev, mem_kv, kv_state, a_w_in, a_w_dw, a_b_dw, a_cn_g, a_cn_b, a_w_out,
           b_w_in, b_w_out, w_kv_shared, rel_bias, ln_g, ln_b):
    b, t, d_model = x.shape
    m = b * t
    x2 = x.reshape(m, d_model)
    xb = x2
    new_conv = []
    for l in range(N_A_LAYERS):
        u, gates = _a_proj(xb, a_w_in, l)
        cm, nc = _a_mix(u.reshape(b, t, -1), gates.reshape(b, t, -1), conv_prev[l], mem_kv, l,
                        a_w_dw[l], a_b_dw[l], a_cn_g[l], a_cn_b[l])
        new_conv.append(nc)
        x2, xb = _out_ln(cm.reshape(m, d_model), a_w_out, l, x2, ln_g[l], ln_b[l])

    kv = _mm(xb, w_kv_shared[None], 0).reshape(b, t, 2 * MIX_WIDTH)
    vcol0 = MIX_WIDTH // GROUP_WIDTH
    biases = [_group_bias(rel_bias, g) for g in range(N_GROUPS)]
    kv_new = [_kv_window(kv, g, min(w, t)) for g, (w, _) in enumerate(B_GROUPS)]
    if kv_state is None:
        new_bufs = kv_new
        band_bias = jnp.stack([_band_bias(bi) for bi in biases])
    else:
        new_bufs = [_shift_state(st, kn) for st, kn in zip(kv_state, kv_new)]
        samp_bias = [_sample_bias(bi, w, d, t) for bi, (w, d) in zip(biases, B_GROUPS)]
    new_bufs = [nb.reshape(b, -1, 2, HEADS_PER_GROUP, HEAD_DIM) for nb in new_bufs]

    x3 = x2.reshape(b, t, d_model)
    for i in range(DEPTH - N_A_LAYERS):
        l = N_A_LAYERS + i
        h = _mm(xb.reshape(m, d_model), b_w_in, i).reshape(b, t, -1)
        if kv_state is None:
            o = _b_attn(h, kv, band_bias)
        else:
            parts = [_samp_attn(h, g, kv_state[g], kv, g, vcol0 + g, *samp_bias[g]) for g in range(N_GROUPS)]
            o = _samp_merge(parts, h)
        x3, xb = _b_out(o, h, mem_kv, l, b_w_out, i, x3, ln_g[l], ln_b[l])
    return x3, jnp.stack(new_conv), new_bufs


def kernel(x_prompt, x_sample, state_conv, state_kv_g0, state_kv_g1, state_kv_g2, cache_mem_kv, mem_prompt,
           a_w_in, a_w_dw, a_b_dw, a_cn_g, a_cn_b, a_w_out, b_w_in, b_w_out, w_kv_shared, w_mem_kv,
           rel_bias, ln_g, ln_b):
    bp = x_prompt.shape[0]
    bs = x_sample.shape[0]
    mem_p = _mem_proj(mem_prompt.reshape(bp * N_MEM, D_MODEL), w_mem_kv).reshape(
        DEPTH, bp, N_MEM * KV_SLOTS, HEAD_DIM)
    new_mem_kv = mem_p.reshape(DEPTH, bp, N_MEM, 2, MEM_HEADS, HEAD_DIM)
    conv_zero = jnp.zeros((N_A_LAYERS, bp, CONV_WIDTH - 1, CONV_CH), x_prompt.dtype)
    weights = (a_w_in, a_w_dw, a_b_dw, a_cn_g, a_cn_b, a_w_out, b_w_in, b_w_out, w_kv_shared, rel_bias, ln_g, ln_b)

    y_p, conv_p, bufs_p = _trunk(x_prompt, conv_zero, mem_p, None, *weights)
    kv_state = [s.reshape(bs, s.shape[1] * KV_SLOTS, HEAD_DIM) for s in (state_kv_g0, state_kv_g1, state_kv_g2)]
    mem_s = cache_mem_kv.reshape(DEPTH, bs, N_MEM * KV_SLOTS, HEAD_DIM)
    y_s, conv_s, bufs_s = _trunk(x_sample, state_conv, mem_s, kv_state, *weights)

    return (y_p, y_s, conv_p, conv_s, bufs_p[0], bufs_s[0], bufs_p[1], bufs_s[1], bufs_p[2], bufs_s[2], new_mem_kv)
```

```python
import functools
import math

import jax
import jax.numpy as jnp
from jax.experimental import pallas as pl
from jax.experimental.pallas import tpu as pltpu

D_MODEL = 2048
DEPTH = 4
HEAD_DIM = 128
MEM_HEADS = 4
MEM_WIDTH = MEM_HEADS * HEAD_DIM
N_MEM = 256
MIX_WIDTH = D_MODEL - MEM_WIDTH
N_A_LAYERS = DEPTH // 2
CONV_CH = MIX_WIDTH
CONV_WIDTH = 31
B_GROUPS = ((128, 1), (512, 4), (2048, 16))
N_GROUPS = len(B_GROUPS)
HEADS_PER_GROUP = 4
B_HEADS = N_GROUPS * HEADS_PER_GROUP
GROUP_WIDTH = HEADS_PER_GROUP * HEAD_DIM
KV_SLOTS = 2 * HEADS_PER_GROUP
B_OUT = GROUP_WIDTH
REL_BUCKETS = 32
REL_MAX_DIST = 2048
BAND = 128
ATTN_TOKENS = BAND * max(d for _, d in B_GROUPS)
LN_EPS = 1e-5
ALPHA = (2 * DEPTH) ** 0.25
NEG_INF = -1e30
SCALE = HEAD_DIM ** -0.5

HALO = 32
VMEM_LIMIT = 56 * 1024 * 1024
BF16 = jnp.bfloat16
F32 = jnp.float32


def _params(*sem):
    return pltpu.CompilerParams(dimension_semantics=sem, vmem_limit_bytes=VMEM_LIMIT)


def _sigmoid(x):
    return 1.0 / (1.0 + jnp.exp(-x))


def _silu(x):
    return x * _sigmoid(x)


def _layernorm(z, g, b):
    mu = jnp.mean(z, -1, keepdims=True)
    zc = z - mu
    var = jnp.mean(zc * zc, -1, keepdims=True)
    return zc * jax.lax.rsqrt(var + LN_EPS) * g + b


def _lhs_bf16(x_ref, scratch, col_axis):
    if not scratch:
        return x_ref[...]
    xb_ref, = scratch

    @pl.when(pl.program_id(col_axis) == 0)
    def _():
        xb_ref[...] = x_ref[...].astype(BF16)

    return xb_ref[...]


def _row_tile(x):
    m, k = x.shape
    if x.dtype == BF16:
        return min(m, 2048), []
    return min(m, 1024), [pltpu.VMEM((min(m, 1024), k), BF16)]


def _mm_kernel(x_ref, w_ref, o_ref, *scratch):
    xb = _lhs_bf16(x_ref, scratch, 2)
    o_ref[...] = jnp.dot(xb, w_ref[...].astype(BF16), preferred_element_type=F32)


def _mm(x, w, layer=None, *, tn=512):
    m, k = x.shape
    n = w.shape[2]
    tm, scratch = _row_tile(x)
    l0, nl = (0, w.shape[0]) if layer is None else (layer, 1)
    out = pl.pallas_call(
        _mm_kernel,
        grid=(nl, m // tm, n // tn),
        in_specs=[pl.BlockSpec((tm, k), lambda l, i, j: (i, 0)),
                  pl.BlockSpec((None, k, tn), lambda l, i, j: (l0 + l, 0, j))],
        out_specs=pl.BlockSpec((None, tm, tn), lambda l, i, j: (l, i, j)),
        out_shape=jax.ShapeDtypeStruct((nl, m, n), F32),
        scratch_shapes=scratch,
        compiler_params=_params("arbitrary", "arbitrary", "arbitrary"),
        name="mm",
    )(x, w)
    return out if layer is None else out.reshape(m, n)


def _store_kv_slots(o_ref, lead, slot0, vals):
    rows = vals.shape[0]
    for j in range(vals.shape[1] // HEAD_DIM):
        o_ref[lead + (pl.ds(slot0 + j, rows, stride=KV_SLOTS), slice(None))] = vals[:, j * HEAD_DIM:(j + 1) * HEAD_DIM]


def _mem_proj_kernel(x_ref, w_ref, o_ref, xb_ref):
    @pl.when(pl.program_id(0) == 0)
    def _():
        xb_ref[...] = x_ref[...].astype(BF16)

    _store_kv_slots(o_ref, (), 0, jnp.dot(xb_ref[...], w_ref[...].astype(BF16), preferred_element_type=F32))


def _mem_proj(x, w):
    m, k = x.shape
    nl, _, n = w.shape
    return pl.pallas_call(
        _mem_proj_kernel,
        grid=(nl,),
        in_specs=[pl.BlockSpec((m, k), lambda l: (0, 0)),
                  pl.BlockSpec((None, k, n), lambda l: (l, 0, 0))],
        out_specs=pl.BlockSpec((None, m * KV_SLOTS, HEAD_DIM), lambda l: (l, 0, 0)),
        out_shape=jax.ShapeDtypeStruct((nl, m * KV_SLOTS, HEAD_DIM), F32),
        scratch_shapes=[pltpu.VMEM((m, k), BF16)],
        compiler_params=_params("arbitrary"),
        name="mem_proj",
    )(x, w)


def _kv_window_kernel(k_ref, v_ref, o_ref):
    _store_kv_slots(o_ref, (0,), 0, k_ref[0])
    _store_kv_slots(o_ref, (0,), HEADS_PER_GROUP, v_ref[0])


def _kv_window(kv, g, w):
    b, t, _ = kv.shape
    rows = min(w, 512)
    first = (t - w) // rows
    return pl.pallas_call(
        _kv_window_kernel,
        grid=(b, w // rows),
        in_specs=[pl.BlockSpec((1, rows, GROUP_WIDTH), lambda bi, i: (bi, first + i, g)),
                  pl.BlockSpec((1, rows, GROUP_WIDTH), lambda bi, i: (bi, first + i, N_GROUPS + g))],
        out_specs=pl.BlockSpec((1, rows * KV_SLOTS, HEAD_DIM), lambda bi, i: (bi, i, 0)),
        out_shape=jax.ShapeDtypeStruct((b, w * KV_SLOTS, HEAD_DIM), F32),
        compiler_params=_params("arbitrary", "arbitrary"),
        name="kv_window",
    )(kv, kv)


A_TN = 512
N_GLU = CONV_CH // A_TN
QM_STEP = 3 * N_GLU


def _a_proj_kernel(x_ref, w_ref, u_ref, g_ref, ga_ref, *scratch):
    j = pl.program_id(1)
    xb = _lhs_bf16(x_ref, scratch, 1)

    def proj():
        return jnp.dot(xb, w_ref[...].astype(BF16), preferred_element_type=F32)

    is_glu = j < 2 * N_GLU

    @pl.when(is_glu & (j % 2 == 0))
    def _():
        ga_ref[...] = proj()

    @pl.when(is_glu & (j % 2 == 1))
    def _():
        u_ref[...] = ga_ref[...] * _sigmoid(proj())

    @pl.when(j == QM_STEP)
    def _():
        g_ref[...] = proj().astype(BF16)

    @pl.when(jnp.logical_not(is_glu) & (j != QM_STEP))
    def _():
        g_ref[...] = _silu(proj()).astype(BF16)


def _a_proj(x, w, layer):
    m, k = x.shape
    tm, scratch = _row_tile(x)
    n_steps = 3 * N_GLU + 2 * MEM_WIDTH // A_TN

    def w_col(j):
        return jnp.where(j < 2 * N_GLU, (j % 2) * N_GLU + j // 2, j)

    return pl.pallas_call(
        _a_proj_kernel,
        grid=(m // tm, n_steps),
        in_specs=[pl.BlockSpec((tm, k), lambda i, j: (i, 0)),
                  pl.BlockSpec((None, k, A_TN), lambda i, j: (layer, 0, w_col(j)))],
        out_specs=[pl.BlockSpec((tm, A_TN), lambda i, j: (i, jnp.minimum(j // 2, N_GLU - 1))),
                   pl.BlockSpec((tm, A_TN), lambda i, j: (i, jnp.maximum(j - 2 * N_GLU, 0)))],
        out_shape=[jax.ShapeDtypeStruct((m, CONV_CH), F32),
                   jax.ShapeDtypeStruct((m, CONV_CH + 2 * MEM_WIDTH), BF16)],
        scratch_shapes=[pltpu.VMEM((tm, A_TN), F32)] + scratch,
        compiler_params=_params("arbitrary", "arbitrary"),
        name="a_proj",
    )(x, w)


def _out_ln_kernel(cm_ref, w_ref, x_ref, g_ref, b_ref, o_ref, ob_ref, wb_ref):
    @pl.when(pl.program_id(0) == 0)
    def _():
        wb_ref[...] = w_ref[...].astype(BF16)

    y = jnp.dot(cm_ref[...], wb_ref[...], preferred_element_type=F32)
    out = _layernorm(ALPHA * x_ref[...] + y, g_ref[...], b_ref[...])
    o_ref[...] = out
    ob_ref[...] = out.astype(BF16)


def _out_ln(cm, w, layer, x, g, b):
    m, kk = cm.shape
    d = w.shape[2]
    tm = min(m, 512)
    row = pl.BlockSpec((tm, d), lambda i: (i, 0))
    return pl.pallas_call(
        _out_ln_kernel,
        grid=(m // tm,),
        in_specs=[pl.BlockSpec((tm, kk), lambda i: (i, 0)),
                  pl.BlockSpec((None, kk, d), lambda i: (layer, 0, 0), pipeline_mode=pl.Buffered(1)),
                  row,
                  pl.BlockSpec((1, d), lambda i: (0, 0)),
                  pl.BlockSpec((1, d), lambda i: (0, 0))],
        out_specs=[row, row],
        out_shape=[jax.ShapeDtypeStruct((m, d), F32), jax.ShapeDtypeStruct((m, d), BF16)],
        scratch_shapes=[pltpu.VMEM((kk, d), BF16)],
        compiler_params=_params("arbitrary"),
        name="out_ln",
    )(cm, w, x, g.reshape(1, d), b.reshape(1, d))


def _kv_rows(ref, lead, slot, n):
    return ref[lead + (pl.ds(slot, n, stride=KV_SLOTS), slice(None))]


def _mem_attn(qm, mem_ref):
    outs = []
    for h in range(MEM_HEADS):
        lo = h * HEAD_DIM
        q = qm[:, lo:lo + HEAD_DIM]
        k = _kv_rows(mem_ref, (0,), h, N_MEM).astype(BF16)
        v = _kv_rows(mem_ref, (0,), MEM_HEADS + h, N_MEM).astype(BF16)
        s = jax.lax.dot_general(q, k, (((1,), (1,)), ((), ())), preferred_element_type=F32) * SCALE
        s = s - jnp.max(s, -1, keepdims=True)
        e = jnp.exp(s)
        p = e / jnp.sum(e, -1, keepdims=True)
        outs.append(jnp.dot(p.astype(BF16), v, preferred_element_type=F32))
    return jnp.concatenate(outs, -1)


def _a_mix_kernel(*refs, tt, tr, has_halo):
    if has_halo:
        (u_ref, sgate, qm, sgm, u_prev, cprev, mem, wdw, bdw, cng, cnb,
         cm_out, nc_out, ext, wrep, conv) = refs
    else:
        (u_ref, sgate, qm, sgm, cprev, mem, wdw, bdw, cng, cnb,
         cm_out, nc_out, ext, wrep, conv) = refs
    i = pl.program_id(1)
    n_blk = CONV_CH // HEAD_DIM

    @pl.when((pl.program_id(0) == 0) & (i == 0))
    def _():
        for k in range(CONV_WIDTH):
            wrep[k] = jnp.broadcast_to(wdw[k:k + 1, :], (8, CONV_CH))

    for c in range(n_blk):
        lanes = slice(c * HEAD_DIM, (c + 1) * HEAD_DIM)
        if has_halo:
            ext[c, 0:HALO, :] = jnp.where(i == 0, cprev[0, :, lanes], u_prev[0, :, lanes])
        else:
            ext[c, 0:HALO, :] = cprev[0, :, lanes]
        ext[c, HALO:HALO + tt, :] = u_ref[0, :, lanes]
        nc_out[0, :, lanes] = ext[c, HALO + tt - (CONV_WIDTH - 1):HALO + tt, :]

    off0 = HALO - (CONV_WIDTH - 1)
    for c in range(n_blk):
        lanes = slice(c * HEAD_DIM, (c + 1) * HEAD_DIM)

        def body(rc, carry, c=c, lanes=lanes):
            r0 = pl.multiple_of(rc * tr, 8)
            acc = jnp.zeros((tr // 8, 8, HEAD_DIM), F32)
            for k in range(CONV_WIDTH):
                win = ext[c, pl.ds(r0 + k + off0, tr), :].reshape(tr // 8, 8, HEAD_DIM)
                acc = acc + win * wrep[k, :, lanes][None]
            conv[pl.ds(r0, tr), lanes] = acc.reshape(tr, HEAD_DIM) + bdw[:, lanes]
            return carry

        jax.lax.fori_loop(0, tt // tr, body, 0)

    zn = _layernorm(conv[...], cng[...], cnb[...])
    cm_out[0, :, 0:CONV_CH] = (_silu(zn) * sgate[0].astype(F32)).astype(BF16)
    cm_out[0, :, CONV_CH:D_MODEL] = (_mem_attn(qm[0], mem) * sgm[0].astype(F32)).astype(BF16)


def _a_mix(u, gates, conv_prev, mem, layer, w_dw, b_dw, cn_g, cn_b):
    b, t, _ = u.shape
    tt = min(t, 256)
    tr = min(tt, 128)
    nt = t // tt
    has_halo = nt > 1
    cprev = jnp.pad(conv_prev, ((0, 0), (HALO - (CONV_WIDTH - 1), 0), (0, 0)))
    c3 = CONV_CH
    qcol = CONV_CH // MEM_WIDTH

    in_specs = [pl.BlockSpec((1, tt, c3), lambda bi, i: (bi, i, 0)),
                pl.BlockSpec((1, tt, c3), lambda bi, i: (bi, i, 0)),
                pl.BlockSpec((1, tt, MEM_WIDTH), lambda bi, i: (bi, i, qcol)),
                pl.BlockSpec((1, tt, MEM_WIDTH), lambda bi, i: (bi, i, qcol + 1))]
    args = [u, gates, gates, gates]
    if has_halo:
        per = tt // HALO
        in_specs += [pl.BlockSpec((1, HALO, c3), lambda bi, i: (bi, jnp.maximum(i * per - 1, 0), 0))]
        args += [u]
    in_specs += [pl.BlockSpec((1, HALO, c3), lambda bi, i: (bi, 0, 0)),
                 pl.BlockSpec((None, 1, N_MEM * KV_SLOTS, HEAD_DIM), lambda bi, i: (layer, bi, 0, 0)),
                 pl.BlockSpec((CONV_WIDTH, c3), lambda bi, i: (0, 0)),
                 pl.BlockSpec((1, c3), lambda bi, i: (0, 0)),
                 pl.BlockSpec((1, c3), lambda bi, i: (0, 0)),
                 pl.BlockSpec((1, c3), lambda bi, i: (0, 0))]
    args += [cprev, mem, w_dw, b_dw.reshape(1, c3), cn_g.reshape(1, c3), cn_b.reshape(1, c3)]

    return pl.pallas_call(
        functools.partial(_a_mix_kernel, tt=tt, tr=tr, has_halo=has_halo),
        grid=(b, nt),
        in_specs=in_specs,
        out_specs=[pl.BlockSpec((1, tt, D_MODEL), lambda bi, i: (bi, i, 0)),
                   pl.BlockSpec((1, CONV_WIDTH - 1, c3), lambda bi, i: (bi, 0, 0))],
        out_shape=[jax.ShapeDtypeStruct((b, t, D_MODEL), BF16),
                   jax.ShapeDtypeStruct((b, CONV_WIDTH - 1, c3), F32)],
        scratch_shapes=[pltpu.VMEM((c3 // HEAD_DIM, tt + HALO, HEAD_DIM), F32),
                        pltpu.VMEM((CONV_WIDTH, 8, c3), F32),
                        pltpu.VMEM((tt, c3), F32)],
        compiler_params=_params("arbitrary", "arbitrary"),
        name="a_mix",
    )(*args)


def _rows(ref, start, size, stride):
    if stride == 1:
        return ref[0, start:start + size, :]
    return ref[0, pl.ds(start, size, stride=stride), :]


def _b_attn_kernel(q0, q1, q2, k0, k1, k2, v0, v1, v2, hk0, hk1, hk2, hv0, hv1, hv2, bias_ref, gate_ref,
                   o_ref, num_scr, m_scr, s_scr):
    first = jnp.minimum(pl.program_id(2), 1)
    qs, ks, vs = (q0, q1, q2), (k0, k1, k2), (v0, v1, v2)
    hks, hvs = (hk0, hk1, hk2), (hv0, hv1, hv2)
    dn = (((1,), (1,)), ((), ()))
    for g, (_, d) in enumerate(B_GROUPS):
        for r in range(d):
            for s in range(ATTN_TOKENS // (BAND * d)):
                start = r + d * BAND * s
                q = _rows(qs[g], start, BAND, d)
                if s == 0:
                    k = jnp.concatenate([_rows(hks[g], r, BAND, d), _rows(ks[g], r, BAND, d)], 0)
                    v = jnp.concatenate([_rows(hvs[g], r, BAND, d), _rows(vs[g], r, BAND, d)], 0)
                    bias = bias_ref[g, first, 0]
                else:
                    k = _rows(ks[g], start - d * BAND, 2 * BAND, d)
                    v = _rows(vs[g], start - d * BAND, 2 * BAND, d)
                    bias = bias_ref[g, 1, 0]
                logits = jax.lax.dot_general(q.astype(BF16), k.astype(BF16), dn, preferred_element_type=F32)
                logits = logits * SCALE + bias
                m = jnp.max(logits, -1, keepdims=True)
                e = jnp.exp(logits - m)
                ssum = jnp.sum(e, -1, keepdims=True)
                num = jnp.dot(e.astype(BF16), v.astype(BF16), preferred_element_type=F32)
                if d == 1:
                    dst = pl.ds(start, BAND)
                else:
                    dst = pl.ds(start, BAND, stride=d)
                num_scr[g, dst, :] = num
                m_scr[g, dst, :] = jnp.broadcast_to(m, (BAND, HEAD_DIM))
                s_scr[g, dst, :] = jnp.broadcast_to(ssum, (BAND, HEAD_DIM))

    def merge(c, carry):
        sl = pl.ds(pl.multiple_of(c * BAND, BAND), BAND)
        ms = [m_scr[g, sl, :] for g in range(N_GROUPS)]
        m_all = jnp.maximum(jnp.maximum(ms[0], ms[1]), ms[2])
        coef = [jnp.exp(m - m_all) for m in ms]
        num = coef[0] * num_scr[0, sl, :]
        den = coef[0] * s_scr[0, sl, :]
        for g in range(1, N_GROUPS):
            num = num + coef[g] * num_scr[g, sl, :]
            den = den + coef[g] * s_scr[g, sl, :]
        o_ref[0, sl, :] = (num / den * _silu(gate_ref[0, sl, :])).astype(BF16)
        return carry

    jax.lax.fori_loop(0, ATTN_TOKENS // BAND, merge, 0)


def _b_attn(h, kv, band_bias):
    b, t, _ = h.shape
    tb = ATTN_TOKENS
    blk = (1, tb, HEAD_DIM)

    def cur(col0):
        return lambda bi, hi, i: (bi, i, col0 + hi)

    def halo_spec(d, col0):
        rows = BAND * d
        per = tb // rows
        return pl.BlockSpec((1, rows, HEAD_DIM), lambda bi, hi, i: (bi, jnp.maximum(i * per - 1, 0), col0 + hi))

    hpg = HEADS_PER_GROUP
    in_specs = ([pl.BlockSpec(blk, cur(hpg * g)) for g in range(N_GROUPS)]
                + [pl.BlockSpec(blk, cur(hpg * g)) for g in range(N_GROUPS)]
                + [pl.BlockSpec(blk, cur(B_HEADS + hpg * g)) for g in range(N_GROUPS)]
                + [halo_spec(d, hpg * g) for g, (_, d) in enumerate(B_GROUPS)]
                + [halo_spec(d, B_HEADS + hpg * g) for g, (_, d) in enumerate(B_GROUPS)]
                + [pl.BlockSpec((N_GROUPS, 2, 1, BAND, 2 * BAND), lambda bi, hi, i: (0, 0, hi, 0, 0)),
                   pl.BlockSpec(blk, cur(B_HEADS))])
    return pl.pallas_call(
        _b_attn_kernel,
        grid=(b, hpg, t // tb),
        in_specs=in_specs,
        out_specs=pl.BlockSpec(blk, lambda bi, hi, i: (bi, i, hi)),
        out_shape=jax.ShapeDtypeStruct((b, t, B_OUT), BF16),
        scratch_shapes=[pltpu.VMEM((N_GROUPS, tb, HEAD_DIM), F32)] * 3,
        compiler_params=_params("parallel", "parallel", "arbitrary"),
        name="b_attn",
    )(h, h, h, kv, kv, kv, kv, kv, kv, kv, kv, kv, kv, kv, kv, band_bias, h)


def _b_out_kernel(o_ref, qm_ref, gm_ref, mem_ref, w_ref, x_ref, g_ref, b_ref, y_ref, yb_ref, wb_ref):
    @pl.when((pl.program_id(0) == 0) & (pl.program_id(1) == 0))
    def _():
        wb_ref[...] = w_ref[...].astype(BF16)

    mbranch = _mem_attn(qm_ref[0].astype(BF16), mem_ref) * _silu(gm_ref[0])
    y = jnp.dot(o_ref[0], wb_ref[0:B_OUT, :], preferred_element_type=F32)
    y = y + jnp.dot(mbranch.astype(BF16), wb_ref[B_OUT:B_OUT + MEM_WIDTH, :], preferred_element_type=F32)
    out = _layernorm(ALPHA * x_ref[0] + y, g_ref[...], b_ref[...])
    y_ref[0] = out
    yb_ref[0] = out.astype(BF16)


def _b_out(o, h, mem, layer, w, wl, x, g, b):
    bsz, t, d = x.shape
    tt = min(t, 512)
    kk = w.shape[1]
    qcol = (MIX_WIDTH + B_OUT) // MEM_WIDTH
    row = pl.BlockSpec((1, tt, d), lambda bi, i: (bi, i, 0))
    return pl.pallas_call(
        _b_out_kernel,
        grid=(bsz, t // tt),
        in_specs=[pl.BlockSpec((1, tt, B_OUT), lambda bi, i: (bi, i, 0)),
                  pl.BlockSpec((1, tt, MEM_WIDTH), lambda bi, i: (bi, i, qcol)),
                  pl.BlockSpec((1, tt, MEM_WIDTH), lambda bi, i: (bi, i, qcol + 1)),
                  pl.BlockSpec((None, 1, N_MEM * KV_SLOTS, HEAD_DIM), lambda bi, i: (layer, bi, 0, 0)),
                  pl.BlockSpec((None, kk, d), lambda bi, i: (wl, 0, 0), pipeline_mode=pl.Buffered(1)),
                  row,
                  pl.BlockSpec((1, d), lambda bi, i: (0, 0)),
                  pl.BlockSpec((1, d), lambda bi, i: (0, 0))],
        out_specs=[row, row],
        out_shape=[jax.ShapeDtypeStruct((bsz, t, d), F32), jax.ShapeDtypeStruct((bsz, t, d), BF16)],
        scratch_shapes=[pltpu.VMEM((kk, d), BF16)],
        compiler_params=_params("arbitrary", "arbitrary"),
        name="b_out",
    )(o, h, h, mem, w, x, g.reshape(1, d), b.reshape(1, d))


def _samp_attn_kernel(*refs, nq, widths, emit_state):
    ng = N_GROUPS
    q_ref, gate_ref = refs[0], refs[1]
    st_refs, new_refs = refs[2:2 + ng], refs[2 + ng:2 + 2 * ng]
    bs_refs, bn_refs = refs[2 + 2 * ng:2 + 3 * ng], refs[2 + 3 * ng:2 + 4 * ng]
    o_ref = refs[2 + 4 * ng]
    dn = (((1,), (1,)), ((), ()))
    for h in range(HEADS_PER_GROUP):
        ms, sums, nums = [], [], []
        for g in range(ng):
            lo = (g * HEADS_PER_GROUP + h) * HEAD_DIM
            q = q_ref[0, :, lo:lo + HEAD_DIM].astype(BF16)
            ks = _kv_rows(st_refs[g], (0,), h, widths[g]).astype(BF16)
            vs = _kv_rows(st_refs[g], (0,), HEADS_PER_GROUP + h, widths[g]).astype(BF16)
            kn = _kv_rows(new_refs[g], (0,), h, nq).astype(BF16)
            vn = _kv_rows(new_refs[g], (0,), HEADS_PER_GROUP + h, nq).astype(BF16)
            ls = jax.lax.dot_general(q, ks, dn, preferred_element_type=F32) * SCALE + bs_refs[g][h]
            ln = jax.lax.dot_general(q, kn, dn, preferred_element_type=F32) * SCALE + bn_refs[g][h]
            m = jnp.maximum(jnp.max(ls, -1, keepdims=True), jnp.max(ln, -1, keepdims=True))
            es = jnp.exp(ls - m)
            en = jnp.exp(ln - m)
            ms.append(m)
            sums.append(jnp.sum(es, -1, keepdims=True) + jnp.sum(en, -1, keepdims=True))
            nums.append(jnp.dot(es.astype(BF16), vs, preferred_element_type=F32)
                        + jnp.dot(en.astype(BF16), vn, preferred_element_type=F32))
        m_all = jnp.maximum(jnp.maximum(ms[0], ms[1]), ms[2])
        coef = [jnp.exp(m - m_all) for m in ms]
        num = coef[0] * nums[0]
        den = coef[0] * sums[0]
        for g in range(1, ng):
            num = num + coef[g] * nums[g]
            den = den + coef[g] * sums[g]
        lo = h * HEAD_DIM
        o_ref[0, :, lo:lo + HEAD_DIM] = (num / den * _silu(gate_ref[0, :, lo:lo + HEAD_DIM])).astype(BF16)

    if emit_state:
        for g in range(ng):
            out = refs[3 + 4 * ng + g]
            rows, shift = widths[g] * KV_SLOTS, nq * KV_SLOTS
            out[0, 0:rows - shift, :] = st_refs[g][0, shift:rows, :]
            out[0, rows - shift:rows, :] = new_refs[g][0]


def _samp_attn(h, states, news, bias_state, bias_new, emit_state):
    b, nq, _ = h.shape
    widths = tuple(s.shape[1] // KV_SLOTS for s in states)
    whole = lambda a: pl.BlockSpec((1,) + a.shape[1:], lambda i: (i, 0, 0))
    const = lambda a: pl.BlockSpec(a.shape, lambda i: (0, 0, 0))
    out_spec = pl.BlockSpec((1, nq, GROUP_WIDTH), lambda i: (i, 0, 0))
    out_shape = jax.ShapeDtypeStruct((b, nq, GROUP_WIDTH), BF16)
    res = pl.pallas_call(
        functools.partial(_samp_attn_kernel, nq=nq, widths=widths, emit_state=emit_state),
        grid=(b,),
        in_specs=([pl.BlockSpec((1, nq, MIX_WIDTH), lambda i: (i, 0, 0)),
                   pl.BlockSpec((1, nq, GROUP_WIDTH), lambda i: (i, 0, MIX_WIDTH // GROUP_WIDTH))]
                  + [whole(s) for s in states] + [whole(n) for n in news]
                  + [const(a) for a in bias_state] + [const(a) for a in bias_new]),
        out_specs=[out_spec] + ([whole(s) for s in states] if emit_state else []),
        out_shape=[out_shape] + ([jax.ShapeDtypeStruct(s.shape, s.dtype) for s in states] if emit_state else []),
        compiler_params=_params("arbitrary"),
        name="samp_attn",
    )(h, h, *states, *news, *bias_state, *bias_new)
    return res[0], list(res[1:])


def _t5_bucket(dist):
    max_exact = REL_BUCKETS // 2
    safe = jnp.maximum(dist, 1).astype(F32)
    large = max_exact + (jnp.log(safe / max_exact) / math.log(REL_MAX_DIST / max_exact)
                         * (REL_BUCKETS - max_exact)).astype(jnp.int32)
    large = jnp.minimum(large, REL_BUCKETS - 1)
    return jnp.where(dist < max_exact, dist, large)


def _group_bias(rel_bias, g):
    w, d = B_GROUPS[g]
    dist = d * jnp.arange(w // d + 1, dtype=jnp.int32)
    tab = rel_bias[_t5_bucket(dist)]
    return tab[:, g * HEADS_PER_GROUP:(g + 1) * HEADS_PER_GROUP].T.astype(F32)


def _toeplitz(vec, rows, cols):
    hh, p = vec.shape
    flat = jnp.tile(vec, (1, rows))[:, :rows * (p - 1)]
    return flat.reshape(hh, rows, p - 1)[:, :, :cols]


def _band_bias(bias):
    period = 3 * BAND
    vec = jnp.full((bias.shape[0], period), NEG_INF, F32)
    vec = jax.lax.dynamic_update_slice(vec, bias[:, ::-1], (0, 0))
    mat = _toeplitz(vec, BAND, 2 * BAND)
    c = jnp.arange(2 * BAND, dtype=jnp.int32)[None, None, :]
    first = jnp.where(c >= BAND, mat, NEG_INF)
    return jnp.stack([first, mat])


def _sample_bias(bias, w, d, nq):
    hh = bias.shape[0]
    neg = jnp.full_like(bias, NEG_INF)
    by_dist = jnp.stack([bias] + [neg] * (d - 1), -1).reshape(hh, -1)
    by_dist = jnp.concatenate([by_dist[:, :w + 1], jnp.full((hh, nq), NEG_INF, F32)], 1)
    rev = by_dist[:, ::-1]
    bs = jnp.stack([rev[:, nq - n:nq - n + w] for n in range(nq)], 1)
    small = jnp.concatenate([jnp.full((hh, nq), NEG_INF, F32), by_dist[:, :nq]], 1)[:, ::-1]
    bn = jnp.stack([small[:, nq - 1 - n:2 * nq - 1 - n] for n in range(nq)], 1)
    return bs, bn


def _trunk(x, conv_prev, mem_kv, kv_state, a_w_in, a_w_dw, a_b_dw, a_cn_g, a_cn_b, a_w_out,
           b_w_in, b_w_out, w_kv_shared, rel_bias, ln_g, ln_b):
    b, t, d_model = x.shape
    m = b * t
    x2 = x.reshape(m, d_model)
    xb = x2
    new_conv = []
    for l in range(N_A_LAYERS):
        u, gates = _a_proj(xb, a_w_in, l)
        cm, nc = _a_mix(u.reshape(b, t, -1), gates.reshape(b, t, -1), conv_prev[l], mem_kv, l,
                        a_w_dw[l], a_b_dw[l], a_cn_g[l], a_cn_b[l])
        new_conv.append(nc)
        x2, xb = _out_ln(cm.reshape(m, d_model), a_w_out, l, x2, ln_g[l], ln_b[l])

    kv = _mm(xb, w_kv_shared[None], 0).reshape(b, t, 2 * MIX_WIDTH)
    biases =[_group_bias(rel_bias, g) for g in range(N_GROUPS)]
    kv_new = [_kv_window(kv, g, min(w, t)) for g, (w, _) in enumerate(B_GROUPS)]
    if kv_state is None:
        new_bufs = kv_new
        band_bias = jnp.stack([_band_bias(bi) for bi in biases])
    else:
        bias_state, bias_new = zip(*[_sample_bias(bi, w, d, t) for bi, (w, d) in zip(biases, B_GROUPS)])

    x3 = x2.reshape(b, t, d_model)
    for i in range(DEPTH - N_A_LAYERS):
        l = N_A_LAYERS + i
        h = _mm(xb.reshape(m, d_model), b_w_in, i).reshape(b, t, -1)
        if kv_state is None:
            o = _b_attn(h, kv, band_bias)
        else:
            o, updated = _samp_attn(h, kv_state, kv_new, bias_state, bias_new, emit_state=(i == 0))
            if i == 0:
                new_bufs = updated
        x3, xb = _b_out(o, h, mem_kv, l, b_w_out, i, x3, ln_g[l], ln_b[l])
    new_bufs = [nb.reshape(b, -1, 2, HEADS_PER_GROUP, HEAD_DIM) for nb in new_bufs]
    return x3, jnp.stack(new_conv), new_bufs


def kernel(x_prompt, x_sample, state_conv, state_kv_g0, state_kv_g1, state_kv_g2, cache_mem_kv, mem_prompt,
           a_w_in, a_w_dw, a_b_dw, a_cn_g, a_cn_b, a_w_out, b_w_in, b_w_out, w_kv_shared, w_mem_kv,
           rel_bias, ln_g, ln_b):
    bp = x_prompt.shape[0]
    bs = x_sample.shape[0]
    mem_p = _mem_proj(mem_prompt.reshape(bp * N_MEM, D_MODEL), w_mem_kv).reshape(
        DEPTH, bp, N_MEM * KV_SLOTS, HEAD_DIM)
    new_mem_kv = mem_p.reshape(DEPTH, bp, N_MEM, 2, MEM_HEADS, HEAD_DIM)
    conv_zero = jnp.zeros((N_A_LAYERS, bp, CONV_WIDTH - 1, CONV_CH), x_prompt.dtype)
    weights = (a_w_in, a_w_dw, a_b_dw, a_cn_g, a_cn_b, a_w_out, b_w_in, b_w_out, w_kv_shared, rel_bias, ln_g, ln_b)

    y_p, conv_p, bufs_p = _trunk(x_prompt, conv_zero, mem_p, None, *weights)
    kv_state = [s.reshape(bs, s.shape[1] * KV_SLOTS, HEAD_DIM) for s in (state_kv_g0, state_kv_g1, state_kv_g2)]
    mem_s = cache_mem_kv.reshape(DEPTH, bs, N_MEM * KV_SLOTS, HEAD_DIM)
    y_s, conv_s, bufs_s = _trunk(x_sample, state_conv, mem_s, kv_state, *weights)

    return (y_p, y_s, conv_p, conv_s, bufs_p[0], bufs_s[0], bufs_p[1], bufs_s[1], bufs_p[2], bufs_s[2], new_mem_kv)
```

```python
import functools
import math

import jax
import jax.numpy as jnp
from jax.experimental import pallas as pl
from jax.experimental.pallas import tpu as pltpu

D_MODEL = 2048
DEPTH = 4
HEAD_DIM = 128
MEM_HEADS = 4
MEM_WIDTH = MEM_HEADS * HEAD_DIM
N_MEM = 256
MIX_WIDTH = D_MODEL - MEM_WIDTH
N_A_LAYERS = DEPTH // 2
CONV_CH = MIX_WIDTH
CONV_WIDTH = 31
B_GROUPS = ((128, 1), (512, 4), (2048, 16))
N_GROUPS = len(B_GROUPS)
HEADS_PER_GROUP = 4
B_HEADS = N_GROUPS * HEADS_PER_GROUP
GROUP_WIDTH = HEADS_PER_GROUP * HEAD_DIM
KV_SLOTS = 2 * HEADS_PER_GROUP
B_OUT = GROUP_WIDTH
REL_BUCKETS = 32
REL_MAX_DIST = 2048
BAND = 128
ATTN_TOKENS = BAND * max(d for _, d in B_GROUPS)
LN_EPS = 1e-5
ALPHA = (2 * DEPTH) ** 0.25
NEG_INF = -1e30
SCALE = HEAD_DIM ** -0.5

HALO = 32
VMEM_LIMIT = 56 * 1024 * 1024
BF16 = jnp.bfloat16
F32 = jnp.float32


def _params(*sem):
    return pltpu.CompilerParams(dimension_semantics=sem, vmem_limit_bytes=VMEM_LIMIT)


def _sigmoid(x):
    return 1.0 / (1.0 + jnp.exp(-x))


def _silu(x):
    return x * _sigmoid(x)


def _layernorm(z, g, b):
    mu = jnp.mean(z, -1, keepdims=True)
    zc = z - mu
    var = jnp.mean(zc * zc, -1, keepdims=True)
    return zc * jax.lax.rsqrt(var + LN_EPS) * g + b


def _lhs_bf16(x_ref, scratch, col_axis):
    if not scratch:
        return x_ref[...]
    xb_ref, = scratch

    @pl.when(pl.program_id(col_axis) == 0)
    def _():
        xb_ref[...] = x_ref[...].astype(BF16)

    return xb_ref[...]


def _row_tile(x):
    m, k = x.shape
    if x.dtype == BF16:
        return min(m, 2048), []
    return min(m, 1024), [pltpu.VMEM((min(m, 1024), k), BF16)]


def _mm_kernel(x_ref, w_ref, o_ref, *scratch):
    xb = _lhs_bf16(x_ref, scratch, 2)
    o_ref[...] = jnp.dot(xb, w_ref[...].astype(BF16), preferred_element_type=F32)


def _mm(x, w, layer=None, *, tn=512):
    m, k = x.shape
    n = w.shape[2]
    tm, scratch = _row_tile(x)
    l0, nl = (0, w.shape[0]) if layer is None else (layer, 1)
    out = pl.pallas_call(
        _mm_kernel,
        grid=(nl, m // tm, n // tn),
        in_specs=[pl.BlockSpec((tm, k), lambda l, i, j: (i, 0)),
                  pl.BlockSpec((None, k, tn), lambda l, i, j: (l0 + l, 0, j))],
        out_specs=pl.BlockSpec((None, tm, tn), lambda l, i, j: (l, i, j)),
        out_shape=jax.ShapeDtypeStruct((nl, m, n), F32),
        scratch_shapes=scratch,
        compiler_params=_params("arbitrary", "arbitrary", "arbitrary"),
        name="mm",
    )(x, w)
    return out if layer is None else out.reshape(m, n)


def _store_kv_slots(o_ref, lead, slot0, vals):
    rows = vals.shape[0]
    for j in range(vals.shape[1] // HEAD_DIM):
        o_ref[lead + (pl.ds(slot0 + j, rows, stride=KV_SLOTS), slice(None))] = vals[:, j * HEAD_DIM:(j + 1) * HEAD_DIM]


def _mem_proj_kernel(x_ref, w_ref, o_ref, xb_ref):
    @pl.when(pl.program_id(0) == 0)
    def _():
        xb_ref[...] = x_ref[...].astype(BF16)

    _store_kv_slots(o_ref, (), 0, jnp.dot(xb_ref[...], w_ref[...].astype(BF16), preferred_element_type=F32))


def _mem_proj(x, w):
    m, k = x.shape
    nl, _, n = w.shape
    return pl.pallas_call(
        _mem_proj_kernel,
        grid=(nl,),
        in_specs=[pl.BlockSpec((m, k), lambda l: (0, 0)),
                  pl.BlockSpec((None, k, n), lambda l: (l, 0, 0))],
        out_specs=pl.BlockSpec((None, m * KV_SLOTS, HEAD_DIM), lambda l: (l, 0, 0)),
        out_shape=jax.ShapeDtypeStruct((nl, m * KV_SLOTS, HEAD_DIM), F32),
        scratch_shapes=[pltpu.VMEM((m, k), BF16)],
        compiler_params=_params("arbitrary"),
        name="mem_proj",
    )(x, w)


def _kv_window_kernel(k_ref, v_ref, o_ref):
    _store_kv_slots(o_ref, (0,), 0, k_ref[0])
    _store_kv_slots(o_ref, (0,), HEADS_PER_GROUP, v_ref[0])


def _kv_window(kv, g, w):
    b, t, _ = kv.shape
    rows = min(w, 512)
    first = (t - w) // rows
    return pl.pallas_call(
        _kv_window_kernel,
        grid=(b, w // rows),
        in_specs=[pl.BlockSpec((1, rows, GROUP_WIDTH), lambda bi, i: (bi, first + i, g)),
                  pl.BlockSpec((1, rows, GROUP_WIDTH), lambda bi, i: (bi, first + i, N_GROUPS + g))],
        out_specs=pl.BlockSpec((1, rows * KV_SLOTS, HEAD_DIM), lambda bi, i: (bi, i, 0)),
        out_shape=jax.ShapeDtypeStruct((b, w * KV_SLOTS, HEAD_DIM), F32),
        compiler_params=_params("arbitrary", "arbitrary"),
        name="kv_window",
    )(kv, kv)


A_TN = 512
N_GLU = CONV_CH // A_TN
QM_STEP = 3 * N_GLU


def _a_proj_kernel(x_ref, w_ref, u_ref, g_ref, ga_ref, *scratch):
    j = pl.program_id(1)
    xb = _lhs_bf16(x_ref, scratch, 1)

    def proj():
        return jnp.dot(xb, w_ref[...].astype(BF16), preferred_element_type=F32)

    is_glu = j < 2 * N_GLU

    @pl.when(is_glu & (j % 2 == 0))
    def _():
        ga_ref[...] = proj()

    @pl.when(is_glu & (j % 2 == 1))
    def _():
        u_ref[...] = ga_ref[...] * _sigmoid(proj())

    @pl.when(j == QM_STEP)
    def _():
        g_ref[...] = proj().astype(BF16)

    @pl.when(jnp.logical_not(is_glu) & (j != QM_STEP))
    def _():
        g_ref[...] = _silu(proj()).astype(BF16)


def _a_proj(x, w, layer):
    m, k = x.shape
    tm, scratch = _row_tile(x)
    n_steps = 3 * N_GLU + 2 * MEM_WIDTH // A_TN

    def w_col(j):
        return jnp.where(j < 2 * N_GLU, (j % 2) * N_GLU + j // 2, j)

    return pl.pallas_call(
        _a_proj_kernel,
        grid=(m // tm, n_steps),
        in_specs=[pl.BlockSpec((tm, k), lambda i, j: (i, 0)),
                  pl.BlockSpec((None, k, A_TN), lambda i, j: (layer, 0, w_col(j)))],
        out_specs=[pl.BlockSpec((tm, A_TN), lambda i, j: (i, jnp.minimum(j // 2, N_GLU - 1))),
                   pl.BlockSpec((tm, A_TN), lambda i, j: (i, jnp.maximum(j - 2 * N_GLU, 0)))],
        out_shape=[jax.ShapeDtypeStruct((m, CONV_CH), F32),
                   jax.ShapeDtypeStruct((m, CONV_CH + 2 * MEM_WIDTH), BF16)],
        scratch_shapes=[pltpu.VMEM((tm, A_TN), F32)] + scratch,
        compiler_params=_params("arbitrary", "arbitrary"),
        name="a_proj",
    )(x, w)


def _out_ln_kernel(cm_ref, w_ref, x_ref, g_ref, b_ref, o_ref, ob_ref, wb_ref):
    @pl.when(pl.program_id(0) == 0)
    def _():
        wb_ref[...] = w_ref[...].astype(BF16)

    y = jnp.dot(cm_ref[...], wb_ref[...], preferred_element_type=F32)
    out = _layernorm(ALPHA * x_ref[...] + y, g_ref[...], b_ref[...])
    o_ref[...] = out
    ob_ref[...] = out.astype(BF16)


def _out_ln(cm, w, layer, x, g, b):
    m, kk = cm.shape
    d = w.shape[2]
    tm = min(m, 512)
    row = pl.BlockSpec((tm, d), lambda i: (i, 0))
    return pl.pallas_call(
        _out_ln_kernel,
        grid=(m // tm,),
        in_specs=[pl.BlockSpec((tm, kk), lambda i: (i, 0)),
                  pl.BlockSpec((None, kk, d), lambda i: (layer, 0, 0), pipeline_mode=pl.Buffered(1)),
                  row,
                  pl.BlockSpec((1, d), lambda i: (0, 0)),
                  pl.BlockSpec((1, d), lambda i: (0, 0))],
        out_specs=[row, row],
        out_shape=[jax.ShapeDtypeStruct((m, d), F32), jax.ShapeDtypeStruct((m, d), BF16)],
        scratch_shapes=[pltpu.VMEM((kk, d), BF16)],
        compiler_params=_params("arbitrary"),
        name="out_ln",
    )(cm, w, x, g.reshape(1, d), b.reshape(1, d))


def _kv_rows(ref, lead, slot, n):
    return ref[lead + (pl.ds(slot, n, stride=KV_SLOTS), slice(None))]


def _mem_attn(qm, mem_ref, entry=0):
    outs = []
    for h in range(MEM_HEADS):
        lo = h * HEAD_DIM
        q = qm[:, lo:lo + HEAD_DIM]
        k = _kv_rows(mem_ref, (entry,), h, N_MEM).astype(BF16)
        v = _kv_rows(mem_ref, (entry,), MEM_HEADS + h, N_MEM).astype(BF16)
        s = jax.lax.dot_general(q, k, (((1,), (1,)), ((), ())), preferred_element_type=F32) * SCALE
        s = s - jnp.max(s, -1, keepdims=True)
        e = jnp.exp(s)
        p = e / jnp.sum(e, -1, keepdims=True)
        outs.append(jnp.dot(p.astype(BF16), v, preferred_element_type=F32))
    return jnp.concatenate(outs, -1)


def _a_mix_kernel(*refs, tt, tr, has_halo):
    if has_halo:
        (u_ref, sgate, qm, sgm, u_prev, cprev, mem, wdw, bdw, cng, cnb,
         cm_out, nc_out, ext, wrep, conv) = refs
    else:
        (u_ref, sgate, qm, sgm, cprev, mem, wdw, bdw, cng, cnb,
         cm_out, nc_out, ext, wrep, conv) = refs
    i = pl.program_id(1)
    n_blk = CONV_CH // HEAD_DIM

    @pl.when((pl.program_id(0) == 0) & (i == 0))
    def _():
        for k in range(CONV_WIDTH):
            wrep[k] = jnp.broadcast_to(wdw[k:k + 1, :], (8, CONV_CH))

    for c in range(n_blk):
        lanes = slice(c * HEAD_DIM, (c + 1) * HEAD_DIM)
        if has_halo:
            ext[c, 0:HALO, :] = jnp.where(i == 0, cprev[0, :, lanes], u_prev[0, :, lanes])
        else:
            ext[c, 0:HALO, :] = cprev[0, :, lanes]
        ext[c, HALO:HALO + tt, :] = u_ref[0, :, lanes]
        nc_out[0, :, lanes] = ext[c, HALO + tt - (CONV_WIDTH - 1):HALO + tt, :]

    off0 = HALO - (CONV_WIDTH - 1)
    for c in range(n_blk):
        lanes = slice(c * HEAD_DIM, (c + 1) * HEAD_DIM)

        def body(rc, carry, c=c, lanes=lanes):
            r0 = pl.multiple_of(rc * tr, 8)
            acc = jnp.zeros((tr // 8, 8, HEAD_DIM), F32)
            for k in range(CONV_WIDTH):
                win = ext[c, pl.ds(r0 + k + off0, tr), :].reshape(tr // 8, 8, HEAD_DIM)
                acc = acc + win * wrep[k, :, lanes][None]
            conv[pl.ds(r0, tr), lanes] = acc.reshape(tr, HEAD_DIM) + bdw[:, lanes]
            return carry

        jax.lax.fori_loop(0, tt // tr, body, 0)

    zn = _layernorm(conv[...], cng[...], cnb[...])
    cm_out[0, :, 0:CONV_CH] = (_silu(zn) * sgate[0].astype(F32)).astype(BF16)
    cm_out[0, :, CONV_CH:D_MODEL] = (_mem_attn(qm[0], mem) * sgm[0].astype(F32)).astype(BF16)


def _a_mix(u, gates, conv_prev, mem, layer, w_dw, b_dw, cn_g, cn_b):
    b, t, _ = u.shape
    tt = min(t, 512)
    tr = min(tt, 128)
    nt = t // tt
    has_halo = nt > 1
    cprev = jnp.pad(conv_prev, ((0, 0), (HALO - (CONV_WIDTH - 1), 0), (0, 0)))
    c3 = CONV_CH
    qcol = CONV_CH // MEM_WIDTH

    in_specs = [pl.BlockSpec((1, tt, c3), lambda bi, i: (bi, i, 0)),
                pl.BlockSpec((1, tt, c3), lambda bi, i: (bi, i, 0)),
                pl.BlockSpec((1, tt, MEM_WIDTH), lambda bi, i: (bi, i, qcol)),
                pl.BlockSpec((1, tt, MEM_WIDTH), lambda bi, i: (bi, i, qcol + 1))]
    args = [u, gates, gates, gates]
    if has_halo:
        per = tt // HALO
        in_specs += [pl.BlockSpec((1, HALO, c3), lambda bi, i: (bi, jnp.maximum(i * per - 1, 0), 0))]
        args += [u]
    in_specs += [pl.BlockSpec((1, HALO, c3), lambda bi, i: (bi, 0, 0)),
                 pl.BlockSpec((None, 1, N_MEM * KV_SLOTS, HEAD_DIM), lambda bi, i: (layer, bi, 0, 0)),
                 pl.BlockSpec((CONV_WIDTH, c3), lambda bi, i: (0, 0)),
                 pl.BlockSpec((1, c3), lambda bi, i: (0, 0)),
                 pl.BlockSpec((1, c3), lambda bi, i: (0, 0)),
                 pl.BlockSpec((1, c3), lambda bi, i: (0, 0))]
    args += [cprev, mem, w_dw, b_dw.reshape(1, c3), cn_g.reshape(1, c3), cn_b.reshape(1, c3)]

    return pl.pallas_call(
        functools.partial(_a_mix_kernel, tt=tt, tr=tr, has_halo=has_halo),
        grid=(b, nt),
        in_specs=in_specs,
        out_specs=[pl.BlockSpec((1, tt, D_MODEL), lambda bi, i: (bi, i, 0)),
                   pl.BlockSpec((1, CONV_WIDTH - 1, c3), lambda bi, i: (bi, 0, 0))],
        out_shape=[jax.ShapeDtypeStruct((b, t, D_MODEL), BF16),
                   jax.ShapeDtypeStruct((b, CONV_WIDTH - 1, c3), F32)],
        scratch_shapes=[pltpu.VMEM((c3 // HEAD_DIM, tt + HALO, HEAD_DIM), F32),
                        pltpu.VMEM((CONV_WIDTH, 8, c3), F32),
                        pltpu.VMEM((tt, c3), F32)],
        compiler_params=_params("arbitrary", "arbitrary"),
        name="a_mix",
    )(*args)


def _rows(ref, start, size, stride):
    if stride == 1:
        return ref[0, start:start + size, :]
    return ref[0, pl.ds(start, size, stride=stride), :]


def _b_attn_kernel(q0, q1, q2, k0, k1, k2, v0, v1, v2, hk0, hk1, hk2, hv0, hv1, hv2, bias_ref, gate_ref,
                   o_ref, num_scr, m_scr, s_scr):
    first = jnp.minimum(pl.program_id(2), 1)
    qs, ks, vs = (q0, q1, q2), (k0, k1, k2), (v0, v1, v2)
    hks, hvs = (hk0, hk1, hk2), (hv0, hv1, hv2)
    dn = (((1,), (1,)), ((), ()))
    for g, (_, d) in enumerate(B_GROUPS):
        for r in range(d):
            for s in range(ATTN_TOKENS // (BAND * d)):
                start = r + d * BAND * s
                q = _rows(qs[g], start, BAND, d)
                if s == 0:
                    k = jnp.concatenate([_rows(hks[g], r, BAND, d), _rows(ks[g], r, BAND, d)], 0)
                    v = jnp.concatenate([_rows(hvs[g], r, BAND, d), _rows(vs[g], r, BAND, d)], 0)
                    bias = bias_ref[g, first, 0]
                else:
                    k = _rows(ks[g], start - d * BAND, 2 * BAND, d)
                    v = _rows(vs[g], start - d * BAND, 2 * BAND, d)
                    bias = bias_ref[g, 1, 0]
                logits = jax.lax.dot_general(q.astype(BF16), k.astype(BF16), dn, preferred_element_type=F32)
                logits = logits * SCALE + bias
                m = jnp.max(logits, -1, keepdims=True)
                e = jnp.exp(logits - m)
                ssum = jnp.sum(e, -1, keepdims=True)
                num = jnp.dot(e.astype(BF16), v.astype(BF16), preferred_element_type=F32)
                if d == 1:
                    dst = pl.ds(start, BAND)
                else:
                    dst = pl.ds(start, BAND, stride=d)
                num_scr[g, dst, :] = num
                m_scr[g, dst, :] = jnp.broadcast_to(m, (BAND, HEAD_DIM))
                s_scr[g, dst, :] = jnp.broadcast_to(ssum, (BAND, HEAD_DIM))

    def merge(c, carry):
        sl = pl.ds(pl.multiple_of(c * BAND, BAND), BAND)
        ms = [m_scr[g, sl, :] for g in range(N_GROUPS)]
        m_all = jnp.maximum(jnp.maximum(ms[0], ms[1]), ms[2])
        coef = [jnp.exp(m - m_all) for m in ms]
        num = coef[0] * num_scr[0, sl, :]
        den = coef[0] * s_scr[0, sl, :]
        for g in range(1, N_GROUPS):
            num = num + coef[g] * num_scr[g, sl, :]
            den = den + coef[g] * s_scr[g, sl, :]
        o_ref[0, sl, :] = (num / den * _silu(gate_ref[0, sl, :])).astype(BF16)
        return carry

    jax.lax.fori_loop(0, ATTN_TOKENS // BAND, merge, 0)


def _b_attn(h, kv, band_bias):
    b, t, _ = h.shape
    tb = ATTN_TOKENS
    blk = (1, tb, HEAD_DIM)

    def cur(col0):
        return lambda bi, hi, i: (bi, i, col0 + hi)

    def halo_spec(d, col0):
        rows = BAND * d
        per = tb // rows
        return pl.BlockSpec((1, rows, HEAD_DIM), lambda bi, hi, i: (bi, jnp.maximum(i * per - 1, 0), col0 + hi))

    hpg = HEADS_PER_GROUP
    in_specs = ([pl.BlockSpec(blk, cur(hpg * g)) for g in range(N_GROUPS)]
                + [pl.BlockSpec(blk, cur(hpg * g)) for g in range(N_GROUPS)]
                + [pl.BlockSpec(blk, cur(B_HEADS + hpg * g)) for g in range(N_GROUPS)]
                + [halo_spec(d, hpg * g) for g, (_, d) in enumerate(B_GROUPS)]
                + [halo_spec(d, B_HEADS + hpg * g) for g, (_, d) in enumerate(B_GROUPS)]
                + [pl.BlockSpec((N_GROUPS, 2, 1, BAND, 2 * BAND), lambda bi, hi, i: (0, 0, hi, 0, 0)),
                   pl.BlockSpec(blk, cur(B_HEADS))])
    return pl.pallas_call(
        _b_attn_kernel,
        grid=(b, hpg, t // tb),
        in_specs=in_specs,
        out_specs=pl.BlockSpec(blk, lambda bi, hi, i: (bi, i, hi)),
        out_shape=jax.ShapeDtypeStruct((b, t, B_OUT), BF16),
        scratch_shapes=[pltpu.VMEM((N_GROUPS, tb, HEAD_DIM), F32)] * 3,
        compiler_params=_params("parallel", "parallel", "arbitrary"),
        name="b_attn",
    )(h, h, h, kv, kv, kv, kv, kv, kv, kv, kv, kv, kv, kv, kv, band_bias, h)


def _b_out_kernel(o_ref, qm_ref, gm_ref, mem_ref, w_ref, x_ref, g_ref, b_ref, y_ref, yb_ref, wb_ref, *, nb, tt):
    @pl.when((pl.program_id(0) == 0) & (pl.program_id(1) == 0))
    def _():
        wb_ref[...] = w_ref[...].astype(BF16)

    stack = lambda parts: parts[0] if nb == 1 else jnp.concatenate(parts, 0)
    mbranch = stack([_mem_attn(qm_ref[bi].astype(BF16), mem_ref, bi) * _silu(gm_ref[bi]) for bi in range(nb)])
    if nb == 1:
        obranch = o_ref[0]
    else:
        obranch = jnp.concatenate([o_ref[bi].astype(F32) for bi in range(nb)], 0).astype(BF16)
    y = jnp.dot(obranch, wb_ref[0:B_OUT, :], preferred_element_type=F32)
    y = y + jnp.dot(mbranch.astype(BF16), wb_ref[B_OUT:B_OUT + MEM_WIDTH, :], preferred_element_type=F32)
    x = stack([x_ref[bi] for bi in range(nb)])
    out = _layernorm(ALPHA * x + y, g_ref[...], b_ref[...])
    for bi in range(nb):
        y_ref[bi] = out[bi * tt:(bi + 1) * tt]
        yb_ref[bi] = out[bi * tt:(bi + 1) * tt].astype(BF16)


def _b_out(o, h, mem, layer, w, wl, x, g, b):
    bsz, t, d = x.shape
    tt = min(t, 512)
    nb = max(1, min(bsz, 64 // tt))
    kk = w.shape[1]
    qcol = (MIX_WIDTH + B_OUT) // MEM_WIDTH
    row = pl.BlockSpec((nb, tt, d), lambda bi, i: (bi, i, 0))
    return pl.pallas_call(
        functools.partial(_b_out_kernel, nb=nb, tt=tt),
        grid=(bsz // nb, t // tt),
        in_specs=[pl.BlockSpec((nb, tt, B_OUT), lambda bi, i: (bi, i, 0)),
                  pl.BlockSpec((nb, tt, MEM_WIDTH), lambda bi, i: (bi, i, qcol)),
                  pl.BlockSpec((nb, tt, MEM_WIDTH), lambda bi, i: (bi, i, qcol + 1)),
                  pl.BlockSpec((None, nb, N_MEM * KV_SLOTS, HEAD_DIM), lambda bi, i: (layer, bi, 0, 0)),
                  pl.BlockSpec((None, kk, d), lambda bi, i: (wl, 0, 0), pipeline_mode=pl.Buffered(1)),
                  row,
                  pl.BlockSpec((1, d), lambda bi, i: (0, 0)),
                  pl.BlockSpec((1, d), lambda bi, i: (0, 0))],
        out_specs=[row, row],
        out_shape=[jax.ShapeDtypeStruct((bsz, t, d), F32), jax.ShapeDtypeStruct((bsz, t, d), BF16)],
        scratch_shapes=[pltpu.VMEM((kk, d), BF16)],
        compiler_params=_params("arbitrary", "arbitrary"),
        name="b_out",
    )(o, h, h, mem, w, x, g.reshape(1, d), b.reshape(1, d))


def _samp_attn_kernel(*refs, nq, widths, emit_state):
    ng = N_GROUPS
    q_ref, gate_ref = refs[0], refs[1]
    st_refs, new_refs = refs[2:2 + ng], refs[2 + ng:2 + 2 * ng]
    bs_refs, bn_refs = refs[2 + 2 * ng:2 + 3 * ng], refs[2 + 3 * ng:2 + 4 * ng]
    o_ref = refs[2 + 4 * ng]
    dn = (((1,), (1,)), ((), ()))
    for h in range(HEADS_PER_GROUP):
        ms, sums, nums = [], [], []
        for g in range(ng):
            lo = (g * HEADS_PER_GROUP + h) * HEAD_DIM
            q = q_ref[0, :, lo:lo + HEAD_DIM].astype(BF16)
            ks = _kv_rows(st_refs[g], (0,), h, widths[g]).astype(BF16)
            vs = _kv_rows(st_refs[g], (0,), HEADS_PER_GROUP + h, widths[g]).astype(BF16)
            kn = _kv_rows(new_refs[g], (0,), h, nq).astype(BF16)
            vn = _kv_rows(new_refs[g], (0,), HEADS_PER_GROUP + h, nq).astype(BF16)
            ls = jax.lax.dot_general(q, ks, dn, preferred_element_type=F32) * SCALE + bs_refs[g][h]
            ln = jax.lax.dot_general(q, kn, dn, preferred_element_type=F32) * SCALE + bn_refs[g][h]
            m = jnp.maximum(jnp.max(ls, -1, keepdims=True), jnp.max(ln, -1, keepdims=True))
            es = jnp.exp(ls - m)
            en = jnp.exp(ln - m)
            ms.append(m)
            sums.append(jnp.sum(es, -1, keepdims=True) + jnp.sum(en, -1, keepdims=True))
            nums.append(jnp.dot(es.astype(BF16), vs, preferred_element_type=F32)
                        + jnp.dot(en.astype(BF16), vn, preferred_element_type=F32))
        m_all = jnp.maximum(jnp.maximum(ms[0], ms[1]), ms[2])
        coef = [jnp.exp(m - m_all) for m in ms]
        num = coef[0] * nums[0]
        den = coef[0] * sums[0]
        for g in range(1, ng):
            num = num + coef[g] * nums[g]
            den = den + coef[g] * sums[g]
        lo = h * HEAD_DIM
        o_ref[0, :, lo:lo + HEAD_DIM] = (num / den * _silu(gate_ref[0, :, lo:lo + HEAD_DIM])).astype(BF16)

    if emit_state:
        for g in range(ng):
            out = refs[3 + 4 * ng + g]
            rows, shift = widths[g] * KV_SLOTS, nq * KV_SLOTS
            out[0, 0:rows - shift, :] = st_refs[g][0, shift:rows, :]
            out[0, rows - shift:rows, :] = new_refs[g][0]


def _samp_attn(h, states, news, bias_state, bias_new, emit_state):
    b, nq, _ = h.shape
    widths = tuple(s.shape[1] // KV_SLOTS for s in states)
    whole = lambda a: pl.BlockSpec((1,) + a.shape[1:], lambda i: (i, 0, 0))
    const = lambda a: pl.BlockSpec(a.shape, lambda i: (0, 0, 0))
    out_spec = pl.BlockSpec((1, nq, GROUP_WIDTH), lambda i: (i, 0, 0))
    out_shape = jax.ShapeDtypeStruct((b, nq, GROUP_WIDTH), BF16)
    res = pl.pallas_call(
        functools.partial(_samp_attn_kernel, nq=nq, widths=widths, emit_state=emit_state),
        grid=(b,),
        in_specs=([pl.BlockSpec((1, nq, MIX_WIDTH), lambda i: (i, 0, 0)),
                   pl.BlockSpec((1, nq, GROUP_WIDTH), lambda i: (i, 0, MIX_WIDTH // GROUP_WIDTH))]
                  + [whole(s) for s in states] + [whole(n) for n in news]
                  + [const(a) for a in bias_state] + [const(a) for a in bias_new]),
        out_specs=[out_spec] + ([whole(s) for s in states] if emit_state else []),
        out_shape=[out_shape] + ([jax.ShapeDtypeStruct(s.shape, s.dtype) for s in states] if emit_state else []),
        compiler_params=_params("arbitrary"),
        name="samp_attn",
    )(h, h, *states, *news, *bias_state, *bias_new)
    return res[0], list(res[1:])


def _t5_bucket(dist):
    max_exact = REL_BUCKETS // 2
    safe = jnp.maximum(dist, 1).astype(F32)
    large = max_exact + (jnp.log(safe / max_exact) / math.log(REL_MAX_DIST / max_exact)
                         * (REL_BUCKETS - max_exact)).astype(jnp.int32)
    large = jnp.minimum(large, REL_BUCKETS - 1)
    return jnp.where(dist < max_exact, dist, large)


def _group_bias(rel_bias, g):
    w, d = B_GROUPS[g]
    dist = d * jnp.arange(w // d + 1, dtype=jnp.int32)
    tab = rel_bias[_t5_bucket(dist)]
    return tab[:, g * HEADS_PER_GROUP:(g + 1) * HEADS_PER_GROUP].T.astype(F32)


def _toeplitz(vec, rows, cols):
    hh, p = vec.shape
    flat = jnp.tile(vec, (1, rows))[:, :rows * (p - 1)]
    return flat.reshape(hh, rows, p - 1)[:, :, :cols]


def _band_bias(bias):
    period = 3 * BAND
    vec = jnp.full((bias.shape[0], period), NEG_INF, F32)
    vec = jax.lax.dynamic_update_slice(vec, bias[:, ::-1], (0, 0))
    mat = _toeplitz(vec, BAND, 2 * BAND)
    c = jnp.arange(2 * BAND, dtype=jnp.int32)[None, None, :]
    first = jnp.where(c >= BAND, mat, NEG_INF)
    return jnp.stack([first, mat])


def _sample_bias(bias, w, d, nq):
    hh = bias.shape[0]
    neg = jnp.full_like(bias, NEG_INF)
    by_dist = jnp.stack([bias] + [neg] * (d - 1), -1).reshape(hh, -1)
    by_dist = jnp.concatenate([by_dist[:, :w + 1], jnp.full((hh, nq), NEG_INF, F32)], 1)
    rev = by_dist[:, ::-1]
    bs = jnp.stack([rev[:, nq - n:nq - n + w] for n in range(nq)], 1)
    small = jnp.concatenate([jnp.full((hh, nq), NEG_INF, F32), by_dist[:, :nq]], 1)[:, ::-1]
    bn = jnp.stack([small[:, nq - 1 - n:2 * nq - 1 - n] for n in range(nq)], 1)
    return bs, bn


def _trunk(x, conv_prev, mem_kv, kv_state, a_w_in, a_w_dw, a_b_dw, a_cn_g, a_cn_b, a_w_out,
           b_w_in, b_w_out, w_kv_shared, rel_bias, ln_g, ln_b):
    b, t, d_model = x.shape
    m = b * t
    x2 = x.reshape(m, d_model)
    xb = x2
    new_conv = []
    for l in range(N_A_LAYERS):
        u, gates = _a_proj(xb, a_w_in, l)
        cm, nc = _a_mix(u.reshape(b, t, -1), gates.reshape(b, t, -1), conv_prev[l], mem_kv, l,
                        a_w_dw[l], a_b_dw[l], a_cn_g[l], a_cn_b[l])
        new_conv.append(nc)
        x2, xb = _out_ln(cm.reshape(m, d_model), a_w_out, l, x2, ln_g[l], ln_b[l])

    kv = _mm(xb, w_kv_shared[None], 0).reshape(b, t, 2 * MIX_WIDTH)
    biases = [_group_bias(rel_bias, g) for g in range(N_GROUPS)]
    kv_new = [_kv_window(kv, g, min(w, t)) for g, (w, _) in enumerate(B_GROUPS)]
    if kv_state is None:
        new_bufs = kv_new
        band_bias = jnp.stack([_band_bias(bi) for bi in biases])
    else:
        bias_state, bias_new = zip(*[_sample_bias(bi, w, d, t) for bi, (w, d) in zip(biases, B_GROUPS)])

    x3 = x2.reshape(b, t, d_model)
    for i in range(DEPTH - N_A_LAYERS):
        l = N_A_LAYERS + i
        h = _mm(xb.reshape(m, d_model), b_w_in, i).reshape(b, t, -1)
        if kv_state is None:
            o = _b_attn(h, kv, band_bias)
        else:
            o, updated = _samp_attn(h, kv_state, kv_new, bias_state, bias_new, emit_state=(i == 0))
            if i == 0:
                new_bufs = updated
        x3, xb = _b_out(o, h, mem_kv, l, b_w_out, i, x3, ln_g[l], ln_b[l])
    new_bufs = [nb.reshape(b, -1, 2, HEADS_PER_GROUP, HEAD_DIM) for nb in new_bufs]
    return x3, jnp.stack(new_conv), new_bufs


def kernel(x_prompt, x_sample, state_conv, state_kv_g0, state_kv_g1, state_kv_g2, cache_mem_kv, mem_prompt,
           a_w_in, a_w_dw, a_b_dw, a_cn_g, a_cn_b, a_w_out, b_w_in, b_w_out, w_kv_shared, w_mem_kv,
           rel_bias, ln_g, ln_b):
    bp = x_prompt.shape[0]
    bs = x_sample.shape[0]
    mem_p = _mem_proj(mem_prompt.reshape(bp * N_MEM, D_MODEL), w_mem_kv).reshape(
        DEPTH, bp, N_MEM * KV_SLOTS, HEAD_DIM)
    new_mem_kv = mem_p.reshape(DEPTH, bp, N_MEM, 2, MEM_HEADS, HEAD_DIM)
    conv_zero = jnp.zeros((N_A_LAYERS, bp, CONV_WIDTH - 1, CONV_CH), x_prompt.dtype)
    weights = (a_w_in, a_w_dw, a_b_dw, a_cn_g, a_cn_b, a_w_out, b_w_in, b_w_out, w_kv_shared, rel_bias, ln_g, ln_b)

    y_p, conv_p, bufs_p = _trunk(x_prompt, conv_zero, mem_p, None, *weights)
    kv_state = [s.reshape(bs, s.shape[1] * KV_SLOTS, HEAD_DIM) for s in (state_kv_g0, state_kv_g1, state_kv_g2)]
    mem_s = cache_mem_kv.reshape(DEPTH, bs, N_MEM * KV_SLOTS, HEAD_DIM)
    y_s, conv_s, bufs_s = _trunk(x_sample, state_conv, mem_s, kv_state, *weights)

    return (y_p, y_s, conv_p, conv_s, bufs_p[0], bufs_s[0], bufs_p[1], bufs_s[1], bufs_p[2], bufs_s[2], new_mem_kv)
```

```python
import functools
import math

import jax
import jax.numpy as jnp
from jax.experimental import pallas as pl
from jax.experimental.pallas import tpu as pltpu

D_MODEL = 2048
DEPTH = 4
HEAD_DIM = 128
MEM_HEADS = 4
MEM_WIDTH = MEM_HEADS * HEAD_DIM
N_MEM = 256
MIX_WIDTH = D_MODEL - MEM_WIDTH
N_A_LAYERS = DEPTH // 2
CONV_CH = MIX_WIDTH
CONV_WIDTH = 31
B_GROUPS = ((128, 1), (512, 4), (2048, 16))
N_GROUPS = len(B_GROUPS)
HEADS_PER_GROUP = 4
B_HEADS = N_GROUPS * HEADS_PER_GROUP
GROUP_WIDTH = HEADS_PER_GROUP * HEAD_DIM
KV_SLOTS = 2 * HEADS_PER_GROUP
B_OUT = GROUP_WIDTH
REL_BUCKETS = 32
REL_MAX_DIST = 2048
BAND = 128
ATTN_TOKENS = BAND * max(d for _, d in B_GROUPS)
LN_EPS = 1e-5
ALPHA = (2 * DEPTH) ** 0.25
NEG_INF = -1e30
SCALE = HEAD_DIM ** -0.5

HALO = 32
VMEM_LIMIT = 56 * 1024 * 1024
BF16 = jnp.bfloat16
F32 = jnp.float32


def _params(*sem):
    return pltpu.CompilerParams(dimension_semantics=sem, vmem_limit_bytes=VMEM_LIMIT)


def _sigmoid(x):
    return 1.0 / (1.0 + jnp.exp(-x))


def _silu(x):
    return x * _sigmoid(x)


def _layernorm(z, g, b):
    mu = jnp.mean(z, -1, keepdims=True)
    zc = z - mu
    var = jnp.mean(zc * zc, -1, keepdims=True)
    return zc * jax.lax.rsqrt(var + LN_EPS) * g + b


def _lhs_bf16(x_ref, scratch):
    if not scratch:
        return x_ref[...]
    xb_ref, = scratch

    @pl.when(pl.program_id(1) == 0)
    def _():
        xb_ref[...] = x_ref[...].astype(BF16)

    return xb_ref[...]


def _row_tile(x):
    m, k = x.shape
    if x.dtype == BF16:
        return min(m, 2048), []
    return min(m, 1024), [pltpu.VMEM((min(m, 1024), k), BF16)]


def _mm_kernel(x_ref, w_ref, o_ref, *scratch):
    xb = _lhs_bf16(x_ref, scratch)
    o_ref[...] = jnp.dot(xb, w_ref[...].astype(BF16), preferred_element_type=F32)


def _mm(x, w, layer, *, tn=512):
    m, k = x.shape
    n = w.shape[2]
    tm, scratch = _row_tile(x)
    return pl.pallas_call(
        _mm_kernel,
        grid=(m // tm, n // tn),
        in_specs=[pl.BlockSpec((tm, k), lambda i, j: (i, 0)),
                  pl.BlockSpec((None, k, tn), lambda i, j: (layer, 0, j))],
        out_specs=pl.BlockSpec((tm, tn), lambda i, j: (i, j)),
        out_shape=jax.ShapeDtypeStruct((m, n), F32),
        scratch_shapes=scratch,
        compiler_params=_params("arbitrary", "arbitrary"),
        name="mm",
    )(x, w)


def _store_kv_slots(o_ref, lead, slot0, vals):
    rows = vals.shape[0]
    for j in range(vals.shape[1] // HEAD_DIM):
        o_ref[lead + (pl.ds(slot0 + j, rows, stride=KV_SLOTS), slice(None))] = vals[:, j * HEAD_DIM:(j + 1) * HEAD_DIM]


def _mem_proj_kernel(x_ref, w_ref, o_ref, xb_ref):
    @pl.when(pl.program_id(0) == 0)
    def _():
        xb_ref[...] = x_ref[...].astype(BF16)

    _store_kv_slots(o_ref, (), 0, jnp.dot(xb_ref[...], w_ref[...].astype(BF16), preferred_element_type=F32))


def _mem_proj(x, w):
    m, k = x.shape
    nl, _, n = w.shape
    return pl.pallas_call(
        _mem_proj_kernel,
        grid=(nl,),
        in_specs=[pl.BlockSpec((m, k), lambda l: (0, 0)),
                  pl.BlockSpec((None, k, n), lambda l: (l, 0, 0))],
        out_specs=pl.BlockSpec((None, m * KV_SLOTS, HEAD_DIM), lambda l: (l, 0, 0)),
        out_shape=jax.ShapeDtypeStruct((nl, m * KV_SLOTS, HEAD_DIM), F32),
        scratch_shapes=[pltpu.VMEM((m, k), BF16)],
        compiler_params=_params("arbitrary"),
        name="mem_proj",
    )(x, w)


def _kv_window_kernel(k_ref, v_ref, o_ref):
    _store_kv_slots(o_ref, (0,), 0, k_ref[0])
    _store_kv_slots(o_ref, (0,), HEADS_PER_GROUP, v_ref[0])


def _kv_window(kv, g, w):
    b, t, _ = kv.shape
    rows = min(w, 512)
    first = (t - w) // rows
    return pl.pallas_call(
        _kv_window_kernel,
        grid=(b, w // rows),
        in_specs=[pl.BlockSpec((1, rows, GROUP_WIDTH), lambda bi, i: (bi, first + i, g)),
                  pl.BlockSpec((1, rows, GROUP_WIDTH), lambda bi, i: (bi, first + i, N_GROUPS + g))],
        out_specs=pl.BlockSpec((1, rows * KV_SLOTS, HEAD_DIM), lambda bi, i: (bi, i, 0)),
        out_shape=jax.ShapeDtypeStruct((b, w * KV_SLOTS, HEAD_DIM), F32),
        compiler_params=_params("arbitrary", "arbitrary"),
        name="kv_window",
    )(kv, kv)


A_TN = 512
N_GLU = CONV_CH // A_TN
QM_STEP = 3 * N_GLU


def _a_proj_kernel(x_ref, w_ref, u_ref, g_ref, ga_ref, *scratch):
    j = pl.program_id(1)
    xb = _lhs_bf16(x_ref, scratch)

    def proj():
        return jnp.dot(xb, w_ref[...].astype(BF16), preferred_element_type=F32)

    is_glu = j < 2 * N_GLU

    @pl.when(is_glu & (j % 2 == 0))
    def _():
        ga_ref[...] = proj()

    @pl.when(is_glu & (j % 2 == 1))
    def _():
        u_ref[...] = ga_ref[...] * _sigmoid(proj())

    @pl.when(j == QM_STEP)
    def _():
        g_ref[...] = proj().astype(BF16)

    @pl.when(jnp.logical_not(is_glu) & (j != QM_STEP))
    def _():
        g_ref[...] = _silu(proj()).astype(BF16)


def _a_proj(x, w, layer):
    m, k = x.shape
    tm, scratch = _row_tile(x)
    n_steps = 3 * N_GLU + 2 * MEM_WIDTH // A_TN

    def w_col(j):
        return jnp.where(j < 2 * N_GLU, (j % 2) * N_GLU + j // 2, j)

    return pl.pallas_call(
        _a_proj_kernel,
        grid=(m // tm, n_steps),
        in_specs=[pl.BlockSpec((tm, k), lambda i, j: (i, 0)),
                  pl.BlockSpec((None, k, A_TN), lambda i, j: (layer, 0, w_col(j)))],
        out_specs=[pl.BlockSpec((tm, A_TN), lambda i, j: (i, jnp.minimum(j // 2, N_GLU - 1))),
                   pl.BlockSpec((tm, A_TN), lambda i, j: (i, jnp.maximum(j - 2 * N_GLU, 0)))],
        out_shape=[jax.ShapeDtypeStruct((m, CONV_CH), F32),
                   jax.ShapeDtypeStruct((m, CONV_CH + 2 * MEM_WIDTH), BF16)],
        scratch_shapes=[pltpu.VMEM((tm, A_TN), F32)] + scratch,
        compiler_params=_params("arbitrary", "arbitrary"),
        name="a_proj",
    )(x, w)


def _out_ln_kernel(cm_ref, w_ref, x_ref, g_ref, b_ref, o_ref, ob_ref, wb_ref):
    @pl.when(pl.program_id(0) == 0)
    def _():
        wb_ref[...] = w_ref[...].astype(BF16)

    y = jnp.dot(cm_ref[...], wb_ref[...], preferred_element_type=F32)
    out = _layernorm(ALPHA * x_ref[...] + y, g_ref[...], b_ref[...])
    o_ref[...] = out
    ob_ref[...] = out.astype(BF16)


def _out_ln(cm, w, layer, x, g, b):
    m, kk = cm.shape
    d = w.shape[2]
    tm = min(m, 512)
    row = pl.BlockSpec((tm, d), lambda i: (i, 0))
    return pl.pallas_call(
        _out_ln_kernel,
        grid=(m // tm,),
        in_specs=[pl.BlockSpec((tm, kk), lambda i: (i, 0)),
                  pl.BlockSpec((None, kk, d), lambda i: (layer, 0, 0), pipeline_mode=pl.Buffered(1)),
                  row,
                  pl.BlockSpec((1, d), lambda i: (0, 0)),
                  pl.BlockSpec((1, d), lambda i: (0, 0))],
        out_specs=[row, row],
        out_shape=[jax.ShapeDtypeStruct((m, d), F32), jax.ShapeDtypeStruct((m, d), BF16)],
        scratch_shapes=[pltpu.VMEM((kk, d), BF16)],
        compiler_params=_params("arbitrary"),
        name="out_ln",
    )(cm, w, x, g.reshape(1, d), b.reshape(1, d))


def _kv_rows(ref, lead, slot, n):
    return ref[lead + (pl.ds(slot, n, stride=KV_SLOTS), slice(None))]


def _mem_attn(qm, mem_ref, entry=0):
    outs = []
    for h in range(MEM_HEADS):
        lo = h * HEAD_DIM
        q = qm[:, lo:lo + HEAD_DIM]
        k = _kv_rows(mem_ref, (entry,), h, N_MEM).astype(BF16)
        v = _kv_rows(mem_ref, (entry,), MEM_HEADS + h, N_MEM).astype(BF16)
        s = jax.lax.dot_general(q, k, (((1,), (1,)), ((), ())), preferred_element_type=F32) * SCALE
        s = s - jnp.max(s, -1, keepdims=True)
        e = jnp.exp(s)
        p = e / jnp.sum(e, -1, keepdims=True)
        outs.append(jnp.dot(p.astype(BF16), v, preferred_element_type=F32))
    return jnp.concatenate(outs, -1)


def _a_mix_kernel(*refs, tt, tr, has_halo):
    if has_halo:
        (u_ref, sgate, qm, sgm, u_prev, cprev, mem, wdw, bdw, cng, cnb,
         cm_out, nc_out, ext, wrep, conv) = refs
    else:
        (u_ref, sgate, qm, sgm, cprev, mem, wdw, bdw, cng, cnb,
         cm_out, nc_out, ext, wrep, conv) = refs
    i = pl.program_id(1)
    n_blk = CONV_CH // HEAD_DIM

    @pl.when((pl.program_id(0) == 0) & (i == 0))
    def _():
        for k in range(CONV_WIDTH):
            wrep[k] = jnp.broadcast_to(wdw[k:k + 1, :], (8, CONV_CH))

    for c in range(n_blk):
        lanes = slice(c * HEAD_DIM, (c + 1) * HEAD_DIM)
        if has_halo:
            ext[c, 0:HALO, :] = jnp.where(i == 0, cprev[0, :, lanes], u_prev[0, :, lanes])
        else:
            ext[c, 0:HALO, :] = cprev[0, :, lanes]
        ext[c, HALO:HALO + tt, :] = u_ref[0, :, lanes]
        nc_out[0, :, lanes] = ext[c, HALO + tt - (CONV_WIDTH - 1):HALO + tt, :]

    off0 = HALO - (CONV_WIDTH - 1)
    for c in range(n_blk):
        lanes = slice(c * HEAD_DIM, (c + 1) * HEAD_DIM)

        def body(rc, carry, c=c, lanes=lanes):
            r0 = pl.multiple_of(rc * tr, 8)
            acc = jnp.zeros((tr // 8, 8, HEAD_DIM), F32)
            for k in range(CONV_WIDTH):
                win = ext[c, pl.ds(r0 + k + off0, tr), :].reshape(tr // 8, 8, HEAD_DIM)
                acc = acc + win * wrep[k, :, lanes][None]
            conv[pl.ds(r0, tr), lanes] = acc.reshape(tr, HEAD_DIM) + bdw[:, lanes]
            return carry

        jax.lax.fori_loop(0, tt // tr, body, 0)

    zn = _layernorm(conv[...], cng[...], cnb[...])
    cm_out[0, :, 0:CONV_CH] = (_silu(zn) * sgate[0].astype(F32)).astype(BF16)
    cm_out[0, :, CONV_CH:D_MODEL] = (_mem_attn(qm[0], mem) * sgm[0].astype(F32)).astype(BF16)


def _a_mix(u, gates, conv_prev, mem, layer, w_dw, b_dw, cn_g, cn_b):
    b, t, _ = u.shape
    tt = min(t, 512)
    tr = min(tt, 128)
    nt = t // tt
    has_halo = nt > 1
    cprev = jnp.pad(conv_prev, ((0, 0), (HALO - (CONV_WIDTH - 1), 0), (0, 0)))
    c3 = CONV_CH
    qcol = CONV_CH // MEM_WIDTH

    in_specs = [pl.BlockSpec((1, tt, c3), lambda bi, i: (bi, i, 0)),
                pl.BlockSpec((1, tt, c3), lambda bi, i: (bi, i, 0)),
                pl.BlockSpec((1, tt, MEM_WIDTH), lambda bi, i: (bi, i, qcol)),
                pl.BlockSpec((1, tt, MEM_WIDTH), lambda bi, i: (bi, i, qcol + 1))]
    args = [u, gates, gates, gates]
    if has_halo:
        per = tt // HALO
        in_specs += [pl.BlockSpec((1, HALO, c3), lambda bi, i: (bi, jnp.maximum(i * per - 1, 0), 0))]
        args += [u]
    in_specs += [pl.BlockSpec((1, HALO, c3), lambda bi, i: (bi, 0, 0)),
                 pl.BlockSpec((None, 1, N_MEM * KV_SLOTS, HEAD_DIM), lambda bi, i: (layer, bi, 0, 0)),
                 pl.BlockSpec((CONV_WIDTH, c3), lambda bi, i: (0, 0)),
                 pl.BlockSpec((1, c3), lambda bi, i: (0, 0)),
                 pl.BlockSpec((1, c3), lambda bi, i: (0, 0)),
                 pl.BlockSpec((1, c3), lambda bi, i: (0, 0))]
    args += [cprev, mem, w_dw, b_dw.reshape(1, c3), cn_g.reshape(1, c3), cn_b.reshape(1, c3)]

    return pl.pallas_call(
        functools.partial(_a_mix_kernel, tt=tt, tr=tr, has_halo=has_halo),
        grid=(b, nt),
        in_specs=in_specs,
        out_specs=[pl.BlockSpec((1, tt, D_MODEL), lambda bi, i: (bi, i, 0)),
                   pl.BlockSpec((1, CONV_WIDTH - 1, c3), lambda bi, i: (bi, 0, 0))],
        out_shape=[jax.ShapeDtypeStruct((b, t, D_MODEL), BF16),
                   jax.ShapeDtypeStruct((b, CONV_WIDTH - 1, c3), F32)],
        scratch_shapes=[pltpu.VMEM((c3 // HEAD_DIM, tt + HALO, HEAD_DIM), F32),
                        pltpu.VMEM((CONV_WIDTH, 8, c3), F32),
                        pltpu.VMEM((tt, c3), F32)],
        compiler_params=_params("arbitrary", "arbitrary"),
        name="a_mix",
    )(*args)


def _rows(ref, start, size, stride):
    if stride == 1:
        return ref[0, start:start + size, :]
    return ref[0, pl.ds(start, size, stride=stride), :]


def _b_attn_kernel(q0, q1, q2, k0, k1, k2, v0, v1, v2, hk0, hk1, hk2, hv0, hv1, hv2, bias_ref, gate_ref,
                   o_ref, num_scr, m_scr, s_scr):
    first = jnp.minimum(pl.program_id(2), 1)
    qs, ks, vs = (q0, q1, q2), (k0, k1, k2), (v0, v1, v2)
    hks, hvs = (hk0, hk1, hk2), (hv0, hv1, hv2)
    dn = (((1,), (1,)), ((), ()))
    for g, (_, d) in enumerate(B_GROUPS):
        for r in range(d):
            for s in range(ATTN_TOKENS // (BAND * d)):
                start = r + d * BAND * s
                q = _rows(qs[g], start, BAND, d)
                if s == 0:
                    k = jnp.concatenate([_rows(hks[g], r, BAND, d), _rows(ks[g], r, BAND, d)], 0)
                    v = jnp.concatenate([_rows(hvs[g], r, BAND, d), _rows(vs[g], r, BAND, d)], 0)
                    bias = bias_ref[g, first, 0]
                else:
                    k = _rows(ks[g], start - d * BAND, 2 * BAND, d)
                    v = _rows(vs[g], start - d * BAND, 2 * BAND, d)
                    bias = bias_ref[g, 1, 0]
                logits = jax.lax.dot_general(q.astype(BF16), k.astype(BF16), dn, preferred_element_type=F32)
                logits = logits * SCALE + bias
                m = jnp.max(logits, -1, keepdims=True)
                e = jnp.exp(logits - m)
                ssum = jnp.sum(e, -1, keepdims=True)
                num = jnp.dot(e.astype(BF16), v.astype(BF16), preferred_element_type=F32)
                if d == 1:
                    dst = pl.ds(start, BAND)
                else:
                    dst = pl.ds(start, BAND, stride=d)
                num_scr[g, dst, :] = num
                m_scr[g, dst, :] = jnp.broadcast_to(m, (BAND, HEAD_DIM))
                s_scr[g, dst, :] = jnp.broadcast_to(ssum, (BAND, HEAD_DIM))

    def merge(c, carry):
        sl = pl.ds(pl.multiple_of(c * BAND, BAND), BAND)
        ms = [m_scr[g, sl, :] for g in range(N_GROUPS)]
        m_all = jnp.maximum(jnp.maximum(ms[0], ms[1]), ms[2])
        coef = [jnp.exp(m - m_all) for m in ms]
        num = coef[0] * num_scr[0, sl, :]
        den = coef[0] * s_scr[0, sl, :]
        for g in range(1, N_GROUPS):
            num = num + coef[g] * num_scr[g, sl, :]
            den = den + coef[g] * s_scr[g, sl, :]
        o_ref[0, sl, :] = (num / den * _silu(gate_ref[0, sl, :])).astype(BF16)
        return carry

    jax.lax.fori_loop(0, ATTN_TOKENS // BAND, merge, 0)


def _b_attn(h, kv, band_bias):
    b, t, _ = h.shape
    tb = ATTN_TOKENS
    blk = (1, tb, HEAD_DIM)

    def cur(col0):
        return lambda bi, hi, i: (bi, i, col0 + hi)

    def halo_spec(d, col0):
        rows = BAND * d
        per = tb // rows
        return pl.BlockSpec((1, rows, HEAD_DIM), lambda bi, hi, i: (bi, jnp.maximum(i * per - 1, 0), col0 + hi))

    hpg = HEADS_PER_GROUP
    in_specs = ([pl.BlockSpec(blk, cur(hpg * g)) for g in range(N_GROUPS)]
                + [pl.BlockSpec(blk, cur(hpg * g)) for g in range(N_GROUPS)]
                + [pl.BlockSpec(blk, cur(B_HEADS + hpg * g)) for g in range(N_GROUPS)]
                + [halo_spec(d, hpg * g) for g, (_, d) in enumerate(B_GROUPS)]
                + [halo_spec(d, B_HEADS + hpg * g) for g, (_, d) in enumerate(B_GROUPS)]
                + [pl.BlockSpec((N_GROUPS, 2, 1, BAND, 2 * BAND), lambda bi, hi, i: (0, 0, hi, 0, 0)),
                   pl.BlockSpec(blk, cur(B_HEADS))])
    return pl.pallas_call(
        _b_attn_kernel,
        grid=(b, hpg, t // tb),
        in_specs=in_specs,
        out_specs=pl.BlockSpec(blk, lambda bi, hi, i: (bi, i, hi)),
        out_shape=jax.ShapeDtypeStruct((b, t, B_OUT), BF16),
        scratch_shapes=[pltpu.VMEM((N_GROUPS, tb, HEAD_DIM), F32)] * 3,
        compiler_params=_params("parallel", "parallel", "arbitrary"),
        name="b_attn",
    )(h, h, h, kv, kv, kv, kv, kv, kv, kv, kv, kv, kv, kv, kv, band_bias, h)


def _b_out_kernel(o_ref, qm_ref, gm_ref, mem_ref, w_ref, x_ref, g_ref, b_ref, y_ref, *rest, nb, tt):
    yb_ref, wb_ref = rest if len(rest) == 2 else (None, rest[0])

    @pl.when((pl.program_id(0) == 0) & (pl.program_id(1) == 0))
    def _():
        wb_ref[...] = w_ref[...].astype(BF16)

    stack = lambda parts: parts[0] if nb == 1 else jnp.concatenate(parts, 0)
    mbranch = stack([_mem_attn(qm_ref[bi].astype(BF16), mem_ref, bi) * _silu(gm_ref[bi]) for bi in range(nb)])
    if nb == 1:
        obranch = o_ref[0]
    else:
        obranch = jnp.concatenate([o_ref[bi].astype(F32) for bi in range(nb)], 0).astype(BF16)
    y = jnp.dot(obranch, wb_ref[0:B_OUT, :], preferred_element_type=F32)
    y = y + jnp.dot(mbranch.astype(BF16), wb_ref[B_OUT:B_OUT + MEM_WIDTH, :], preferred_element_type=F32)
    x = stack([x_ref[bi] for bi in range(nb)])
    out = _layernorm(ALPHA * x + y, g_ref[...], b_ref[...])
    for bi in range(nb):
        y_ref[bi] = out[bi * tt:(bi + 1) * tt]
        if yb_ref is not None:
            yb_ref[bi] = out[bi * tt:(bi + 1) * tt].astype(BF16)


def _b_out(o, h, mem, layer, w, wl, x, g, b, emit_bf16):
    bsz, t, d = x.shape
    tt = min(t, 512)
    nb = max(1, min(bsz, 64 // tt))
    kk = w.shape[1]
    qcol = (MIX_WIDTH + B_OUT) // MEM_WIDTH
    row = pl.BlockSpec((nb, tt, d), lambda bi, i: (bi, i, 0))
    out_dtypes = [F32, BF16] if emit_bf16 else [F32]
    res = pl.pallas_call(
        functools.partial(_b_out_kernel, nb=nb, tt=tt),
        grid=(bsz // nb, t // tt),
        in_specs=[pl.BlockSpec((nb, tt, B_OUT), lambda bi, i: (bi, i, 0)),
                  pl.BlockSpec((nb, tt, MEM_WIDTH), lambda bi, i: (bi, i, qcol)),
                  pl.BlockSpec((nb, tt, MEM_WIDTH), lambda bi, i: (bi, i, qcol + 1)),
                  pl.BlockSpec((None, nb, N_MEM * KV_SLOTS, HEAD_DIM), lambda bi, i: (layer, bi, 0, 0)),
                  pl.BlockSpec((None, kk, d), lambda bi, i: (wl, 0, 0), pipeline_mode=pl.Buffered(1)),
                  row,
                  pl.BlockSpec((1, d), lambda bi, i: (0, 0)),
                  pl.BlockSpec((1, d), lambda bi, i: (0, 0))],
        out_specs=[row] * len(out_dtypes),
        out_shape=[jax.ShapeDtypeStruct((bsz, t, d), dt) for dt in out_dtypes],
        scratch_shapes=[pltpu.VMEM((kk, d), BF16)],
        compiler_params=_params("arbitrary", "arbitrary"),
        name="b_out",
    )(o, h, h, mem, w, x, g.reshape(1, d), b.reshape(1, d))
    return res if emit_bf16 else (res[0], None)


def _samp_attn_kernel(*refs, nq, widths, emit_state):
    ng = N_GROUPS
    q_ref, gate_ref = refs[0], refs[1]
    st_refs, new_refs = refs[2:2 + ng], refs[2 + ng:2 + 2 * ng]
    bs_refs, bn_refs = refs[2 + 2 * ng:2 + 3 * ng], refs[2 + 3 * ng:2 + 4 * ng]
    o_ref = refs[2 + 4 * ng]
    dn = (((1,), (1,)), ((), ()))
    for h in range(HEADS_PER_GROUP):
        ms, sums, nums = [], [], []
        for g in range(ng):
            lo = (g * HEADS_PER_GROUP + h) * HEAD_DIM
            q = q_ref[0, :, lo:lo + HEAD_DIM].astype(BF16)
            ks = _kv_rows(st_refs[g], (0,), h, widths[g]).astype(BF16)
            vs = _kv_rows(st_refs[g], (0,), HEADS_PER_GROUP + h, widths[g]).astype(BF16)
            kn = _kv_rows(new_refs[g], (0,), h, nq).astype(BF16)
            vn = _kv_rows(new_refs[g], (0,), HEADS_PER_GROUP + h, nq).astype(BF16)
            ls = jax.lax.dot_general(q, ks, dn, preferred_element_type=F32) * SCALE + bs_refs[g][h]
            ln = jax.lax.dot_general(q, kn, dn, preferred_element_type=F32) * SCALE + bn_refs[g][h]
            m = jnp.maximum(jnp.max(ls, -1, keepdims=True), jnp.max(ln, -1, keepdims=True))
            es = jnp.exp(ls - m)
            en = jnp.exp(ln - m)
            ms.append(m)
            sums.append(jnp.sum(es, -1, keepdims=True) + jnp.sum(en, -1, keepdims=True))
            nums.append(jnp.dot(es.astype(BF16), vs, preferred_element_type=F32)
                        + jnp.dot(en.astype(BF16), vn, preferred_element_type=F32))
        m_all = jnp.maximum(jnp.maximum(ms[0], ms[1]), ms[2])
        coef = [jnp.exp(m - m_all) for m in ms]
        num = coef[0] * nums[0]
        den = coef[0] * sums[0]
        for g in range(1, ng):
            num = num + coef[g] * nums[g]
            den = den + coef[g] * sums[g]
        lo = h * HEAD_DIM
        o_ref[0, :, lo:lo + HEAD_DIM] = (num / den * _silu(gate_ref[0, :, lo:lo + HEAD_DIM])).astype(BF16)

    if emit_state:
        for g in range(ng):
            out = refs[3 + 4 * ng + g]
            rows, shift = widths[g] * KV_SLOTS, nq * KV_SLOTS
            out[0, 0:rows - shift, :] = st_refs[g][0, shift:rows, :]
            out[0, rows - shift:rows, :] = new_refs[g][0]


def _samp_attn(h, states, news, bias_state, bias_new, emit_state):
    b, nq, _ = h.shape
    widths = tuple(s.shape[1] // KV_SLOTS for s in states)
    whole = lambda a: pl.BlockSpec((1,) + a.shape[1:], lambda i: (i, 0, 0))
    const = lambda a: pl.BlockSpec(a.shape, lambda i: (0, 0, 0))
    out_spec = pl.BlockSpec((1, nq, GROUP_WIDTH), lambda i: (i, 0, 0))
    out_shape = jax.ShapeDtypeStruct((b, nq, GROUP_WIDTH), BF16)
    res = pl.pallas_call(
        functools.partial(_samp_attn_kernel, nq=nq, widths=widths, emit_state=emit_state),
        grid=(b,),
        in_specs=([pl.BlockSpec((1, nq, MIX_WIDTH), lambda i: (i, 0, 0)),
                   pl.BlockSpec((1, nq, GROUP_WIDTH), lambda i: (i, 0, MIX_WIDTH // GROUP_WIDTH))]
                  + [whole(s) for s in states] + [whole(n) for n in news]
                  + [const(a) for a in bias_state] + [const(a) for a in bias_new]),
        out_specs=[out_spec] + ([whole(s) for s in states] if emit_state else []),
        out_shape=[out_shape] + ([jax.ShapeDtypeStruct(s.shape, s.dtype) for s in states] if emit_state else []),
        compiler_params=_params("arbitrary"),
        name="samp_attn",
    )(h, h, *states, *news, *bias_state, *bias_new)
    return res[0], list(res[1:])


def _t5_bucket(dist):
    max_exact = REL_BUCKETS // 2
    safe = jnp.maximum(dist, 1).astype(F32)
    large = max_exact + (jnp.log(safe / max_exact) / math.log(REL_MAX_DIST / max_exact)
                         * (REL_BUCKETS - max_exact)).astype(jnp.int32)
    large = jnp.minimum(large, REL_BUCKETS - 1)
    return jnp.where(dist < max_exact, dist, large)


def _group_bias(rel_bias, g):
    w, d = B_GROUPS[g]
    dist = d * jnp.arange(w // d + 1, dtype=jnp.int32)
    tab = rel_bias[_t5_bucket(dist)]
    return tab[:, g * HEADS_PER_GROUP:(g + 1) * HEADS_PER_GROUP].T.astype(F32)


def _toeplitz(vec, rows, cols):
    hh, p = vec.shape
    flat = jnp.tile(vec, (1, rows))[:, :rows * (p - 1)]
    return flat.reshape(hh, rows, p - 1)[:, :, :cols]


def _band_bias(bias):
    period = 3 * BAND
    vec = jnp.full((bias.shape[0], period), NEG_INF, F32)
    vec = jax.lax.dynamic_update_slice(vec, bias[:, ::-1], (0, 0))
    mat = _toeplitz(vec, BAND, 2 * BAND)
    c = jnp.arange(2 * BAND, dtype=jnp.int32)[None, None, :]
    first = jnp.where(c >= BAND, mat, NEG_INF)
    return jnp.stack([first, mat])


def _sample_bias(bias, w, d, nq):
    hh = bias.shape[0]
    neg = jnp.full_like(bias, NEG_INF)
    pad = jnp.full((hh, nq), NEG_INF, F32)
    rev_dist = jnp.stack([neg] * (d - 1) + [bias[:, ::-1]], -1).reshape(hh, -1)
    bs = _toeplitz(jnp.concatenate([rev_dist[:, d - 1:d - 1 + w], pad], 1), nq, w)
    bn = _toeplitz(jnp.concatenate([bias[:, 0:1], pad, rev_dist[:, w + d - nq:w + d - 1]], 1), nq, nq)
    return bs, bn


def _trunk(x, conv_prev, mem_kv, kv_state, a_w_in, a_w_dw, a_b_dw, a_cn_g, a_cn_b, a_w_out,
           b_w_in, b_w_out, w_kv_shared, rel_bias, ln_g, ln_b):
    b, t, d_model = x.shape
    m = b * t
    x2 = x.reshape(m, d_model)
    xb = x2
    new_conv = []
    for l in range(N_A_LAYERS):
        u, gates = _a_proj(xb, a_w_in, l)
        cm, nc = _a_mix(u.reshape(b, t, -1), gates.reshape(b, t, -1), conv_prev[l], mem_kv, l,
                        a_w_dw[l], a_b_dw[l], a_cn_g[l], a_cn_b[l])
        new_conv.append(nc)
        x2, xb = _out_ln(cm.reshape(m, d_model), a_w_out, l, x2, ln_g[l], ln_b[l])

    kv = _mm(xb, w_kv_shared[None], 0).reshape(b, t, 2 * MIX_WIDTH)
    biases = [_group_bias(rel_bias, g) for g in range(N_GROUPS)]
    kv_new = [_kv_window(kv, g, min(w, t)) for g, (w, _) in enumerate(B_GROUPS)]
    if kv_state is None:
        new_bufs = kv_new
        band_bias = jnp.stack([_band_bias(bi) for bi in biases])
    else:
        bias_state, bias_new = zip(*[_sample_bias(bi, w, d, t) for bi, (w, d) in zip(biases, B_GROUPS)])

    x3 = x2.reshape(b, t, d_model)
    for i in range(DEPTH - N_A_LAYERS):
        l = N_A_LAYERS + i
        h = _mm(xb.reshape(m, d_model), b_w_in, i).reshape(b, t, -1)
        if kv_state is None:
            o = _b_attn(h, kv, band_bias)
        else:
            o, updated = _samp_attn(h, kv_state, kv_new, bias_state, bias_new, emit_state=(i == 0))
            if i == 0:
                new_bufs = updated
        x3, xb = _b_out(o, h, mem_kv, l, b_w_out, i, x3, ln_g[l], ln_b[l], emit_bf16=l < DEPTH - 1)
    new_bufs = [nb.reshape(b, -1, 2, HEADS_PER_GROUP, HEAD_DIM) for nb in new_bufs]
    return x3, jnp.stack(new_conv), new_bufs


def kernel(x_prompt, x_sample, state_conv, state_kv_g0, state_kv_g1, state_kv_g2, cache_mem_kv, mem_prompt,
           a_w_in, a_w_dw, a_b_dw, a_cn_g, a_cn_b, a_w_out, b_w_in, b_w_out, w_kv_shared, w_mem_kv,
           rel_bias, ln_g, ln_b):
    bp = x_prompt.shape[0]
    bs = x_sample.shape[0]
    mem_p = _mem_proj(mem_prompt.reshape(bp * N_MEM, D_MODEL), w_mem_kv).reshape(
        DEPTH, bp, N_MEM * KV_SLOTS, HEAD_DIM)
    new_mem_kv = mem_p.reshape(DEPTH, bp, N_MEM, 2, MEM_HEADS, HEAD_DIM)
    conv_zero = jnp.zeros((N_A_LAYERS, bp, CONV_WIDTH - 1, CONV_CH), x_prompt.dtype)
    weights = (a_w_in, a_w_dw, a_b_dw, a_cn_g, a_cn_b, a_w_out, b_w_in, b_w_out, w_kv_shared, rel_bias, ln_g, ln_b)

    y_p, conv_p, bufs_p = _trunk(x_prompt, conv_zero, mem_p, None, *weights)
    kv_state = [s.reshape(bs, s.shape[1] * KV_SLOTS, HEAD_DIM) for s in (state_kv_g0, state_kv_g1, state_kv_g2)]
    mem_s = cache_mem_kv.reshape(DEPTH, bs, N_MEM * KV_SLOTS, HEAD_DIM)
    y_s, conv_s, bufs_s = _trunk(x_sample, state_conv, mem_s, kv_state, *weights)

    return (y_p, y_s, conv_p, conv_s, bufs_p[0], bufs_s[0], bufs_p[1], bufs_s[1], bufs_p[2], bufs_s[2], new_mem_kv)
```

```python
import functools
import math

import jax
import jax.numpy as jnp
from jax.experimental import pallas as pl
from jax.experimental.pallas import tpu as pltpu

D_MODEL = 2048
DEPTH = 4
HEAD_DIM = 128
MEM_HEADS = 4
MEM_WIDTH = MEM_HEADS * HEAD_DIM
N_MEM = 256
MIX_WIDTH = D_MODEL - MEM_WIDTH
N_A_LAYERS = DEPTH // 2
CONV_CH = MIX_WIDTH
CONV_WIDTH = 31
B_GROUPS = ((128, 1), (512, 4), (2048, 16))
N_GROUPS = len(B_GROUPS)
HEADS_PER_GROUP = 4
B_HEADS = N_GROUPS * HEADS_PER_GROUP
GROUP_WIDTH = HEADS_PER_GROUP * HEAD_DIM
KV_SLOTS = 2 * HEADS_PER_GROUP
B_OUT = GROUP_WIDTH
REL_BUCKETS = 32
REL_MAX_DIST = 2048
BAND = 128
ATTN_TOKENS = BAND * max(d for _, d in B_GROUPS)
LN_EPS = 1e-5
ALPHA = (2 * DEPTH) ** 0.25
NEG_INF = -1e30
SCALE = HEAD_DIM ** -0.5

HALO = 32
VMEM_LIMIT = 56 * 1024 * 1024
BF16 = jnp.bfloat16
F32 = jnp.float32


def _params(*sem):
    return pltpu.CompilerParams(dimension_semantics=sem, vmem_limit_bytes=VMEM_LIMIT)


def _sigmoid(x):
    return 1.0 / (1.0 + jnp.exp(-x))


def _silu(x):
    return x * _sigmoid(x)


def _layernorm(z, g, b):
    mu = jnp.mean(z, -1, keepdims=True)
    zc = z - mu
    var = jnp.mean(zc * zc, -1, keepdims=True)
    return zc * jax.lax.rsqrt(var + LN_EPS) * g + b


DOT_CHUNK = 512


PROJ_ROWS = 2048


def _project(x_ref, w_ref, emit):
    wb = w_ref[...].astype(BF16)
    step = min(x_ref.shape[0], DOT_CHUNK)
    for r in range(0, x_ref.shape[0], step):
        emit(slice(r, r + step), jnp.dot(x_ref[r:r + step, :], wb, preferred_element_type=F32))


def _mm_kernel(x_ref, w_ref, o_ref):
    def emit(rows, y):
        o_ref[rows, :] = y
    _project(x_ref, w_ref, emit)


def _mm(x, w, *, tn=512):
    m, k = x.shape
    n = w.shape[1]
    tm = min(m, PROJ_ROWS)
    return pl.pallas_call(
        _mm_kernel,
        grid=(m // tm, n // tn),
        in_specs=[pl.BlockSpec((tm, k), lambda i, j: (i, 0)),
                  pl.BlockSpec((k, tn), lambda i, j: (0, j))],
        out_specs=pl.BlockSpec((tm, tn), lambda i, j: (i, j)),
        out_shape=jax.ShapeDtypeStruct((m, n), F32),
        compiler_params=_params("arbitrary", "arbitrary"),
        name="mm",
    )(x, w)


B_GATE_STEP = MIX_WIDTH // GROUP_WIDTH


def _b_proj_kernel(x_ref, w_ref, q_ref, g_ref):
    j = pl.program_id(1)

    @pl.when(j < B_GATE_STEP)
    def _():
        def emit(rows, y):
            q_ref[rows, :] = y
        _project(x_ref, w_ref, emit)

    @pl.when(j == B_GATE_STEP + 1)
    def _():
        def emit(rows, y):
            g_ref[rows, :] = y.astype(BF16)
        _project(x_ref, w_ref, emit)

    @pl.when((j == B_GATE_STEP) | (j == B_GATE_STEP + 2))
    def _():
        def emit(rows, y):
            g_ref[rows, :] = _silu(y).astype(BF16)
        _project(x_ref, w_ref, emit)


def _b_proj(x, w, layer):
    m, k = x.shape
    tm = min(m, PROJ_ROWS)
    tn = GROUP_WIDTH
    return pl.pallas_call(
        _b_proj_kernel,
        grid=(m // tm, B_GATE_STEP + 3),
        in_specs=[pl.BlockSpec((tm, k), lambda i, j: (i, 0)),
                  pl.BlockSpec((None, k, tn), lambda i, j: (layer, 0, j))],
        out_specs=[pl.BlockSpec((tm, tn), lambda i, j: (i, jnp.minimum(j, B_GATE_STEP - 1))),
                   pl.BlockSpec((tm, tn), lambda i, j: (i, jnp.maximum(j - B_GATE_STEP, 0)))],
        out_shape=[jax.ShapeDtypeStruct((m, MIX_WIDTH), F32),
                   jax.ShapeDtypeStruct((m, B_OUT + 2 * MEM_WIDTH), BF16)],
        compiler_params=_params("arbitrary", "arbitrary"),
        name="b_proj",
    )(x, w)


def _store_kv_slots(o_ref, lead, slot0, vals):
    rows = vals.shape[0]
    for j in range(vals.shape[1] // HEAD_DIM):
        o_ref[lead + (pl.ds(slot0 + j, rows, stride=KV_SLOTS), slice(None))] = vals[:, j * HEAD_DIM:(j + 1) * HEAD_DIM]


def _mem_proj_kernel(x_ref, w_ref, o_ref, xb_ref):
    @pl.when(pl.program_id(0) == 0)
    def _():
        xb_ref[...] = x_ref[...].astype(BF16)

    _store_kv_slots(o_ref, (), 0, jnp.dot(xb_ref[...], w_ref[...].astype(BF16), preferred_element_type=F32))


def _mem_proj(x, w):
    m, k = x.shape
    nl, _, n = w.shape
    return pl.pallas_call(
        _mem_proj_kernel,
        grid=(nl,),
        in_specs=[pl.BlockSpec((m, k), lambda l: (0, 0)),
                  pl.BlockSpec((None, k, n), lambda l: (l, 0, 0))],
        out_specs=pl.BlockSpec((None, m * KV_SLOTS, HEAD_DIM), lambda l: (l, 0, 0)),
        out_shape=jax.ShapeDtypeStruct((nl, m * KV_SLOTS, HEAD_DIM), F32),
        scratch_shapes=[pltpu.VMEM((m, k), BF16)],
        compiler_params=_params("arbitrary"),
        name="mem_proj",
    )(x, w)


def _kv_window_kernel(k_ref, v_ref, o_ref):
    _store_kv_slots(o_ref, (0,), 0, k_ref[0])
    _store_kv_slots(o_ref, (0,), HEADS_PER_GROUP, v_ref[0])


def _kv_window(kv, g, w):
    b, t, _ = kv.shape
    rows = min(w, 512)
    first = (t - w) // rows
    return pl.pallas_call(
        _kv_window_kernel,
        grid=(b, w // rows),
        in_specs=[pl.BlockSpec((1, rows, GROUP_WIDTH), lambda bi, i: (bi, first + i, g)),
                  pl.BlockSpec((1, rows, GROUP_WIDTH), lambda bi, i: (bi, first + i, N_GROUPS + g))],
        out_specs=pl.BlockSpec((1, rows * KV_SLOTS, HEAD_DIM), lambda bi, i: (bi, i, 0)),
        out_shape=jax.ShapeDtypeStruct((b, w * KV_SLOTS, HEAD_DIM), F32),
        compiler_params=_params("arbitrary", "arbitrary"),
        name="kv_window",
    )(kv, kv)


A_TN = 512
N_GLU = CONV_CH // A_TN
QM_STEP = 3 * N_GLU


def _a_proj_kernel(x_ref, w_ref, u_ref, g_ref, ga_ref):
    j = pl.program_id(1)
    project = functools.partial(_project, x_ref, w_ref)
    is_glu = j < 2 * N_GLU

    @pl.when(is_glu & (j % 2 == 0))
    def _():
        def emit(rows, y):
            ga_ref[rows, :] = y
        project(emit)

    @pl.when(is_glu & (j % 2 == 1))
    def _():
        def emit(rows, y):
            u_ref[rows, :] = ga_ref[rows, :] * _sigmoid(y)
        project(emit)

    @pl.when(j == QM_STEP)
    def _():
        def emit(rows, y):
            g_ref[rows, :] = y.astype(BF16)
        project(emit)

    @pl.when(jnp.logical_not(is_glu) & (j != QM_STEP))
    def _():
        def emit(rows, y):
            g_ref[rows, :] = _silu(y).astype(BF16)
        project(emit)


def _a_proj(x, w, layer):
    m, k = x.shape
    tm = min(m, PROJ_ROWS)
    n_steps = 3 * N_GLU + 2 * MEM_WIDTH // A_TN

    def w_col(j):
        return jnp.where(j < 2 * N_GLU, (j % 2) * N_GLU + j // 2, j)

    return pl.pallas_call(
        _a_proj_kernel,
        grid=(m // tm, n_steps),
        in_specs=[pl.BlockSpec((tm, k), lambda i, j: (i, 0)),
                  pl.BlockSpec((None, k, A_TN), lambda i, j: (layer, 0, w_col(j)))],
        out_specs=[pl.BlockSpec((tm, A_TN), lambda i, j: (i, jnp.minimum(j // 2, N_GLU - 1))),
                   pl.BlockSpec((tm, A_TN), lambda i, j: (i, jnp.maximum(j - 2 * N_GLU, 0)))],
        out_shape=[jax.ShapeDtypeStruct((m, CONV_CH), F32),
                   jax.ShapeDtypeStruct((m, CONV_CH + 2 * MEM_WIDTH), BF16)],
        scratch_shapes=[pltpu.VMEM((tm, A_TN), F32)],
        compiler_params=_params("arbitrary", "arbitrary"),
        name="a_proj",
    )(x, w)


def _out_ln_kernel(cm_ref, w_ref, x_ref, g_ref, b_ref, o_ref, ob_ref, wb_ref):
    @pl.when(pl.program_id(0) == 0)
    def _():
        wb_ref[...] = w_ref[...].astype(BF16)

    cm = cm_ref[...]
    for c in range(0, o_ref.shape[1], DOT_CHUNK):
        cols = slice(c, c + DOT_CHUNK)
        o_ref[:, cols] = ALPHA * x_ref[:, cols] + jnp.dot(cm, wb_ref[:, cols], preferred_element_type=F32)
    out = _layernorm(o_ref[...], g_ref[...], b_ref[...])
    o_ref[...] = out
    ob_ref[...] = out.astype(BF16)


def _out_ln(cm, w, layer, x, g, b):
    m, kk = cm.shape
    d = w.shape[2]
    tm = min(m, 512)
    row = pl.BlockSpec((tm, d), lambda i: (i, 0))
    return pl.pallas_call(
        _out_ln_kernel,
        grid=(m // tm,),
        in_specs=[pl.BlockSpec((tm, kk), lambda i: (i, 0)),
                  pl.BlockSpec((None, kk, d), lambda i: (layer, 0, 0), pipeline_mode=pl.Buffered(1)),
                  row,
                  pl.BlockSpec((1, d), lambda i: (0, 0)),
                  pl.BlockSpec((1, d), lambda i: (0, 0))],
        out_specs=[row, row],
        out_shape=[jax.ShapeDtypeStruct((m, d), F32), jax.ShapeDtypeStruct((m, d), BF16)],
        scratch_shapes=[pltpu.VMEM((kk, d), BF16)],
        compiler_params=_params("arbitrary"),
        name="out_ln",
    )(cm, w, x, g.reshape(1, d), b.reshape(1, d))


def _kv_rows(ref, lead, slot, n):
    return ref[lead + (pl.ds(slot, n, stride=KV_SLOTS), slice(None))]


def _mem_attn(qm, mem_ref, entry=0):
    outs = []
    for h in range(MEM_HEADS):
        lo = h * HEAD_DIM
        q = qm[:, lo:lo + HEAD_DIM]
        k = _kv_rows(mem_ref, (entry,), h, N_MEM).astype(BF16)
        v = _kv_rows(mem_ref, (entry,), MEM_HEADS + h, N_MEM).astype(BF16)
        s = jax.lax.dot_general(q, k, (((1,), (1,)), ((), ())), preferred_element_type=F32) * SCALE
        s = s - jnp.max(s, -1, keepdims=True)
        e = jnp.exp(s)
        p = e / jnp.sum(e, -1, keepdims=True)
        outs.append(jnp.dot(p.astype(BF16), v, preferred_element_type=F32))
    return jnp.concatenate(outs, -1)


def _a_mix_kernel(*refs, tt, tr, has_halo):
    if has_halo:
        (u_ref, sgate, qm, sgm, u_prev, cprev, mem, wdw, bdw, cng, cnb,
         cm_out, nc_out, ext, wrep, conv) = refs
    else:
        (u_ref, sgate, qm, sgm, cprev, mem, wdw, bdw, cng, cnb,
         cm_out, nc_out, ext, wrep, conv) = refs
    i = pl.program_id(1)
    n_blk = CONV_CH // HEAD_DIM

    @pl.when((pl.program_id(0) == 0) & (i == 0))
    def _():
        for k in range(CONV_WIDTH):
            wrep[k] = jnp.broadcast_to(wdw[k:k + 1, :], (8, CONV_CH))

    for c in range(n_blk):
        lanes = slice(c * HEAD_DIM, (c + 1) * HEAD_DIM)
        if has_halo:
            ext[c, 0:HALO, :] = jnp.where(i == 0, cprev[0, :, lanes], u_prev[0, :, lanes])
        else:
            ext[c, 0:HALO, :] = cprev[0, :, lanes]
        ext[c, HALO:HALO + tt, :] = u_ref[0, :, lanes]
        nc_out[0, :, lanes] = ext[c, HALO + tt - (CONV_WIDTH - 1):HALO + tt, :]

    off0 = HALO - (CONV_WIDTH - 1)
    for c in range(n_blk):
        lanes = slice(c * HEAD_DIM, (c + 1) * HEAD_DIM)

        def body(rc, carry, c=c, lanes=lanes):
            r0 = pl.multiple_of(rc * tr, 8)
            acc = jnp.zeros((tr // 8, 8, HEAD_DIM), F32)
            for k in range(CONV_WIDTH):
                win = ext[c, pl.ds(r0 + k + off0, tr), :].reshape(tr // 8, 8, HEAD_DIM)
                acc = acc + win * wrep[k, :, lanes][None]
            conv[pl.ds(r0, tr), lanes] = acc.reshape(tr, HEAD_DIM) + bdw[:, lanes]
            return carry

        jax.lax.fori_loop(0, tt // tr, body, 0)

    zn = _layernorm(conv[...], cng[...], cnb[...])
    cm_out[0, :, 0:CONV_CH] = (_silu(zn) * sgate[0].astype(F32)).astype(BF16)
    cm_out[0, :, CONV_CH:D_MODEL] = (_mem_attn(qm[0], mem) * sgm[0].astype(F32)).astype(BF16)


def _a_mix(u, gates, conv_prev, mem, layer, w_dw, b_dw, cn_g, cn_b):
    b, t, _ = u.shape
    tt = min(t, 512)
    tr = min(tt, 128)
    nt = t // tt
    has_halo = nt > 1
    cprev = jnp.pad(conv_prev, ((0, 0), (HALO - (CONV_WIDTH - 1), 0), (0, 0)))
    c3 = CONV_CH
    qcol = CONV_CH // MEM_WIDTH

    in_specs = [pl.BlockSpec((1, tt, c3), lambda bi, i: (bi, i, 0)),
                pl.BlockSpec((1, tt, c3), lambda bi, i: (bi, i, 0)),
                pl.BlockSpec((1, tt, MEM_WIDTH), lambda bi, i: (bi, i, qcol)),
                pl.BlockSpec((1, tt, MEM_WIDTH), lambda bi, i: (bi, i, qcol + 1))]
    args = [u, gates, gates, gates]
    if has_halo:
        per = tt // HALO
        in_specs += [pl.BlockSpec((1, HALO, c3), lambda bi, i: (bi, jnp.maximum(i * per - 1, 0), 0))]
        args += [u]
    in_specs += [pl.BlockSpec((1, HALO, c3), lambda bi, i: (bi, 0, 0)),
                 pl.BlockSpec((None, 1, N_MEM * KV_SLOTS, HEAD_DIM), lambda bi, i: (layer, bi, 0, 0)),
                 pl.BlockSpec((CONV_WIDTH, c3), lambda bi, i: (0, 0)),
                 pl.BlockSpec((1, c3), lambda bi, i: (0, 0)),
                 pl.BlockSpec((1, c3), lambda bi, i: (0, 0)),
                 pl.BlockSpec((1, c3), lambda bi, i: (0, 0))]
    args += [cprev, mem, w_dw, b_dw.reshape(1, c3), cn_g.reshape(1, c3), cn_b.reshape(1, c3)]

    return pl.pallas_call(
        functools.partial(_a_mix_kernel, tt=tt, tr=tr, has_halo=has_halo),
        grid=(b, nt),
        in_specs=in_specs,
        out_specs=[pl.BlockSpec((1, tt, D_MODEL), lambda bi, i: (bi, i, 0)),
                   pl.BlockSpec((1, CONV_WIDTH - 1, c3), lambda bi, i: (bi, 0, 0))],
        out_shape=[jax.ShapeDtypeStruct((b, t, D_MODEL), BF16),
                   jax.ShapeDtypeStruct((b, CONV_WIDTH - 1, c3), F32)],
        scratch_shapes=[pltpu.VMEM((c3 // HEAD_DIM, tt + HALO, HEAD_DIM), F32),
                        pltpu.VMEM((CONV_WIDTH, 8, c3), F32),
                        pltpu.VMEM((tt, c3), F32)],
        compiler_params=_params("arbitrary", "arbitrary"),
        name="a_mix",
    )(*args)


def _rows(ref, start, size, stride):
    if stride == 1:
        return ref[0, start:start + size, :]
    return ref[0, pl.ds(start, size, stride=stride), :]


def _b_attn_kernel(q0, q1, q2, k0, k1, k2, v0, v1, v2, hk0, hk1, hk2, hv0, hv1, hv2, bias_ref, gate_ref,
                   o_ref, num_scr, m_scr, s_scr):
    first = jnp.minimum(pl.program_id(2), 1)
    qs, ks, vs = (q0, q1, q2), (k0, k1, k2), (v0, v1, v2)
    hks, hvs = (hk0, hk1, hk2), (hv0, hv1, hv2)
    dn = (((1,), (1,)), ((), ()))
    for g, (_, d) in enumerate(B_GROUPS):
        for r in range(d):
            for s in range(ATTN_TOKENS // (BAND * d)):
                start = r + d * BAND * s
                q = _rows(qs[g], start, BAND, d)
                if s == 0:
                    k = jnp.concatenate([_rows(hks[g], r, BAND, d), _rows(ks[g], r, BAND, d)], 0)
                    v = jnp.concatenate([_rows(hvs[g], r, BAND, d), _rows(vs[g], r, BAND, d)], 0)
                    bias = bias_ref[g, first, 0]
                else:
                    k = _rows(ks[g], start - d * BAND, 2 * BAND, d)
                    v = _rows(vs[g], start - d * BAND, 2 * BAND, d)
                    bias = bias_ref[g, 1, 0]
                logits = jax.lax.dot_general(q.astype(BF16), k.astype(BF16), dn, preferred_element_type=F32)
                logits = logits * SCALE + bias
                m = jnp.max(logits, -1, keepdims=True)
                e = jnp.exp(logits - m)
                ssum = jnp.sum(e, -1, keepdims=True)
                num = jnp.dot(e.astype(BF16), v.astype(BF16), preferred_element_type=F32)
                if d == 1:
                    dst = pl.ds(start, BAND)
                else:
                    dst = pl.ds(start, BAND, stride=d)
                num_scr[g, dst, :] = num
                m_scr[g, dst, :] = jnp.broadcast_to(m, (BAND, HEAD_DIM))
                s_scr[g, dst, :] = jnp.broadcast_to(ssum, (BAND, HEAD_DIM))

    def merge(c, carry):
        sl = pl.ds(pl.multiple_of(c * BAND, BAND), BAND)
        ms = [m_scr[g, sl, :] for g in range(N_GROUPS)]
        m_all = jnp.maximum(jnp.maximum(ms[0], ms[1]), ms[2])
        coef = [jnp.exp(m - m_all) for m in ms]
        num = coef[0] * num_scr[0, sl, :]
        den = coef[0] * s_scr[0, sl, :]
        for g in range(1, N_GROUPS):
            num = num + coef[g] * num_scr[g, sl, :]
            den = den + coef[g] * s_scr[g, sl, :]
        o_ref[0, sl, :] = (num / den * gate_ref[0, sl, :].astype(F32)).astype(BF16)
        return carry

    jax.lax.fori_loop(0, ATTN_TOKENS // BAND, merge, 0)


def _b_attn(q, gates, kv, band_bias):
    b, t, _ = q.shape
    tb = ATTN_TOKENS
    blk = (1, tb, HEAD_DIM)

    def cur(col0):
        return lambda bi, hi, i: (bi, i, col0 + hi)

    def halo_spec(d, col0):
        rows = BAND * d
        per = tb // rows
        return pl.BlockSpec((1, rows, HEAD_DIM), lambda bi, hi, i: (bi, jnp.maximum(i * per - 1, 0), col0 + hi))

    hpg = HEADS_PER_GROUP
    in_specs = ([pl.BlockSpec(blk, cur(hpg * g)) for g in range(N_GROUPS)]
                + [pl.BlockSpec(blk, cur(hpg * g)) for g in range(N_GROUPS)]
                + [pl.BlockSpec(blk, cur(B_HEADS + hpg * g)) for g in range(N_GROUPS)]
                + [halo_spec(d, hpg * g) for g, (_, d) in enumerate(B_GROUPS)]
                + [halo_spec(d, B_HEADS + hpg * g) for g, (_, d) in enumerate(B_GROUPS)]
                + [pl.BlockSpec((N_GROUPS, 2, 1, BAND, 2 * BAND), lambda bi, hi, i: (0, 0, hi, 0, 0)),
                   pl.BlockSpec(blk, cur(0))])
    return pl.pallas_call(
        _b_attn_kernel,
        grid=(b, hpg, t // tb),
        in_specs=in_specs,
        out_specs=pl.BlockSpec(blk, lambda bi, hi, i: (bi, i, hi)),
        out_shape=jax.ShapeDtypeStruct((b, t, B_OUT), BF16),
        scratch_shapes=[pltpu.VMEM((N_GROUPS, tb, HEAD_DIM), F32)] * 3,
        compiler_params=_params("parallel", "parallel", "arbitrary"),
        name="b_attn",
    )(q, q, q, kv, kv, kv, kv, kv, kv, kv, kv, kv, kv, kv, kv, band_bias, gates)


def _b_out_kernel(o_ref, qm_ref, gm_ref, mem_ref, w_ref, x_ref, g_ref, b_ref, y_ref, *rest, nb, tt):
    yb_ref, wb_ref = rest if len(rest) == 2 else (None, rest[0])

    @pl.when((pl.program_id(0) == 0) & (pl.program_id(1) == 0))
    def _():
        wb_ref[...] = w_ref[...].astype(BF16)

    stack = lambda parts: parts[0] if nb == 1 else jnp.concatenate(parts, 0)
    mbranch = stack([_mem_attn(qm_ref[bi], mem_ref, bi) * gm_ref[bi].astype(F32) for bi in range(nb)])
    if nb == 1:
        obranch = o_ref[0]
    else:
        obranch = jnp.concatenate([o_ref[bi].astype(F32) for bi in range(nb)], 0).astype(BF16)
    y = jnp.dot(obranch, wb_ref[0:B_OUT, :], preferred_element_type=F32)
    y = y + jnp.dot(mbranch.astype(BF16), wb_ref[B_OUT:B_OUT + MEM_WIDTH, :], preferred_element_type=F32)
    x = stack([x_ref[bi] for bi in range(nb)])
    out = _layernorm(ALPHA * x + y, g_ref[...], b_ref[...])
    for bi in range(nb):
        y_ref[bi] = out[bi * tt:(bi + 1) * tt]
        if yb_ref is not None:
            yb_ref[bi] = out[bi * tt:(bi + 1) * tt].astype(BF16)


def _b_out(o, gates, mem, layer, w, wl, x, g, b, emit_bf16):
    bsz, t, d = x.shape
    tt = min(t, 512)
    nb = max(1, min(bsz, 64 // tt))
    kk = w.shape[1]
    qcol = B_OUT // MEM_WIDTH
    row = pl.BlockSpec((nb, tt, d), lambda bi, i: (bi, i, 0))
    out_dtypes = [F32, BF16] if emit_bf16 else [F32]
    res = pl.pallas_call(
        functools.partial(_b_out_kernel, nb=nb, tt=tt),
        grid=(bsz // nb, t // tt),
        in_specs=[pl.BlockSpec((nb, tt, B_OUT), lambda bi, i: (bi, i, 0)),
                  pl.BlockSpec((nb, tt, MEM_WIDTH), lambda bi, i: (bi, i, qcol)),
                  pl.BlockSpec((nb, tt, MEM_WIDTH), lambda bi, i: (bi, i, qcol + 1)),
                  pl.BlockSpec((None, nb, N_MEM * KV_SLOTS, HEAD_DIM), lambda bi, i: (layer, bi, 0, 0)),
                  pl.BlockSpec((None, kk, d), lambda bi, i: (wl, 0, 0), pipeline_mode=pl.Buffered(1)),
                  row,
                  pl.BlockSpec((1, d), lambda bi, i: (0, 0)),
                  pl.BlockSpec((1, d), lambda bi, i: (0, 0))],
        out_specs=[row] * len(out_dtypes),
        out_shape=[jax.ShapeDtypeStruct((bsz, t, d), dt) for dt in out_dtypes],
        scratch_shapes=[pltpu.VMEM((kk, d), BF16)],
        compiler_params=_params("arbitrary", "arbitrary"),
        name="b_out",
    )(o, gates, gates, mem, w, x, g.reshape(1, d), b.reshape(1, d))
    return res if emit_bf16 else (res[0], None)


def _samp_attn_kernel(*refs, nq, widths, emit_state):
    ng = N_GROUPS
    q_ref, gate_ref = refs[0], refs[1]
    st_refs, new_refs = refs[2:2 + ng], refs[2 + ng:2 + 2 * ng]
    bs_refs, bn_refs = refs[2 + 2 * ng:2 + 3 * ng], refs[2 + 3 * ng:2 + 4 * ng]
    o_ref = refs[2 + 4 * ng]
    dn = (((1,), (1,)), ((), ()))
    for h in range(HEADS_PER_GROUP):
        ms, sums, nums = [], [], []
        for g in range(ng):
            lo = (g * HEADS_PER_GROUP + h) * HEAD_DIM
            q = q_ref[0, :, lo:lo + HEAD_DIM].astype(BF16)
            ks = _kv_rows(st_refs[g], (0,), h, widths[g]).astype(BF16)
            vs = _kv_rows(st_refs[g], (0,), HEADS_PER_GROUP + h, widths[g]).astype(BF16)
            kn = _kv_rows(new_refs[g], (0,), h, nq).astype(BF16)
            vn = _kv_rows(new_refs[g], (0,), HEADS_PER_GROUP + h, nq).astype(BF16)
            ls = jax.lax.dot_general(q, ks, dn, preferred_element_type=F32) * SCALE + bs_refs[g][h]
            ln = jax.lax.dot_general(q, kn, dn, preferred_element_type=F32) * SCALE + bn_refs[g][h]
            m = jnp.maximum(jnp.max(ls, -1, keepdims=True), jnp.max(ln, -1, keepdims=True))
            es = jnp.exp(ls - m)
            en = jnp.exp(ln - m)
            ms.append(m)
            sums.append(jnp.sum(es, -1, keepdims=True) + jnp.sum(en, -1, keepdims=True))
            nums.append(jnp.dot(es.astype(BF16), vs, preferred_element_type=F32)
                        + jnp.dot(en.astype(BF16), vn, preferred_element_type=F32))
        m_all = jnp.maximum(jnp.maximum(ms[0], ms[1]), ms[2])
        coef = [jnp.exp(m - m_all) for m in ms]
        num = coef[0] * nums[0]
        den = coef[0] * sums[0]
        for g in range(1, ng):
            num = num + coef[g] * nums[g]
            den = den + coef[g] * sums[g]
        lo = h * HEAD_DIM
        o_ref[0, :, lo:lo + HEAD_DIM] = (num / den * gate_ref[0, :, lo:lo + HEAD_DIM].astype(F32)).astype(BF16)

    if emit_state:
        for g in range(ng):
            out = refs[3 + 4 * ng + g]
            rows, shift = widths[g] * KV_SLOTS, nq * KV_SLOTS
            out[0, 0:rows - shift, :] = st_refs[g][0, shift:rows, :]
            out[0, rows - shift:rows, :] = new_refs[g][0]


def _samp_attn(q, gates, states, news, bias_state, bias_new, emit_state):
    b, nq, _ = q.shape
    widths = tuple(s.shape[1] // KV_SLOTS for s in states)
    whole = lambda a: pl.BlockSpec((1,) + a.shape[1:], lambda i: (i, 0, 0))
    const = lambda a: pl.BlockSpec(a.shape, lambda i: (0, 0, 0))
    out_spec = pl.BlockSpec((1, nq, GROUP_WIDTH), lambda i: (i, 0, 0))
    out_shape = jax.ShapeDtypeStruct((b, nq, GROUP_WIDTH), BF16)
    res = pl.pallas_call(
        functools.partial(_samp_attn_kernel, nq=nq, widths=widths, emit_state=emit_state),
        grid=(b,),
        in_specs=([pl.BlockSpec((1, nq, MIX_WIDTH), lambda i: (i, 0, 0)),
                   pl.BlockSpec((1, nq, GROUP_WIDTH), lambda i: (i, 0, 0))]
                  + [whole(s) for s in states] + [whole(n) for n in news]
                  + [const(a) for a in bias_state] + [const(a) for a in bias_new]),
        out_specs=[out_spec] + ([whole(s) for s in states] if emit_state else []),
        out_shape=[out_shape] + ([jax.ShapeDtypeStruct(s.shape, s.dtype) for s in states] if emit_state else []),
        compiler_params=_params("arbitrary"),
        name="samp_attn",
    )(q, gates, *states, *news, *bias_state, *bias_new)
    return res[0], list(res[1:])


def _t5_bucket(dist):
    max_exact = REL_BUCKETS // 2
    safe = jnp.maximum(dist, 1).astype(F32)
    large = max_exact + (jnp.log(safe / max_exact) / math.log(REL_MAX_DIST / max_exact)
                         * (REL_BUCKETS - max_exact)).astype(jnp.int32)
    large = jnp.minimum(large, REL_BUCKETS - 1)
    return jnp.where(dist < max_exact, dist, large)


def _group_bias(rel_bias, g):
    w, d = B_GROUPS[g]
    dist = d * jnp.arange(w // d + 1, dtype=jnp.int32)
    tab = rel_bias[_t5_bucket(dist)]
    return tab[:, g * HEADS_PER_GROUP:(g + 1) * HEADS_PER_GROUP].T.astype(F32)


def _toeplitz(vec, rows, cols):
    hh, p = vec.shape
    flat = jnp.tile(vec, (1, rows))[:, :rows * (p - 1)]
    return flat.reshape(hh, rows, p - 1)[:, :, :cols]


def _band_bias(bias):
    period = 3 * BAND
    vec = jnp.full((bias.shape[0], period), NEG_INF, F32)
    vec = jax.lax.dynamic_update_slice(vec, bias[:, ::-1], (0, 0))
    mat = _toeplitz(vec, BAND, 2 * BAND)
    c = jnp.arange(2 * BAND, dtype=jnp.int32)[None, None, :]
    first = jnp.where(c >= BAND, mat, NEG_INF)
    return jnp.stack([first, mat])


def _sample_bias(bias, w, d, nq):
    hh = bias.shape[0]
    neg = jnp.full_like(bias, NEG_INF)
    pad = jnp.full((hh, nq), NEG_INF, F32)
    rev_dist = jnp.stack([neg] * (d - 1) + [bias[:, ::-1]], -1).reshape(hh, -1)
    bs = _toeplitz(jnp.concatenate([rev_dist[:, d - 1:d - 1 + w], pad], 1), nq, w)
    bn = _toeplitz(jnp.concatenate([bias[:, 0:1], pad, rev_dist[:, w + d - nq:w + d - 1]], 1), nq, nq)
    return bs, bn


def _trunk(x, conv_prev, mem_kv, kv_state, a_w_in, a_w_dw, a_b_dw, a_cn_g, a_cn_b, a_w_out,
           b_w_in, b_w_out, w_kv_shared, rel_bias, ln_g, ln_b):
    b, t, d_model = x.shape
    m = b * t
    x2 = x.reshape(m, d_model)
    xb = x2.astype(BF16)
    new_conv = []
    for l in range(N_A_LAYERS):
        u, gates = _a_proj(xb, a_w_in, l)
        cm, nc = _a_mix(u.reshape(b, t, -1), gates.reshape(b, t, -1), conv_prev[l], mem_kv, l,
                        a_w_dw[l], a_b_dw[l], a_cn_g[l], a_cn_b[l])
        new_conv.append(nc)
        x2, xb = _out_ln(cm.reshape(m, d_model), a_w_out, l, x2, ln_g[l], ln_b[l])

    kv = _mm(xb, w_kv_shared).reshape(b, t, 2 * MIX_WIDTH)
    biases = [_group_bias(rel_bias, g) for g in range(N_GROUPS)]
    kv_new = [_kv_window(kv, g, min(w, t)) for g, (w, _) in enumerate(B_GROUPS)]
    if kv_state is None:
        new_bufs = kv_new
        band_bias = jnp.stack([_band_bias(bi) for bi in biases])
    else:
        bias_state, bias_new = zip(*[_sample_bias(bi, w, d, t) for bi, (w, d) in zip(biases, B_GROUPS)])

    x3 = x2.reshape(b, t, d_model)
    for i in range(DEPTH - N_A_LAYERS):
        l = N_A_LAYERS + i
        q, gates = _b_proj(xb.reshape(m, d_model), b_w_in, i)
        q, gates = q.reshape(b, t, -1), gates.reshape(b, t, -1)
        if kv_state is None:
            o = _b_attn(q, gates, kv, band_bias)
        else:
            o, updated = _samp_attn(q, gates, kv_state, kv_new, bias_state, bias_new, emit_state=(i == 0))
            if i == 0:
                new_bufs = updated
        x3, xb = _b_out(o, gates, mem_kv, l, b_w_out, i, x3, ln_g[l], ln_b[l], emit_bf16=l < DEPTH - 1)
    new_bufs = [nb.reshape(b, -1, 2, HEADS_PER_GROUP, HEAD_DIM) for nb in new_bufs]
    return x3, jnp.stack(new_conv), new_bufs


def kernel(x_prompt, x_sample, state_conv, state_kv_g0, state_kv_g1, state_kv_g2, cache_mem_kv, mem_prompt,
           a_w_in, a_w_dw, a_b_dw, a_cn_g, a_cn_b, a_w_out, b_w_in, b_w_out, w_kv_shared, w_mem_kv,
           rel_bias, ln_g, ln_b):
    bp = x_prompt.shape[0]
    bs = x_sample.shape[0]
    mem_p = _mem_proj(mem_prompt.reshape(bp * N_MEM, D_MODEL), w_mem_kv).reshape(
        DEPTH, bp, N_MEM * KV_SLOTS, HEAD_DIM)
    new_mem_kv = mem_p.reshape(DEPTH, bp, N_MEM, 2, MEM_HEADS, HEAD_DIM)
    conv_zero = jnp.zeros((N_A_LAYERS, bp, CONV_WIDTH - 1, CONV_CH), x_prompt.dtype)
    weights = (a_w_in, a_w_dw, a_b_dw, a_cn_g, a_cn_b, a_w_out, b_w_in, b_w_out, w_kv_shared, rel_bias, ln_g, ln_b)

    y_p, conv_p, bufs_p = _trunk(x_prompt, conv_zero, mem_p, None, *weights)
    kv_state = [s.reshape(bs, s.shape[1] * KV_SLOTS, HEAD_DIM) for s in (state_kv_g0, state_kv_g1, state_kv_g2)]
    mem_s = cache_mem_kv.reshape(DEPTH, bs, N_MEM * KV_SLOTS, HEAD_DIM)
    y_s, conv_s, bufs_s = _trunk(x_sample, state_conv, mem_s, kv_state, *weights)

    return (y_p, y_s, conv_p, conv_s, bufs_p[0], bufs_s[0], bufs_p[1], bufs_s[1], bufs_p[2], bufs_s[2], new_mem_kv)
```

```python
import functools
import math

import jax
import jax.numpy as jnp
from jax.experimental import pallas as pl
from jax.experimental.pallas import tpu as pltpu

D_MODEL = 2048
DEPTH = 4
HEAD_DIM = 128
MEM_HEADS = 4
MEM_WIDTH = MEM_HEADS * HEAD_DIM
N_MEM = 256
MIX_WIDTH = D_MODEL - MEM_WIDTH
N_A_LAYERS = DEPTH // 2
CONV_CH = MIX_WIDTH
CONV_WIDTH = 31
B_GROUPS = ((128, 1), (512, 4), (2048, 16))
N_GROUPS = len(B_GROUPS)
HEADS_PER_GROUP = 4
B_HEADS = N_GROUPS * HEADS_PER_GROUP
GROUP_WIDTH = HEADS_PER_GROUP * HEAD_DIM
KV_SLOTS = 2 * HEADS_PER_GROUP
B_OUT = GROUP_WIDTH
REL_BUCKETS = 32
REL_MAX_DIST = 2048
BAND = 128
ATTN_TOKENS = BAND * max(d for _, d in B_GROUPS)
LN_EPS = 1e-5
ALPHA = (2 * DEPTH) ** 0.25
NEG_INF = -1e30
SCALE = HEAD_DIM ** -0.5

HALO = 32
VMEM_LIMIT = 56 * 1024 * 1024
BF16 = jnp.bfloat16
F32 = jnp.float32


def _params(*sem):
    return pltpu.CompilerParams(dimension_semantics=sem, vmem_limit_bytes=VMEM_LIMIT)


def _sigmoid(x):
    return 0.5 * jnp.tanh(0.5 * x) + 0.5


def _silu(x):
    return x * _sigmoid(x)


def _layernorm(z, g, b):
    mu = jnp.mean(z, -1, keepdims=True)
    zc = z - mu
    var = jnp.mean(zc * zc, -1, keepdims=True)
    return zc * jax.lax.rsqrt(var + LN_EPS) * g + b


DOT_CHUNK = 512


PROJ_ROWS = 2048


def _project(x_ref, w_ref, emit):
    wb = w_ref[...].astype(BF16)
    step = min(x_ref.shape[0], DOT_CHUNK)
    for r in range(0, x_ref.shape[0], step):
        emit(slice(r, r + step), jnp.dot(x_ref[r:r + step, :], wb, preferred_element_type=F32))


def _mm_kernel(x_ref, w_ref, o_ref):
    def emit(rows, y):
        o_ref[rows, :] = y
    _project(x_ref, w_ref, emit)


def _mm(x, w, *, tn=512):
    m, k = x.shape
    n = w.shape[1]
    tm = min(m, PROJ_ROWS)
    return pl.pallas_call(
        _mm_kernel,
        grid=(m // tm, n // tn),
        in_specs=[pl.BlockSpec((tm, k), lambda i, j: (i, 0)),
                  pl.BlockSpec((k, tn), lambda i, j: (0, j))],
        out_specs=pl.BlockSpec((tm, tn), lambda i, j: (i, j)),
        out_shape=jax.ShapeDtypeStruct((m, n), F32),
        compiler_params=_params("arbitrary", "arbitrary"),
        name="mm",
    )(x, w)


B_GATE_STEP = MIX_WIDTH // GROUP_WIDTH


def _b_proj_kernel(x_ref, w_ref, q_ref, g_ref):
    j = pl.program_id(1)

    @pl.when(j < B_GATE_STEP)
    def _():
        def emit(rows, y):
            q_ref[rows, :] = y
        _project(x_ref, w_ref, emit)

    @pl.when(j == B_GATE_STEP + 1)
    def _():
        def emit(rows, y):
            g_ref[rows, :] = y.astype(BF16)
        _project(x_ref, w_ref, emit)

    @pl.when((j == B_GATE_STEP) | (j == B_GATE_STEP + 2))
    def _():
        def emit(rows, y):
            g_ref[rows, :] = _silu(y).astype(BF16)
        _project(x_ref, w_ref, emit)


def _b_proj(x, w, layer):
    m, k = x.shape
    tm = min(m, PROJ_ROWS)
    tn = GROUP_WIDTH
    return pl.pallas_call(
        _b_proj_kernel,
        grid=(m // tm, B_GATE_STEP + 3),
        in_specs=[pl.BlockSpec((tm, k), lambda i, j: (i, 0)),
                  pl.BlockSpec((None, k, tn), lambda i, j: (layer, 0, j))],
        out_specs=[pl.BlockSpec((tm, tn), lambda i, j: (i, jnp.minimum(j, B_GATE_STEP - 1))),
                   pl.BlockSpec((tm, tn), lambda i, j: (i, jnp.maximum(j - B_GATE_STEP, 0)))],
        out_shape=[jax.ShapeDtypeStruct((m, MIX_WIDTH), F32),
                   jax.ShapeDtypeStruct((m, B_OUT + 2 * MEM_WIDTH), BF16)],
        compiler_params=_params("arbitrary", "arbitrary"),
        name="b_proj",
    )(x, w)


def _store_kv_slots(o_ref, lead, slot0, vals):
    rows = vals.shape[0]
    for j in range(vals.shape[1] // HEAD_DIM):
        o_ref[lead + (pl.ds(slot0 + j, rows, stride=KV_SLOTS), slice(None))] = vals[:, j * HEAD_DIM:(j + 1) * HEAD_DIM]


def _mem_proj_kernel(x_ref, w_ref, o_ref, xb_ref):
    @pl.when(pl.program_id(0) == 0)
    def _():
        xb_ref[...] = x_ref[...].astype(BF16)

    _store_kv_slots(o_ref, (), 0, jnp.dot(xb_ref[...], w_ref[...].astype(BF16), preferred_element_type=F32))


def _mem_proj(x, w):
    m, k = x.shape
    nl, _, n = w.shape
    return pl.pallas_call(
        _mem_proj_kernel,
        grid=(nl,),
        in_specs=[pl.BlockSpec((m, k), lambda l: (0, 0)),
                  pl.BlockSpec((None, k, n), lambda l: (l, 0, 0))],
        out_specs=pl.BlockSpec((None, m * KV_SLOTS, HEAD_DIM), lambda l: (l, 0, 0)),
        out_shape=jax.ShapeDtypeStruct((nl, m * KV_SLOTS, HEAD_DIM), F32),
        scratch_shapes=[pltpu.VMEM((m, k), BF16)],
        compiler_params=_params("arbitrary"),
        name="mem_proj",
    )(x, w)


def _kv_window_kernel(k_ref, v_ref, o_ref):
    _store_kv_slots(o_ref, (0,), 0, k_ref[0])
    _store_kv_slots(o_ref, (0,), HEADS_PER_GROUP, v_ref[0])


def _kv_window(kv, g, w):
    b, t, _ = kv.shape
    rows = min(w, 512)
    first = (t - w) // rows
    return pl.pallas_call(
        _kv_window_kernel,
        grid=(b, w // rows),
        in_specs=[pl.BlockSpec((1, rows, GROUP_WIDTH), lambda bi, i: (bi, first + i, g)),
                  pl.BlockSpec((1, rows, GROUP_WIDTH), lambda bi, i: (bi, first + i, N_GROUPS + g))],
        out_specs=pl.BlockSpec((1, rows * KV_SLOTS, HEAD_DIM), lambda bi, i: (bi, i, 0)),
        out_shape=jax.ShapeDtypeStruct((b, w * KV_SLOTS, HEAD_DIM), F32),
        compiler_params=_params("arbitrary", "arbitrary"),
        name="kv_window",
    )(kv, kv)


A_TN = 512
N_GLU = CONV_CH // A_TN
QM_STEP = 3 * N_GLU


def _a_proj_kernel(x_ref, w_ref, u_ref, g_ref, ga_ref):
    j = pl.program_id(1)
    project = functools.partial(_project, x_ref, w_ref)
    is_glu = j < 2 * N_GLU

    @pl.when(is_glu & (j % 2 == 0))
    def _():
        def emit(rows, y):
            ga_ref[rows, :] = y
        project(emit)

    @pl.when(is_glu & (j % 2 == 1))
    def _():
        def emit(rows, y):
            u_ref[rows, :] = ga_ref[rows, :] * _sigmoid(y)
        project(emit)

    @pl.when(j == QM_STEP)
    def _():
        def emit(rows, y):
            g_ref[rows, :] = y.astype(BF16)
        project(emit)

    @pl.when(jnp.logical_not(is_glu) & (j != QM_STEP))
    def _():
        def emit(rows, y):
            g_ref[rows, :] = _silu(y).astype(BF16)
        project(emit)


def _a_proj(x, w, layer):
    m, k = x.shape
    tm = min(m, PROJ_ROWS)
    n_steps = 3 * N_GLU + 2 * MEM_WIDTH // A_TN

    def w_col(j):
        return jnp.where(j < 2 * N_GLU, (j % 2) * N_GLU + j // 2, j)

    return pl.pallas_call(
        _a_proj_kernel,
        grid=(m // tm, n_steps),
        in_specs=[pl.BlockSpec((tm, k), lambda i, j: (i, 0)),
                  pl.BlockSpec((None, k, A_TN), lambda i, j: (layer, 0, w_col(j)))],
        out_specs=[pl.BlockSpec((tm, A_TN), lambda i, j: (i, jnp.minimum(j // 2, N_GLU - 1))),
                   pl.BlockSpec((tm, A_TN), lambda i, j: (i, jnp.maximum(j - 2 * N_GLU, 0)))],
        out_shape=[jax.ShapeDtypeStruct((m, CONV_CH), F32),
                   jax.ShapeDtypeStruct((m, CONV_CH + 2 * MEM_WIDTH), BF16)],
        scratch_shapes=[pltpu.VMEM((tm, A_TN), F32)],
        compiler_params=_params("arbitrary", "arbitrary"),
        name="a_proj",
    )(x, w)


def _out_ln_kernel(cm_ref, w_ref, x_ref, g_ref, b_ref, o_ref, ob_ref, wb_ref):
    @pl.when(pl.program_id(0) == 0)
    def _():
        wb_ref[...] = w_ref[...].astype(BF16)

    cm = cm_ref[...]
    for c in range(0, o_ref.shape[1], DOT_CHUNK):
        cols = slice(c, c + DOT_CHUNK)
        o_ref[:, cols] = ALPHA * x_ref[:, cols] + jnp.dot(cm, wb_ref[:, cols], preferred_element_type=F32)
    out = _layernorm(o_ref[...], g_ref[...], b_ref[...])
    o_ref[...] = out
    ob_ref[...] = out.astype(BF16)


def _out_ln(cm, w, layer, x, g, b):
    m, kk = cm.shape
    d = w.shape[2]
    tm = min(m, 512)
    row = pl.BlockSpec((tm, d), lambda i: (i, 0))
    return pl.pallas_call(
        _out_ln_kernel,
        grid=(m // tm,),
        in_specs=[pl.BlockSpec((tm, kk), lambda i: (i, 0)),
                  pl.BlockSpec((None, kk, d), lambda i: (layer, 0, 0), pipeline_mode=pl.Buffered(1)),
                  row,
                  pl.BlockSpec((1, d), lambda i: (0, 0)),
                  pl.BlockSpec((1, d), lambda i: (0, 0))],
        out_specs=[row, row],
        out_shape=[jax.ShapeDtypeStruct((m, d), F32), jax.ShapeDtypeStruct((m, d), BF16)],
        scratch_shapes=[pltpu.VMEM((kk, d), BF16)],
        compiler_params=_params("arbitrary"),
        name="out_ln",
    )(cm, w, x, g.reshape(1, d), b.reshape(1, d))


def _kv_rows(ref, lead, slot, n):
    return ref[lead + (pl.ds(slot, n, stride=KV_SLOTS), slice(None))]


def _mem_attn(qm, mem_ref, entry=0):
    outs = []
    for h in range(MEM_HEADS):
        lo = h * HEAD_DIM
        q = qm[:, lo:lo + HEAD_DIM]
        k = _kv_rows(mem_ref, (entry,), h, N_MEM).astype(BF16)
        v = _kv_rows(mem_ref, (entry,), MEM_HEADS + h, N_MEM).astype(BF16)
        s = jax.lax.dot_general(q, k, (((1,), (1,)), ((), ())), preferred_element_type=F32) * SCALE
        s = s - jnp.max(s, -1, keepdims=True)
        e = jnp.exp(s)
        p = e / jnp.sum(e, -1, keepdims=True)
        outs.append(jnp.dot(p.astype(BF16), v, preferred_element_type=F32))
    return jnp.concatenate(outs, -1)


def _a_mix_kernel(*refs, tt, tr, has_halo):
    if has_halo:
        (u_ref, sgate, qm, sgm, u_prev, cprev, mem, wdw, bdw, cng, cnb,
         cm_out, nc_out, ext, wrep, conv) = refs
    else:
        (u_ref, sgate, qm, sgm, cprev, mem, wdw, bdw, cng, cnb,
         cm_out, nc_out, ext, wrep, conv) = refs
    i = pl.program_id(1)
    n_blk = CONV_CH // HEAD_DIM

    @pl.when((pl.program_id(0) == 0) & (i == 0))
    def _():
        for k in range(CONV_WIDTH):
            wrep[k] = jnp.broadcast_to(wdw[k:k + 1, :], (8, CONV_CH))

    for c in range(n_blk):
        lanes = slice(c * HEAD_DIM, (c + 1) * HEAD_DIM)
        if has_halo:
            ext[c, 0:HALO, :] = jnp.where(i == 0, cprev[0, :, lanes], u_prev[0, :, lanes])
        else:
            ext[c, 0:HALO, :] = cprev[0, :, lanes]
        ext[c, HALO:HALO + tt, :] = u_ref[0, :, lanes]
        nc_out[0, :, lanes] = ext[c, HALO + tt - (CONV_WIDTH - 1):HALO + tt, :]

    off0 = HALO - (CONV_WIDTH - 1)
    for c in range(n_blk):
        lanes = slice(c * HEAD_DIM, (c + 1) * HEAD_DIM)

        def body(rc, carry, c=c, lanes=lanes):
            r0 = pl.multiple_of(rc * tr, 8)
            acc = jnp.zeros((tr // 8, 8, HEAD_DIM), F32)
            for k in range(CONV_WIDTH):
                win = ext[c, pl.ds(r0 + k + off0, tr), :].reshape(tr // 8, 8, HEAD_DIM)
                acc = acc + win * wrep[k, :, lanes][None]
            conv[pl.ds(r0, tr), lanes] = acc.reshape(tr, HEAD_DIM) + bdw[:, lanes]
            return carry

        jax.lax.fori_loop(0, tt // tr, body, 0)

    zn = _layernorm(conv[...], cng[...], cnb[...])
    cm_out[0, :, 0:CONV_CH] = (_silu(zn) * sgate[0].astype(F32)).astype(BF16)
    cm_out[0, :, CONV_CH:D_MODEL] = (_mem_attn(qm[0], mem) * sgm[0].astype(F32)).astype(BF16)


def _a_mix(u, gates, conv_prev, mem, layer, w_dw, b_dw, cn_g, cn_b):
    b, t, _ = u.shape
    tt = min(t, 512)
    tr = min(tt, 128)
    nt = t // tt
    has_halo = nt > 1
    cprev = jnp.pad(conv_prev, ((0, 0), (HALO - (CONV_WIDTH - 1), 0), (0, 0)))
    c3 = CONV_CH
    qcol = CONV_CH // MEM_WIDTH

    in_specs = [pl.BlockSpec((1, tt, c3), lambda bi, i: (bi, i, 0)),
                pl.BlockSpec((1, tt, c3), lambda bi, i: (bi, i, 0)),
                pl.BlockSpec((1, tt, MEM_WIDTH), lambda bi, i: (bi, i, qcol)),
                pl.BlockSpec((1, tt, MEM_WIDTH), lambda bi, i: (bi, i, qcol + 1))]
    args = [u, gates, gates, gates]
    if has_halo:
        per = tt // HALO
        in_specs += [pl.BlockSpec((1, HALO, c3), lambda bi, i: (bi, jnp.maximum(i * per - 1, 0), 0))]
        args += [u]
    in_specs += [pl.BlockSpec((1, HALO, c3), lambda bi, i: (bi, 0, 0)),
                 pl.BlockSpec((None, 1, N_MEM * KV_SLOTS, HEAD_DIM), lambda bi, i: (layer, bi, 0, 0)),
                 pl.BlockSpec((CONV_WIDTH, c3), lambda bi, i: (0, 0)),
                 pl.BlockSpec((1, c3), lambda bi, i: (0, 0)),
                 pl.BlockSpec((1, c3), lambda bi, i: (0, 0)),
                 pl.BlockSpec((1, c3), lambda bi, i: (0, 0))]
    args += [cprev, mem, w_dw, b_dw.reshape(1, c3), cn_g.reshape(1, c3), cn_b.reshape(1, c3)]

    return pl.pallas_call(
        functools.partial(_a_mix_kernel, tt=tt, tr=tr, has_halo=has_halo),
        grid=(b, nt),
        in_specs=in_specs,
        out_specs=[pl.BlockSpec((1, tt, D_MODEL), lambda bi, i: (bi, i, 0)),
                   pl.BlockSpec((1, CONV_WIDTH - 1, c3), lambda bi, i: (bi, 0, 0))],
        out_shape=[jax.ShapeDtypeStruct((b, t, D_MODEL), BF16),
                   jax.ShapeDtypeStruct((b, CONV_WIDTH - 1, c3), F32)],
        scratch_shapes=[pltpu.VMEM((c3 // HEAD_DIM, tt + HALO, HEAD_DIM), F32),
                        pltpu.VMEM((CONV_WIDTH, 8, c3), F32),
                        pltpu.VMEM((tt, c3), F32)],
        compiler_params=_params("arbitrary", "arbitrary"),
        name="a_mix",
    )(*args)


def _rows(ref, start, size, stride):
    if stride == 1:
        return ref[0, start:start + size, :]
    return ref[0, pl.ds(start, size, stride=stride), :]


def _b_attn_kernel(q0, q1, q2, k0, k1, k2, v0, v1, v2, hk0, hk1, hk2, hv0, hv1, hv2, bias_ref, gate_ref,
                   o_ref, num_scr, m_scr, s_scr):
    first = jnp.minimum(pl.program_id(2), 1)
    qs, ks, vs = (q0, q1, q2), (k0, k1, k2), (v0, v1, v2)
    hks, hvs = (hk0, hk1, hk2), (hv0, hv1, hv2)
    dn = (((1,), (1,)), ((), ()))
    for g, (_, d) in enumerate(B_GROUPS):
        for r in range(d):
            for s in range(ATTN_TOKENS // (BAND * d)):
                start = r + d * BAND * s
                q = _rows(qs[g], start, BAND, d)
                if s == 0:
                    k = jnp.concatenate([_rows(hks[g], r, BAND, d), _rows(ks[g], r, BAND, d)], 0)
                    v = jnp.concatenate([_rows(hvs[g], r, BAND, d), _rows(vs[g], r, BAND, d)], 0)
                    bias = bias_ref[g, first, 0]
                else:
                    k = _rows(ks[g], start - d * BAND, 2 * BAND, d)
                    v = _rows(vs[g], start - d * BAND, 2 * BAND, d)
                    bias = bias_ref[g, 1, 0]
                logits = jax.lax.dot_general(q.astype(BF16), k.astype(BF16), dn, preferred_element_type=F32)
                logits = logits * SCALE + bias
                m = jnp.max(logits, -1, keepdims=True)
                e = jnp.exp(logits - m)
                ssum = jnp.sum(e, -1, keepdims=True)
                num = jnp.dot(e.astype(BF16), v.astype(BF16), preferred_element_type=F32)
                if d == 1:
                    dst = pl.ds(start, BAND)
                else:
                    dst = pl.ds(start, BAND, stride=d)
                num_scr[g, dst, :] = num
                m_scr[g, dst, :] = jnp.broadcast_to(m, (BAND, HEAD_DIM))
                s_scr[g, dst, :] = jnp.broadcast_to(ssum, (BAND, HEAD_DIM))

    def merge(c, carry):
        sl = pl.ds(pl.multiple_of(c * BAND, BAND), BAND)
        ms = [m_scr[g, sl, :] for g in range(N_GROUPS)]
        m_all = jnp.maximum(jnp.maximum(ms[0], ms[1]), ms[2])
        coef = [jnp.exp(m - m_all) for m in ms]
        num = coef[0] * num_scr[0, sl, :]
        den = coef[0] * s_scr[0, sl, :]
        for g in range(1, N_GROUPS):
            num = num + coef[g] * num_scr[g, sl, :]
            den = den + coef[g] * s_scr[g, sl, :]
        o_ref[0, sl, :] = (num / den * gate_ref[0, sl, :].astype(F32)).astype(BF16)
        return carry

    jax.lax.fori_loop(0, ATTN_TOKENS // BAND, merge, 0)


def _b_attn(q, gates, kv, band_bias):
    b, t, _ = q.shape
    tb = ATTN_TOKENS
    blk = (1, tb, HEAD_DIM)

    def cur(col0):
        return lambda bi, hi, i: (bi, i, col0 + hi)

    def halo_spec(d, col0):
        rows = BAND * d
        per = tb // rows
        return pl.BlockSpec((1, rows, HEAD_DIM), lambda bi, hi, i: (bi, jnp.maximum(i * per - 1, 0), col0 + hi))

    hpg = HEADS_PER_GROUP
    in_specs = ([pl.BlockSpec(blk, cur(hpg * g)) for g in range(N_GROUPS)]
                + [pl.BlockSpec(blk, cur(hpg * g)) for g in range(N_GROUPS)]
                + [pl.BlockSpec(blk, cur(B_HEADS + hpg * g)) for g in range(N_GROUPS)]
                + [halo_spec(d, hpg * g) for g, (_, d) in enumerate(B_GROUPS)]
                + [halo_spec(d, B_HEADS + hpg * g) for g, (_, d) in enumerate(B_GROUPS)]
                + [pl.BlockSpec((N_GROUPS, 2, 1, BAND, 2 * BAND), lambda bi, hi, i: (0, 0, hi, 0, 0)),
                   pl.BlockSpec(blk, cur(0))])
    return pl.pallas_call(
        _b_attn_kernel,
        grid=(b, hpg, t // tb),
        in_specs=in_specs,
        out_specs=pl.BlockSpec(blk, lambda bi, hi, i: (bi, i, hi)),
        out_shape=jax.ShapeDtypeStruct((b, t, B_OUT), BF16),
        scratch_shapes=[pltpu.VMEM((N_GROUPS, tb, HEAD_DIM), F32)] * 3,
        compiler_params=_params("parallel", "parallel", "arbitrary"),
        name="b_attn",
    )(q, q, q, kv, kv, kv, kv, kv, kv, kv, kv, kv, kv, kv, kv, band_bias, gates)


def _b_out_kernel(o_ref, qm_ref, gm_ref, mem_ref, w_ref, x_ref, g_ref, b_ref, y_ref, *rest, nb, tt):
    yb_ref, wb_ref = rest if len(rest) == 2 else (None, rest[0])

    @pl.when((pl.program_id(0) == 0) & (pl.program_id(1) == 0))
    def _():
        wb_ref[...] = w_ref[...].astype(BF16)

    stack = lambda parts: parts[0] if nb == 1 else jnp.concatenate(parts, 0)
    mbranch = stack([_mem_attn(qm_ref[bi], mem_ref, bi) * gm_ref[bi].astype(F32) for bi in range(nb)])
    if nb == 1:
        obranch = o_ref[0]
    else:
        obranch = jnp.concatenate([o_ref[bi].astype(F32) for bi in range(nb)], 0).astype(BF16)
    y = jnp.dot(obranch, wb_ref[0:B_OUT, :], preferred_element_type=F32)
    y = y + jnp.dot(mbranch.astype(BF16), wb_ref[B_OUT:B_OUT + MEM_WIDTH, :], preferred_element_type=F32)
    x = stack([x_ref[bi] for bi in range(nb)])
    out = _layernorm(ALPHA * x + y, g_ref[...], b_ref[...])
    for bi in range(nb):
        y_ref[bi] = out[bi * tt:(bi + 1) * tt]
        if yb_ref is not None:
            yb_ref[bi] = out[bi * tt:(bi + 1) * tt].astype(BF16)


def _b_out(o, gates, mem, layer, w, wl, x, g, b, emit_bf16):
    bsz, t, d = x.shape
    tt = min(t, 512)
    nb = max(1, min(bsz, 64 // tt))
    kk = w.shape[1]
    qcol = B_OUT // MEM_WIDTH
    row = pl.BlockSpec((nb, tt, d), lambda bi, i: (bi, i, 0))
    out_dtypes = [F32, BF16] if emit_bf16 else [F32]
    res = pl.pallas_call(
        functools.partial(_b_out_kernel, nb=nb, tt=tt),
        grid=(bsz // nb, t // tt),
        in_specs=[pl.BlockSpec((nb, tt, B_OUT), lambda bi, i: (bi, i, 0)),
                  pl.BlockSpec((nb, tt, MEM_WIDTH), lambda bi, i: (bi, i, qcol)),
                  pl.BlockSpec((nb, tt, MEM_WIDTH), lambda bi, i: (bi, i, qcol + 1)),
                  pl.BlockSpec((None, nb, N_MEM * KV_SLOTS, HEAD_DIM), lambda bi, i: (layer, bi, 0, 0)),
                  pl.BlockSpec((None, kk, d), lambda bi, i: (wl, 0, 0), pipeline_mode=pl.Buffered(1)),
                  row,
                  pl.BlockSpec((1, d), lambda bi, i: (0, 0)),
                  pl.BlockSpec((1, d), lambda bi, i: (0, 0))],
        out_specs=[row] * len(out_dtypes),
        out_shape=[jax.ShapeDtypeStruct((bsz, t, d), dt) for dt in out_dtypes],
        scratch_shapes=[pltpu.VMEM((kk, d), BF16)],
        compiler_params=_params("arbitrary", "arbitrary"),
        name="b_out",
    )(o, gates, gates, mem, w, x, g.reshape(1, d), b.reshape(1, d))
    return res if emit_bf16 else (res[0], None)


def _samp_attn_kernel(*refs, nq, widths, emit_state):
    ng = N_GROUPS
    q_ref, gate_ref = refs[0], refs[1]
    st_refs, new_refs = refs[2:2 + ng], refs[2 + ng:2 + 2 * ng]
    bs_refs, bn_refs = refs[2 + 2 * ng:2 + 3 * ng], refs[2 + 3 * ng:2 + 4 * ng]
    o_ref = refs[2 + 4 * ng]
    dn = (((1,), (1,)), ((), ()))
    for h in range(HEADS_PER_GROUP):
        ms, sums, nums = [], [], []
        for g in range(ng):
            lo = (g * HEADS_PER_GROUP + h) * HEAD_DIM
            q = q_ref[0, :, lo:lo + HEAD_DIM].astype(BF16)
            ks = _kv_rows(st_refs[g], (0,), h, widths[g]).astype(BF16)
            vs = _kv_rows(st_refs[g], (0,), HEADS_PER_GROUP + h, widths[g]).astype(BF16)
            kn = _kv_rows(new_refs[g], (0,), h, nq).astype(BF16)
            vn = _kv_rows(new_refs[g], (0,), HEADS_PER_GROUP + h, nq).astype(BF16)
            ls = jax.lax.dot_general(q, ks, dn, preferred_element_type=F32) * SCALE + bs_refs[g][h]
            ln = jax.lax.dot_general(q, kn, dn, preferred_element_type=F32) * SCALE + bn_refs[g][h]
            m = jnp.maximum(jnp.max(ls, -1, keepdims=True), jnp.max(ln, -1, keepdims=True))
            es = jnp.exp(ls - m)
            en = jnp.exp(ln - m)
            ms.append(m)
            sums.append(jnp.sum(es, -1, keepdims=True) + jnp.sum(en, -1, keepdims=True))
            nums.append(jnp.dot(es.astype(BF16), vs, preferred_element_type=F32)
                        + jnp.dot(en.astype(BF16), vn, preferred_element_type=F32))
        m_all = jnp.maximum(jnp.maximum(ms[0], ms[1]), ms[2])
        coef = [jnp.exp(m - m_all) for m in ms]
        num = coef[0] * nums[0]
        den = coef[0] * sums[0]
        for g in range(1, ng):
            num = num + coef[g] * nums[g]
            den = den + coef[g] * sums[g]
        lo = h * HEAD_DIM
        o_ref[0, :, lo:lo + HEAD_DIM] = (num / den * gate_ref[0, :, lo:lo + HEAD_DIM].astype(F32)).astype(BF16)

    if emit_state:
        for g in range(ng):
            out = refs[3 + 4 * ng + g]
            rows, shift = widths[g] * KV_SLOTS, nq * KV_SLOTS
            out[0, 0:rows - shift, :] = st_refs[g][0, shift:rows, :]
            out[0, rows - shift:rows, :] = new_refs[g][0]


def _samp_attn(q, gates, states, news, bias_state, bias_new, emit_state):
    b, nq, _ = q.shape
    widths = tuple(s.shape[1] // KV_SLOTS for s in states)
    whole = lambda a: pl.BlockSpec((1,) + a.shape[1:], lambda i: (i, 0, 0))
    const = lambda a: pl.BlockSpec(a.shape, lambda i: (0, 0, 0))
    out_spec = pl.BlockSpec((1, nq, GROUP_WIDTH), lambda i: (i, 0, 0))
    out_shape = jax.ShapeDtypeStruct((b, nq, GROUP_WIDTH), BF16)
    res = pl.pallas_call(
        functools.partial(_samp_attn_kernel, nq=nq, widths=widths, emit_state=emit_state),
        grid=(b,),
        in_specs=([pl.BlockSpec((1, nq, MIX_WIDTH), lambda i: (i, 0, 0)),
                   pl.BlockSpec((1, nq, GROUP_WIDTH), lambda i: (i, 0, 0))]
                  + [whole(s) for s in states] + [whole(n) for n in news]
                  + [const(a) for a in bias_state] + [const(a) for a in bias_new]),
        out_specs=[out_spec] + ([whole(s) for s in states] if emit_state else []),
        out_shape=[out_shape] + ([jax.ShapeDtypeStruct(s.shape, s.dtype) for s in states] if emit_state else []),
        compiler_params=_params("arbitrary"),
        name="samp_attn",
    )(q, gates, *states, *news, *bias_state, *bias_new)
    return res[0], list(res[1:])


def _t5_bucket(dist):
    max_exact = REL_BUCKETS // 2
    safe = jnp.maximum(dist, 1).astype(F32)
    large = max_exact + (jnp.log(safe / max_exact) / math.log(REL_MAX_DIST / max_exact)
                         * (REL_BUCKETS - max_exact)).astype(jnp.int32)
    large = jnp.minimum(large, REL_BUCKETS - 1)
    return jnp.where(dist < max_exact, dist, large)


def _group_bias(rel_bias, g):
    w, d = B_GROUPS[g]
    dist = d * jnp.arange(w // d + 1, dtype=jnp.int32)
    tab = rel_bias[_t5_bucket(dist)]
    return tab[:, g * HEADS_PER_GROUP:(g + 1) * HEADS_PER_GROUP].T.astype(F32)


def _toeplitz(vec, rows, cols):
    hh, p = vec.shape
    flat = jnp.tile(vec, (1, rows))[:, :rows * (p - 1)]
    return flat.reshape(hh, rows, p - 1)[:, :, :cols]


def _band_bias(bias):
    period = 3 * BAND
    vec = jnp.full((bias.shape[0], period), NEG_INF, F32)
    vec = jax.lax.dynamic_update_slice(vec, bias[:, ::-1], (0, 0))
    mat = _toeplitz(vec, BAND, 2 * BAND)
    c = jnp.arange(2 * BAND, dtype=jnp.int32)[None, None, :]
    first = jnp.where(c >= BAND, mat, NEG_INF)
    return jnp.stack([first, mat])


def _sample_bias(bias, w, d, nq):
    hh = bias.shape[0]
    neg = jnp.full_like(bias, NEG_INF)
    pad = jnp.full((hh, nq), NEG_INF, F32)
    rev_dist = jnp.stack([neg] * (d - 1) + [bias[:, ::-1]], -1).reshape(hh, -1)
    bs = _toeplitz(jnp.concatenate([rev_dist[:, d - 1:d - 1 + w], pad], 1), nq, w)
    bn = _toeplitz(jnp.concatenate([bias[:, 0:1], pad, rev_dist[:, w + d - nq:w + d - 1]], 1), nq, nq)
    return bs, bn


def _trunk(x, conv_prev, mem_kv, kv_state, a_w_in, a_w_dw, a_b_dw, a_cn_g, a_cn_b, a_w_out,
           b_w_in, b_w_out, w_kv_shared, rel_bias, ln_g, ln_b):
    b, t, d_model = x.shape
    m = b * t
    x2 = x.reshape(m, d_model)
    xb = x2.astype(BF16)
    new_conv = []
    for l in range(N_A_LAYERS):
        u, gates = _a_proj(xb, a_w_in, l)
        cm, nc = _a_mix(u.reshape(b, t, -1), gates.reshape(b, t, -1), conv_prev[l], mem_kv, l,
                        a_w_dw[l], a_b_dw[l], a_cn_g[l], a_cn_b[l])
        new_conv.append(nc)
        x2, xb = _out_ln(cm.reshape(m, d_model), a_w_out, l, x2, ln_g[l], ln_b[l])

    kv = _mm(xb, w_kv_shared).reshape(b, t, 2 * MIX_WIDTH)
    biases = [_group_bias(rel_bias, g) for g in range(N_GROUPS)]
    kv_new = [_kv_window(kv, g, min(w, t)) for g, (w, _) in enumerate(B_GROUPS)]
    if kv_state is None:
        new_bufs = kv_new
        band_bias = jnp.stack([_band_bias(bi) for bi in biases])
    else:
        bias_state, bias_new = zip(*[_sample_bias(bi, w, d, t) for bi, (w, d) in zip(biases, B_GROUPS)])

    x3 = x2.reshape(b, t, d_model)
    for i in range(DEPTH - N_A_LAYERS):
        l = N_A_LAYERS + i
        q, gates = _b_proj(xb.reshape(m, d_model), b_w_in, i)
        q, gates = q.reshape(b, t, -1), gates.reshape(b, t, -1)
        if kv_state is None:
            o = _b_attn(q, gates, kv, band_bias)
        else:
            o, updated = _samp_attn(q, gates, kv_state, kv_new, bias_state, bias_new, emit_state=(i == 0))
            if i == 0:
                new_bufs = updated
        x3, xb = _b_out(o, gates, mem_kv, l, b_w_out, i, x3, ln_g[l], ln_b[l], emit_bf16=l < DEPTH - 1)
    new_bufs = [nb.reshape(b, -1, 2, HEADS_PER_GROUP, HEAD_DIM) for nb in new_bufs]
    return x3, jnp.stack(new_conv), new_bufs


def kernel(x_prompt, x_sample, state_conv, state_kv_g0, state_kv_g1, state_kv_g2, cache_mem_kv, mem_prompt,
           a_w_in, a_w_dw, a_b_dw, a_cn_g, a_cn_b, a_w_out, b_w_in, b_w_out, w_kv_shared, w_mem_kv,
           rel_bias, ln_g, ln_b):
    bp = x_prompt.shape[0]
    bs = x_sample.shape[0]
    mem_p = _mem_proj(mem_prompt.reshape(bp * N_MEM, D_MODEL), w_mem_kv).reshape(
        DEPTH, bp, N_MEM * KV_SLOTS, HEAD_DIM)
    new_mem_kv = mem_p.reshape(DEPTH, bp, N_MEM, 2, MEM_HEADS, HEAD_DIM)
    conv_zero = jnp.zeros((N_A_LAYERS, bp, CONV_WIDTH - 1, CONV_CH), x_prompt.dtype)
    weights = (a_w_in, a_w_dw, a_b_dw, a_cn_g, a_cn_b, a_w_out, b_w_in, b_w_out, w_kv_shared, rel_bias, ln_g, ln_b)

    y_p, conv_p, bufs_p = _trunk(x_prompt, conv_zero, mem_p, None, *weights)
    kv_state = [s.reshape(bs, s.shape[1] * KV_SLOTS, HEAD_DIM) for s in (state_kv_g0, state_kv_g1, state_kv_g2)]
    mem_s = cache_mem_kv.reshape(DEPTH, bs, N_MEM * KV_SLOTS, HEAD_DIM)
    y_s, conv_s, bufs_s = _trunk(x_sample, state_conv, mem_s, kv_state, *weights)

    return (y_p, y_s, conv_p, conv_s, bufs_p[0], bufs_s[0], bufs_p[1], bufs_s[1], bufs_p[2], bufs_s[2], new_mem_kv)
```

```python
import functools
import math

import jax
import jax.numpy as jnp
from jax.experimental import pallas as pl
from jax.experimental.pallas import tpu as pltpu

D_MODEL = 2048
DEPTH = 4
HEAD_DIM = 128
MEM_HEADS = 4
MEM_WIDTH = MEM_HEADS * HEAD_DIM
N_MEM = 256
MIX_WIDTH = D_MODEL - MEM_WIDTH
N_A_LAYERS = DEPTH // 2
CONV_CH = MIX_WIDTH
CONV_WIDTH = 31
B_GROUPS = ((128, 1), (512, 4), (2048, 16))
N_GROUPS = len(B_GROUPS)
HEADS_PER_GROUP = 4
B_HEADS = N_GROUPS * HEADS_PER_GROUP
GROUP_WIDTH = HEADS_PER_GROUP * HEAD_DIM
KV_SLOTS = 2 * HEADS_PER_GROUP
B_OUT = GROUP_WIDTH
REL_BUCKETS = 32
REL_MAX_DIST = 2048
BAND = 128
ATTN_TOKENS = BAND * max(d for _, d in B_GROUPS)
LN_EPS = 1e-5
ALPHA = (2 * DEPTH) ** 0.25
NEG_INF = -1e30
SCALE = HEAD_DIM ** -0.5

HALO = 32
VMEM_LIMIT = 56 * 1024 * 1024
BF16 = jnp.bfloat16
F32 = jnp.float32


def _params(*sem):
    return pltpu.CompilerParams(dimension_semantics=sem, vmem_limit_bytes=VMEM_LIMIT)


def _sigmoid(x):
    return 1.0 / (1.0 + jnp.exp(-x))


def _silu(x):
    return x * _sigmoid(x)


def _layernorm(z, g, b):
    mu = jnp.mean(z, -1, keepdims=True)
    zc = z - mu
    var = jnp.mean(zc * zc, -1, keepdims=True)
    return zc * jax.lax.rsqrt(var + LN_EPS) * g + b


DOT_CHUNK = 512


PROJ_ROWS = 2048


def _project(x_ref, w_ref, emit):
    wb = w_ref[...].astype(BF16)
    step = min(x_ref.shape[0], DOT_CHUNK)
    for r in range(0, x_ref.shape[0], step):
        emit(slice(r, r + step), jnp.dot(x_ref[r:r + step, :], wb, preferred_element_type=F32))


def _mm_kernel(x_ref, w_ref, o_ref):
    def emit(rows, y):
        o_ref[rows, :] = y
    _project(x_ref, w_ref, emit)


def _mm(x, w, *, tn=512):
    m, k = x.shape
    n = w.shape[1]
    tm = min(m, PROJ_ROWS)
    return pl.pallas_call(
        _mm_kernel,
        grid=(m // tm, n // tn),
        in_specs=[pl.BlockSpec((tm, k), lambda i, j: (i, 0)),
                  pl.BlockSpec((k, tn), lambda i, j: (0, j))],
        out_specs=pl.BlockSpec((tm, tn), lambda i, j: (i, j)),
        out_shape=jax.ShapeDtypeStruct((m, n), F32),
        compiler_params=_params("arbitrary", "arbitrary"),
        name="mm",
    )(x, w)


B_GATE_STEP = MIX_WIDTH // GROUP_WIDTH


def _b_proj_kernel(x_ref, w_ref, q_ref, g_ref):
    j = pl.program_id(1)

    @pl.when(j < B_GATE_STEP)
    def _():
        def emit(rows, y):
            q_ref[rows, :] = y
        _project(x_ref, w_ref, emit)

    @pl.when(j == B_GATE_STEP + 1)
    def _():
        def emit(rows, y):
            g_ref[rows, :] = y.astype(BF16)
        _project(x_ref, w_ref, emit)

    @pl.when((j == B_GATE_STEP) | (j == B_GATE_STEP + 2))
    def _():
        def emit(rows, y):
            g_ref[rows, :] = _silu(y).astype(BF16)
        _project(x_ref, w_ref, emit)


def _b_proj(x, w, layer):
    m, k = x.shape
    tm = min(m, PROJ_ROWS)
    tn = GROUP_WIDTH
    return pl.pallas_call(
        _b_proj_kernel,
        grid=(m // tm, B_GATE_STEP + 3),
        in_specs=[pl.BlockSpec((tm, k), lambda i, j: (i, 0)),
                  pl.BlockSpec((None, k, tn), lambda i, j: (layer, 0, j))],
        out_specs=[pl.BlockSpec((tm, tn), lambda i, j: (i, jnp.minimum(j, B_GATE_STEP - 1))),
                   pl.BlockSpec((tm, tn), lambda i, j: (i, jnp.maximum(j - B_GATE_STEP, 0)))],
        out_shape=[jax.ShapeDtypeStruct((m, MIX_WIDTH), F32),
                   jax.ShapeDtypeStruct((m, B_OUT + 2 * MEM_WIDTH), BF16)],
        compiler_params=_params("arbitrary", "arbitrary"),
        name="b_proj",
    )(x, w)


def _store_kv_slots(o_ref, lead, slot0, vals):
    rows = vals.shape[0]
    for j in range(vals.shape[1] // HEAD_DIM):
        o_ref[lead + (pl.ds(slot0 + j, rows, stride=KV_SLOTS), slice(None))] = vals[:, j * HEAD_DIM:(j + 1) * HEAD_DIM]


def _mem_proj_kernel(x_ref, w_ref, o_ref, xb_ref):
    @pl.when(pl.program_id(0) == 0)
    def _():
        xb_ref[...] = x_ref[...].astype(BF16)

    _store_kv_slots(o_ref, (), 0, jnp.dot(xb_ref[...], w_ref[...].astype(BF16), preferred_element_type=F32))


def _mem_proj(x, w):
    m, k = x.shape
    nl, _, n = w.shape
    return pl.pallas_call(
        _mem_proj_kernel,
        grid=(nl,),
        in_specs=[pl.BlockSpec((m, k), lambda l: (0, 0)),
                  pl.BlockSpec((None, k, n), lambda l: (l, 0, 0))],
        out_specs=pl.BlockSpec((None, m * KV_SLOTS, HEAD_DIM), lambda l: (l, 0, 0)),
        out_shape=jax.ShapeDtypeStruct((nl, m * KV_SLOTS, HEAD_DIM), F32),
        scratch_shapes=[pltpu.VMEM((m, k), BF16)],
        compiler_params=_params("arbitrary"),
        name="mem_proj",
    )(x, w)


def _kv_window_kernel(k_ref, v_ref, o_ref):
    _store_kv_slots(o_ref, (0,), 0, k_ref[0])
    _store_kv_slots(o_ref, (0,), HEADS_PER_GROUP, v_ref[0])


def _kv_window(kv, g, w):
    b, t, _ = kv.shape
    rows = min(w, 512)
    first = (t - w) // rows
    return pl.pallas_call(
        _kv_window_kernel,
        grid=(b, w // rows),
        in_specs=[pl.BlockSpec((1, rows, GROUP_WIDTH), lambda bi, i: (bi, first + i, g)),
                  pl.BlockSpec((1, rows, GROUP_WIDTH), lambda bi, i: (bi, first + i, N_GROUPS + g))],
        out_specs=pl.BlockSpec((1, rows * KV_SLOTS, HEAD_DIM), lambda bi, i: (bi, i, 0)),
        out_shape=jax.ShapeDtypeStruct((b, w * KV_SLOTS, HEAD_DIM), F32),
        compiler_params=_params("arbitrary", "arbitrary"),
        name="kv_window",
    )(kv, kv)


A_TN = 512
N_GLU = CONV_CH // A_TN
QM_STEP = 3 * N_GLU


def _a_proj_kernel(x_ref, w_ref, u_ref, g_ref, ga_ref):
    j = pl.program_id(1)
    project = functools.partial(_project, x_ref, w_ref)
    is_glu = j < 2 * N_GLU

    @pl.when(is_glu & (j % 2 == 0))
    def _():
        def emit(rows, y):
            ga_ref[rows, :] = y
        project(emit)

    @pl.when(is_glu & (j % 2 == 1))
    def _():
        def emit(rows, y):
            u_ref[rows, :] = ga_ref[rows, :] * _sigmoid(y)
        project(emit)

    @pl.when(j == QM_STEP)
    def _():
        def emit(rows, y):
            g_ref[rows, :] = y.astype(BF16)
        project(emit)

    @pl.when(jnp.logical_not(is_glu) & (j != QM_STEP))
    def _():
        def emit(rows, y):
            g_ref[rows, :] = _silu(y).astype(BF16)
        project(emit)


def _a_proj(x, w, layer):
    m, k = x.shape
    tm = min(m, PROJ_ROWS)
    n_steps = 3 * N_GLU + 2 * MEM_WIDTH // A_TN

    def w_col(j):
        return jnp.where(j < 2 * N_GLU, (j % 2) * N_GLU + j // 2, j)

    return pl.pallas_call(
        _a_proj_kernel,
        grid=(m // tm, n_steps),
        in_specs=[pl.BlockSpec((tm, k), lambda i, j: (i, 0)),
                  pl.BlockSpec((None, k, A_TN), lambda i, j: (layer, 0, w_col(j)))],
        out_specs=[pl.BlockSpec((tm, A_TN), lambda i, j: (i, jnp.minimum(j // 2, N_GLU - 1))),
                   pl.BlockSpec((tm, A_TN), lambda i, j: (i, jnp.maximum(j - 2 * N_GLU, 0)))],
        out_shape=[jax.ShapeDtypeStruct((m, CONV_CH), F32),
                   jax.ShapeDtypeStruct((m, CONV_CH + 2 * MEM_WIDTH), BF16)],
        scratch_shapes=[pltpu.VMEM((tm, A_TN), F32)],
        compiler_params=_params("arbitrary", "arbitrary"),
        name="a_proj",
    )(x, w)


def _out_ln_kernel(cm_ref, w_ref, x_ref, g_ref, b_ref, o_ref, ob_ref, wb_ref):
    @pl.when(pl.program_id(0) == 0)
    def _():
        wb_ref[...] = w_ref[...].astype(BF16)

    cm = cm_ref[...]
    for c in range(0, o_ref.shape[1], DOT_CHUNK):
        cols = slice(c, c + DOT_CHUNK)
        o_ref[:, cols] = ALPHA * x_ref[:, cols] + jnp.dot(cm, wb_ref[:, cols], preferred_element_type=F32)
    out = _layernorm(o_ref[...], g_ref[...], b_ref[...])
    o_ref[...] = out
    ob_ref[...] = out.astype(BF16)


def _out_ln(cm, w, layer, x, g, b):
    m, kk = cm.shape
    d = w.shape[2]
    tm = min(m, 512)
    row = pl.BlockSpec((tm, d), lambda i: (i, 0))
    return pl.pallas_call(
        _out_ln_kernel,
        grid=(m // tm,),
        in_specs=[pl.BlockSpec((tm, kk), lambda i: (i, 0)),
                  pl.BlockSpec((None, kk, d), lambda i: (layer, 0, 0), pipeline_mode=pl.Buffered(1)),
                  row,
                  pl.BlockSpec((1, d), lambda i: (0, 0)),
                  pl.BlockSpec((1, d), lambda i: (0, 0))],
        out_specs=[row, row],
        out_shape=[jax.ShapeDtypeStruct((m, d), F32), jax.ShapeDtypeStruct((m, d), BF16)],
        scratch_shapes=[pltpu.VMEM((kk, d), BF16)],
        compiler_params=_params("arbitrary"),
        name="out_ln",
    )(cm, w, x, g.reshape(1, d), b.reshape(1, d))


def _kv_rows(ref, lead, slot, n):
    return ref[lead + (pl.ds(slot, n, stride=KV_SLOTS), slice(None))]


def _mem_attn(qm, mem_ref, entry=0):
    outs = []
    for h in range(MEM_HEADS):
        lo = h * HEAD_DIM
        q = qm[:, lo:lo + HEAD_DIM]
        k = _kv_rows(mem_ref, (entry,), h, N_MEM).astype(BF16)
        v = _kv_rows(mem_ref, (entry,), MEM_HEADS + h, N_MEM).astype(BF16)
        s = jax.lax.dot_general(q, k, (((1,), (1,)), ((), ())), preferred_element_type=F32) * SCALE
        s = s - jnp.max(s, -1, keepdims=True)
        e = jnp.exp(s)
        p = e / jnp.sum(e, -1, keepdims=True)
        outs.append(jnp.dot(p.astype(BF16), v, preferred_element_type=F32))
    return jnp.concatenate(outs, -1)


def _a_mix_kernel(*refs, tt, tr, has_halo):
    if has_halo:
        (u_ref, sgate, qm, sgm, u_prev, cprev, mem, wdw, bdw, cng, cnb,
         cm_out, nc_out, ext, wrep, conv) = refs
    else:
        (u_ref, sgate, qm, sgm, cprev, mem, wdw, bdw, cng, cnb,
         cm_out, nc_out, ext, wrep, conv) = refs
    i = pl.program_id(1)
    n_blk = CONV_CH // HEAD_DIM

    @pl.when((pl.program_id(0) == 0) & (i == 0))
    def _():
        for k in range(CONV_WIDTH):
            wrep[k] = jnp.broadcast_to(wdw[k:k + 1, :], (8, CONV_CH))

    for c in range(n_blk):
        lanes = slice(c * HEAD_DIM, (c + 1) * HEAD_DIM)
        if has_halo:
            ext[c, 0:HALO, :] = jnp.where(i == 0, cprev[0, :, lanes], u_prev[0, :, lanes])
        else:
            ext[c, 0:HALO, :] = cprev[0, :, lanes]
        ext[c, HALO:HALO + tt, :] = u_ref[0, :, lanes]
        nc_out[0, :, lanes] = ext[c, HALO + tt - (CONV_WIDTH - 1):HALO + tt, :]

    off0 = HALO - (CONV_WIDTH - 1)
    for c in range(n_blk):
        lanes = slice(c * HEAD_DIM, (c + 1) * HEAD_DIM)

        def body(rc, carry, c=c, lanes=lanes):
            r0 = pl.multiple_of(rc * tr, 8)
            acc = jnp.zeros((tr // 8, 8, HEAD_DIM), F32)
            for k in range(CONV_WIDTH):
                win = ext[c, pl.ds(r0 + k + off0, tr), :].reshape(tr // 8, 8, HEAD_DIM)
                acc = acc + win * wrep[k, :, lanes][None]
            conv[pl.ds(r0, tr), lanes] = acc.reshape(tr, HEAD_DIM) + bdw[:, lanes]
            return carry

        jax.lax.fori_loop(0, tt // tr, body, 0)

    zn = _layernorm(conv[...], cng[...], cnb[...])
    cm_out[0, :, 0:CONV_CH] = (_silu(zn) * sgate[0].astype(F32)).astype(BF16)
    cm_out[0, :, CONV_CH:D_MODEL] = (_mem_attn(qm[0], mem) * sgm[0].astype(F32)).astype(BF16)


def _a_mix(u, gates, conv_prev, mem, layer, w_dw, b_dw, cn_g, cn_b):
    b, t, _ = u.shape
    tt = min(t, 512)
    tr = min(tt, 256)
    nt = t // tt
    has_halo = nt > 1
    cprev = jnp.pad(conv_prev, ((0, 0), (HALO - (CONV_WIDTH - 1), 0), (0, 0)))
    c3 = CONV_CH
    qcol = CONV_CH // MEM_WIDTH

    in_specs = [pl.BlockSpec((1, tt, c3), lambda bi, i: (bi, i, 0)),
                pl.BlockSpec((1, tt, c3), lambda bi, i: (bi, i, 0)),
                pl.BlockSpec((1, tt, MEM_WIDTH), lambda bi, i: (bi, i, qcol)),
                pl.BlockSpec((1, tt, MEM_WIDTH), lambda bi, i: (bi, i, qcol + 1))]
    args = [u, gates, gates, gates]
    if has_halo:
        per = tt // HALO
        in_specs += [pl.BlockSpec((1, HALO, c3), lambda bi, i: (bi, jnp.maximum(i * per - 1, 0), 0))]
        args += [u]
    in_specs += [pl.BlockSpec((1, HALO, c3), lambda bi, i: (bi, 0, 0)),
                 pl.BlockSpec((None, 1, N_MEM * KV_SLOTS, HEAD_DIM), lambda bi, i: (layer, bi, 0, 0)),
                 pl.BlockSpec((CONV_WIDTH, c3), lambda bi, i: (0, 0)),
                 pl.BlockSpec((1, c3), lambda bi, i: (0, 0)),
                 pl.BlockSpec((1, c3), lambda bi, i: (0, 0)),
                 pl.BlockSpec((1, c3), lambda bi, i: (0, 0))]
    args += [cprev, mem, w_dw, b_dw.reshape(1, c3), cn_g.reshape(1, c3), cn_b.reshape(1, c3)]

    return pl.pallas_call(
        functools.partial(_a_mix_kernel, tt=tt, tr=tr, has_halo=has_halo),
        grid=(b, nt),
        in_specs=in_specs,
        out_specs=[pl.BlockSpec((1, tt, D_MODEL), lambda bi, i: (bi, i, 0)),
                   pl.BlockSpec((1, CONV_WIDTH - 1, c3), lambda bi, i: (bi, 0, 0))],
        out_shape=[jax.ShapeDtypeStruct((b, t, D_MODEL), BF16),
                   jax.ShapeDtypeStruct((b, CONV_WIDTH - 1, c3), F32)],
        scratch_shapes=[pltpu.VMEM((c3 // HEAD_DIM, tt + HALO, HEAD_DIM), F32),
                        pltpu.VMEM((CONV_WIDTH, 8, c3), F32),
                        pltpu.VMEM((tt, c3), F32)],
        compiler_params=_params("arbitrary", "arbitrary"),
        name="a_mix",
    )(*args)


def _rows(ref, start, size, stride):
    if stride == 1:
        return ref[0, start:start + size, :]
    return ref[0, pl.ds(start, size, stride=stride), :]


def _b_attn_kernel(q0, q1, q2, k0, k1, k2, v0, v1, v2, hk0, hk1, hk2, hv0, hv1, hv2, bias_ref, gate_ref,
                   o_ref, num_scr, m_scr, s_scr):
    first = jnp.minimum(pl.program_id(2), 1)
    qs, ks, vs = (q0, q1, q2), (k0, k1, k2), (v0, v1, v2)
    hks, hvs = (hk0, hk1, hk2), (hv0, hv1, hv2)
    dn = (((1,), (1,)), ((), ()))
    for g, (_, d) in enumerate(B_GROUPS):
        for r in range(d):
            for s in range(ATTN_TOKENS // (BAND * d)):
                start = r + d * BAND * s
                q = _rows(qs[g], start, BAND, d)
                if s == 0:
                    k = jnp.concatenate([_rows(hks[g], r, BAND, d), _rows(ks[g], r, BAND, d)], 0)
                    v = jnp.concatenate([_rows(hvs[g], r, BAND, d), _rows(vs[g], r, BAND, d)], 0)
                    bias = bias_ref[g, first, 0]
                else:
                    k = _rows(ks[g], start - d * BAND, 2 * BAND, d)
                    v = _rows(vs[g], start - d * BAND, 2 * BAND, d)
                    bias = bias_ref[g, 1, 0]
                logits = jax.lax.dot_general(q.astype(BF16), k.astype(BF16), dn, preferred_element_type=F32)
                logits = logits * SCALE + bias
                m = jnp.max(logits, -1, keepdims=True)
                e = jnp.exp(logits - m)
                ssum = jnp.sum(e, -1, keepdims=True)
                num = jnp.dot(e.astype(BF16), v.astype(BF16), preferred_element_type=F32)
                if d == 1:
                    dst = pl.ds(start, BAND)
                else:
                    dst = pl.ds(start, BAND, stride=d)
                num_scr[g, dst, :] = num
                m_scr[g, dst, :] = jnp.broadcast_to(m, (BAND, HEAD_DIM))
                s_scr[g, dst, :] = jnp.broadcast_to(ssum, (BAND, HEAD_DIM))

    def merge(c, carry):
        sl = pl.ds(pl.multiple_of(c * BAND, BAND), BAND)
        ms = [m_scr[g, sl, :] for g in range(N_GROUPS)]
        m_all = jnp.maximum(jnp.maximum(ms[0], ms[1]), ms[2])
        coef = [jnp.exp(m - m_all) for m in ms]
        num = coef[0] * num_scr[0, sl, :]
        den = coef[0] * s_scr[0, sl, :]
        for g in range(1, N_GROUPS):
            num = num + coef[g] * num_scr[g, sl, :]
            den = den + coef[g] * s_scr[g, sl, :]
        o_ref[0, sl, :] = (num / den * gate_ref[0, sl, :].astype(F32)).astype(BF16)
        return carry

    jax.lax.fori_loop(0, ATTN_TOKENS // BAND, merge, 0)


def _b_attn(q, gates, kv, band_bias):
    b, t, _ = q.shape
    tb = ATTN_TOKENS
    blk = (1, tb, HEAD_DIM)

    def cur(col0):
        return lambda bi, hi, i: (bi, i, col0 + hi)

    def halo_spec(d, col0):
        rows = BAND * d
        per = tb // rows
        return pl.BlockSpec((1, rows, HEAD_DIM), lambda bi, hi, i: (bi, jnp.maximum(i * per - 1, 0), col0 + hi))

    hpg = HEADS_PER_GROUP
    in_specs = ([pl.BlockSpec(blk, cur(hpg * g)) for g in range(N_GROUPS)]
                + [pl.BlockSpec(blk, cur(hpg * g)) for g in range(N_GROUPS)]
                + [pl.BlockSpec(blk, cur(B_HEADS + hpg * g)) for g in range(N_GROUPS)]
                + [halo_spec(d, hpg * g) for g, (_, d) in enumerate(B_GROUPS)]
                + [halo_spec(d, B_HEADS + hpg * g) for g, (_, d) in enumerate(B_GROUPS)]
                + [pl.BlockSpec((N_GROUPS, 2, 1, BAND, 2 * BAND), lambda bi, hi, i: (0, 0, hi, 0, 0)),
                   pl.BlockSpec(blk, cur(0))])
    return pl.pallas_call(
        _b_attn_kernel,
        grid=(b, hpg, t // tb),
        in_specs=in_specs,
        out_specs=pl.BlockSpec(blk, lambda bi, hi, i: (bi, i, hi)),
        out_shape=jax.ShapeDtypeStruct((b, t, B_OUT), BF16),
        scratch_shapes=[pltpu.VMEM((N_GROUPS, tb, HEAD_DIM), F32)] * 3,
        compiler_params=_params("parallel", "parallel", "arbitrary"),
        name="b_attn",
    )(q, q, q, kv, kv, kv, kv, kv, kv, kv, kv, kv, kv, kv, kv, band_bias, gates)


def _b_out_kernel(o_ref, qm_ref, gm_ref, mem_ref, w_ref, x_ref, g_ref, b_ref, y_ref, *rest, nb, tt):
    yb_ref, wb_ref = rest if len(rest) == 2 else (None, rest[0])

    @pl.when((pl.program_id(0) == 0) & (pl.program_id(1) == 0))
    def _():
        wb_ref[...] = w_ref[...].astype(BF16)

    stack = lambda parts: parts[0] if nb == 1 else jnp.concatenate(parts, 0)
    mbranch = stack([_mem_attn(qm_ref[bi], mem_ref, bi) * gm_ref[bi].astype(F32) for bi in range(nb)])
    if nb == 1:
        obranch = o_ref[0]
    else:
        obranch = jnp.concatenate([o_ref[bi].astype(F32) for bi in range(nb)], 0).astype(BF16)
    y = jnp.dot(obranch, wb_ref[0:B_OUT, :], preferred_element_type=F32)
    y = y + jnp.dot(mbranch.astype(BF16), wb_ref[B_OUT:B_OUT + MEM_WIDTH, :], preferred_element_type=F32)
    x = stack([x_ref[bi] for bi in range(nb)])
    out = _layernorm(ALPHA * x + y, g_ref[...], b_ref[...])
    for bi in range(nb):
        y_ref[bi] = out[bi * tt:(bi + 1) * tt]
        if yb_ref is not None:
            yb_ref[bi] = out[bi * tt:(bi + 1) * tt].astype(BF16)


def _b_out(o, gates, mem, layer, w, wl, x, g, b, emit_bf16):
    bsz, t, d = x.shape
    tt = min(t, 512)
    nb = max(1, min(bsz, 64 // tt))
    kk = w.shape[1]
    qcol = B_OUT // MEM_WIDTH
    row = pl.BlockSpec((nb, tt, d), lambda bi, i: (bi, i, 0))
    out_dtypes = [F32, BF16] if emit_bf16 else [F32]
    res = pl.pallas_call(
        functools.partial(_b_out_kernel, nb=nb, tt=tt),
        grid=(bsz // nb, t // tt),
        in_specs=[pl.BlockSpec((nb, tt, B_OUT), lambda bi, i: (bi, i, 0)),
                  pl.BlockSpec((nb, tt, MEM_WIDTH), lambda bi, i: (bi, i, qcol)),
                  pl.BlockSpec((nb, tt, MEM_WIDTH), lambda bi, i: (bi, i, qcol + 1)),
                  pl.BlockSpec((None, nb, N_MEM * KV_SLOTS, HEAD_DIM), lambda bi, i: (layer, bi, 0, 0)),
                  pl.BlockSpec((None, kk, d), lambda bi, i: (wl, 0, 0), pipeline_mode=pl.Buffered(1)),
                  row,
                  pl.BlockSpec((1, d), lambda bi, i: (0, 0)),
                  pl.BlockSpec((1, d), lambda bi, i: (0, 0))],
        out_specs=[row] * len(out_dtypes),
        out_shape=[jax.ShapeDtypeStruct((bsz, t, d), dt) for dt in out_dtypes],
        scratch_shapes=[pltpu.VMEM((kk, d), BF16)],
        compiler_params=_params("arbitrary", "arbitrary"),
        name="b_out",
    )(o, gates, gates, mem, w, x, g.reshape(1, d), b.reshape(1, d))
    return res if emit_bf16 else (res[0], None)


def _samp_attn_kernel(*refs, nq, widths, emit_state):
    ng = N_GROUPS
    q_ref, gate_ref = refs[0], refs[1]
    st_refs, new_refs = refs[2:2 + ng], refs[2 + ng:2 + 2 * ng]
    bs_refs, bn_refs = refs[2 + 2 * ng:2 + 3 * ng], refs[2 + 3 * ng:2 + 4 * ng]
    o_ref = refs[2 + 4 * ng]
    dn = (((1,), (1,)), ((), ()))
    for h in range(HEADS_PER_GROUP):
        ms, sums, nums = [], [], []
        for g in range(ng):
            lo = (g * HEADS_PER_GROUP + h) * HEAD_DIM
            q = q_ref[0, :, lo:lo + HEAD_DIM].astype(BF16)
            ks = _kv_rows(st_refs[g], (0,), h, widths[g]).astype(BF16)
            vs = _kv_rows(st_refs[g], (0,), HEADS_PER_GROUP + h, widths[g]).astype(BF16)
            kn = _kv_rows(new_refs[g], (0,), h, nq).astype(BF16)
            vn = _kv_rows(new_refs[g], (0,), HEADS_PER_GROUP + h, nq).astype(BF16)
            ls = jax.lax.dot_general(q, ks, dn, preferred_element_type=F32) * SCALE + bs_refs[g][h]
            ln = jax.lax.dot_general(q, kn, dn, preferred_element_type=F32) * SCALE + bn_refs[g][h]
            m = jnp.maximum(jnp.max(ls, -1, keepdims=True), jnp.max(ln, -1, keepdims=True))
            es = jnp.exp(ls - m)
            en = jnp.exp(ln - m)
            ms.append(m)
            sums.append(jnp.sum(es, -1, keepdims=True) + jnp.sum(en, -1, keepdims=True))
            nums.append(jnp.dot(es.astype(BF16), vs, preferred_element_type=F32)
                        + jnp.dot(en.astype(BF16), vn, preferred_element_type=F32))
        m_all = jnp.maximum(jnp.maximum(ms[0], ms[1]), ms[2])
        coef = [jnp.exp(m - m_all) for m in ms]
        num = coef[0] * nums[0]
        den = coef[0] * sums[0]
        for g in range(1, ng):
            num = num + coef[g] * nums[g]
            den = den + coef[g] * sums[g]
        lo = h * HEAD_DIM
        o_ref[0, :, lo:lo + HEAD_DIM] = (num / den * gate_ref[0, :, lo:lo + HEAD_DIM].astype(F32)).astype(BF16)

    if emit_state:
        for g in range(ng):
            out = refs[3 + 4 * ng + g]
            rows, shift = widths[g] * KV_SLOTS, nq * KV_SLOTS
            out[0, 0:rows - shift, :] = st_refs[g][0, shift:rows, :]
            out[0, rows - shift:rows, :] = new_refs[g][0]


def _samp_attn(q, gates, states, news, bias_state, bias_new, emit_state):
    b, nq, _ = q.shape
    widths = tuple(s.shape[1] // KV_SLOTS for s in states)
    whole = lambda a: pl.BlockSpec((1,) + a.shape[1:], lambda i: (i, 0, 0))
    const = lambda a: pl.BlockSpec(a.shape, lambda i: (0, 0, 0))
    out_spec = pl.BlockSpec((1, nq, GROUP_WIDTH), lambda i: (i, 0, 0))
    out_shape = jax.ShapeDtypeStruct((b, nq, GROUP_WIDTH), BF16)
    res = pl.pallas_call(
        functools.partial(_samp_attn_kernel, nq=nq, widths=widths, emit_state=emit_state),
        grid=(b,),
        in_specs=([pl.BlockSpec((1, nq, MIX_WIDTH), lambda i: (i, 0, 0)),
                   pl.BlockSpec((1, nq, GROUP_WIDTH), lambda i: (i, 0, 0))]
                  + [whole(s) for s in states] + [whole(n) for n in news]
                  + [const(a) for a in bias_state] + [const(a) for a in bias_new]),
        out_specs=[out_spec] + ([whole(s) for s in states] if emit_state else []),
        out_shape=[out_shape] + ([jax.ShapeDtypeStruct(s.shape, s.dtype) for s in states] if emit_state else []),
        compiler_params=_params("arbitrary"),
        name="samp_attn",
    )(q, gates, *states, *news, *bias_state, *bias_new)
    return res[0], list(res[1:])


def _t5_bucket(dist):
    max_exact = REL_BUCKETS // 2
    safe = jnp.maximum(dist, 1).astype(F32)
    large = max_exact + (jnp.log(safe / max_exact) / math.log(REL_MAX_DIST / max_exact)
                         * (REL_BUCKETS - max_exact)).astype(jnp.int32)
    large = jnp.minimum(large, REL_BUCKETS - 1)
    return jnp.where(dist < max_exact, dist, large)


def _group_bias(rel_bias, g):
    w, d = B_GROUPS[g]
    dist = d * jnp.arange(w // d + 1, dtype=jnp.int32)
    tab = rel_bias[_t5_bucket(dist)]
    return tab[:, g * HEADS_PER_GROUP:(g + 1) * HEADS_PER_GROUP].T.astype(F32)


def _toeplitz(vec, rows, cols):
    hh, p = vec.shape
    flat = jnp.tile(vec, (1, rows))[:, :rows * (p - 1)]
    return flat.reshape(hh, rows, p - 1)[:, :, :cols]


def _band_bias(bias):
    period = 3 * BAND
    vec = jnp.full((bias.shape[0], period), NEG_INF, F32)
    vec = jax.lax.dynamic_update_slice(vec, bias[:, ::-1], (0, 0))
    mat = _toeplitz(vec, BAND, 2 * BAND)
    c = jnp.arange(2 * BAND, dtype=jnp.int32)[None, None, :]
    first = jnp.where(c >= BAND, mat, NEG_INF)
    return jnp.stack([first, mat])


def _sample_bias(bias, w, d, nq):
    hh = bias.shape[0]
    neg = jnp.full_like(bias, NEG_INF)
    pad = jnp.full((hh, nq), NEG_INF, F32)
    rev_dist = jnp.stack([neg] * (d - 1) + [bias[:, ::-1]], -1).reshape(hh, -1)
    bs = _toeplitz(jnp.concatenate([rev_dist[:, d - 1:d - 1 + w], pad], 1), nq, w)
    bn = _toeplitz(jnp.concatenate([bias[:, 0:1], pad, rev_dist[:, w + d - nq:w + d - 1]], 1), nq, nq)
    return bs, bn


def _trunk(x, conv_prev, mem_kv, kv_state, a_w_in, a_w_dw, a_b_dw, a_cn_g, a_cn_b, a_w_out,
           b_w_in, b_w_out, w_kv_shared, rel_bias, ln_g, ln_b):
    b, t, d_model = x.shape
    m = b * t
    x2 = x.reshape(m, d_model)
    xb = x2.astype(BF16)
    new_conv = []
    for l in range(N_A_LAYERS):
        u, gates = _a_proj(xb, a_w_in, l)
        cm, nc = _a_mix(u.reshape(b, t, -1), gates.reshape(b, t, -1), conv_prev[l], mem_kv, l,
                        a_w_dw[l], a_b_dw[l], a_cn_g[l], a_cn_b[l])
        new_conv.append(nc)
        x2, xb = _out_ln(cm.reshape(m, d_model), a_w_out, l, x2, ln_g[l], ln_b[l])

    kv = _mm(xb, w_kv_shared).reshape(b, t, 2 * MIX_WIDTH)
    biases = [_group_bias(rel_bias, g) for g in range(N_GROUPS)]
    kv_new = [_kv_window(kv, g, min(w, t)) for g, (w, _) in enumerate(B_GROUPS)]
    if kv_state is None:
        new_bufs = kv_new
        band_bias = jnp.stack([_band_bias(bi) for bi in biases])
    else:
        bias_state, bias_new = zip(*[_sample_bias(bi, w, d, t) for bi, (w, d) in zip(biases, B_GROUPS)])

    x3 = x2.reshape(b, t, d_model)
    for i in range(DEPTH - N_A_LAYERS):
        l = N_A_LAYERS + i
        q, gates = _b_proj(xb.reshape(m, d_model), b_w_in, i)
        q, gates = q.reshape(b, t, -1), gates.reshape(b, t, -1)
        if kv_state is None:
            o = _b_attn(q, gates, kv, band_bias)
        else:
            o, updated = _samp_attn(q, gates, kv_state, kv_new, bias_state, bias_new, emit_state=(i == 0))
            if i == 0:
                new_bufs = updated
        x3, xb = _b_out(o, gates, mem_kv, l, b_w_out, i, x3, ln_g[l], ln_b[l], emit_bf16=l < DEPTH - 1)
    new_bufs = [nb.reshape(b, -1, 2, HEADS_PER_GROUP, HEAD_DIM) for nb in new_bufs]
    return x3, jnp.stack(new_conv), new_bufs


def kernel(x_prompt, x_sample, state_conv, state_kv_g0, state_kv_g1, state_kv_g2, cache_mem_kv, mem_prompt,
           a_w_in, a_w_dw, a_b_dw, a_cn_g, a_cn_b, a_w_out, b_w_in, b_w_out, w_kv_shared, w_mem_kv,
           rel_bias, ln_g, ln_b):
    bp = x_prompt.shape[0]
    bs = x_sample.shape[0]
    mem_p = _mem_proj(mem_prompt.reshape(bp * N_MEM, D_MODEL), w_mem_kv).reshape(
        DEPTH, bp, N_MEM * KV_SLOTS, HEAD_DIM)
    new_mem_kv = mem_p.reshape(DEPTH, bp, N_MEM, 2, MEM_HEADS, HEAD_DIM)
    conv_zero = jnp.zeros((N_A_LAYERS, bp, CONV_WIDTH - 1, CONV_CH), x_prompt.dtype)
    weights = (a_w_in, a_w_dw, a_b_dw, a_cn_g, a_cn_b, a_w_out, b_w_in, b_w_out, w_kv_shared, rel_bias, ln_g, ln_b)

    y_p, conv_p, bufs_p = _trunk(x_prompt, conv_zero, mem_p, None, *weights)
    kv_state = [s.reshape(bs, s.shape[1] * KV_SLOTS, HEAD_DIM) for s in (state_kv_g0, state_kv_g1, state_kv_g2)]
    mem_s = cache_mem_kv.reshape(DEPTH, bs, N_MEM * KV_SLOTS, HEAD_DIM)
    y_s, conv_s, bufs_s = _trunk(x_sample, state_conv, mem_s, kv_state, *weights)

    return (y_p, y_s, conv_p, conv_s, bufs_p[0], bufs_s[0], bufs_p[1], bufs_s[1], bufs_p[2], bufs_s[2], new_mem_kv)
```

```python
import functools
import math

import jax
import jax.numpy as jnp
from jax.experimental import pallas as pl
from jax.experimental.pallas import tpu as pltpu

D_MODEL = 2048
DEPTH = 4
HEAD_DIM = 128
MEM_HEADS = 4
MEM_WIDTH = MEM_HEADS * HEAD_DIM
N_MEM = 256
MIX_WIDTH = D_MODEL - MEM_WIDTH
N_A_LAYERS = DEPTH // 2
CONV_CH = MIX_WIDTH
CONV_WIDTH = 31
B_GROUPS = ((128, 1), (512, 4), (2048, 16))
N_GROUPS = len(B_GROUPS)
HEADS_PER_GROUP = 4
B_HEADS = N_GROUPS * HEADS_PER_GROUP
GROUP_WIDTH = HEADS_PER_GROUP * HEAD_DIM
KV_SLOTS = 2 * HEADS_PER_GROUP
B_OUT = GROUP_WIDTH
REL_BUCKETS = 32
REL_MAX_DIST = 2048
BAND = 128
ATTN_TOKENS = BAND * max(d for _, d in B_GROUPS)
LN_EPS = 1e-5
ALPHA = (2 * DEPTH) ** 0.25
NEG_INF = -1e30
SCALE = HEAD_DIM ** -0.5

HALO = 32
VMEM_LIMIT = 56 * 1024 * 1024
BF16 = jnp.bfloat16
F32 = jnp.float32


def _params(*sem):
    return pltpu.CompilerParams(dimension_semantics=sem, vmem_limit_bytes=VMEM_LIMIT)


def _sigmoid(x):
    return 1.0 / (1.0 + jnp.exp(-x))


def _silu(x):
    return x * _sigmoid(x)


def _layernorm(z, g, b):
    mu = jnp.mean(z, -1, keepdims=True)
    zc = z - mu
    var = jnp.mean(zc * zc, -1, keepdims=True)
    return zc * jax.lax.rsqrt(var + LN_EPS) * g + b


DOT_CHUNK = 512


PROJ_ROWS = 2048


def _project(x_ref, w_ref, emit):
    wb = w_ref[...].astype(BF16)
    step = min(x_ref.shape[0], DOT_CHUNK)
    for r in range(0, x_ref.shape[0], step):
        emit(slice(r, r + step), jnp.dot(x_ref[r:r + step, :], wb, preferred_element_type=F32))


def _mm_kernel(x_ref, w_ref, o_ref):
    def emit(rows, y):
        o_ref[rows, :] = y
    _project(x_ref, w_ref, emit)


def _mm(x, w, *, tn=512):
    m, k = x.shape
    n = w.shape[1]
    tm = min(m, PROJ_ROWS)
    return pl.pallas_call(
        _mm_kernel,
        grid=(m // tm, n // tn),
        in_specs=[pl.BlockSpec((tm, k), lambda i, j: (i, 0)),
                  pl.BlockSpec((k, tn), lambda i, j: (0, j))],
        out_specs=pl.BlockSpec((tm, tn), lambda i, j: (i, j)),
        out_shape=jax.ShapeDtypeStruct((m, n), F32),
        compiler_params=_params("arbitrary", "arbitrary"),
        name="mm",
    )(x, w)


B_GATE_STEP = MIX_WIDTH // GROUP_WIDTH


def _b_proj_kernel(x_ref, w_ref, q_ref, g_ref):
    j = pl.program_id(1)

    @pl.when(j < B_GATE_STEP)
    def _():
        def emit(rows, y):
            q_ref[rows, :] = y
        _project(x_ref, w_ref, emit)

    @pl.when(j == B_GATE_STEP + 1)
    def _():
        def emit(rows, y):
            g_ref[rows, :] = y.astype(BF16)
        _project(x_ref, w_ref, emit)

    @pl.when((j == B_GATE_STEP) | (j == B_GATE_STEP + 2))
    def _():
        def emit(rows, y):
            g_ref[rows, :] = _silu(y).astype(BF16)
        _project(x_ref, w_ref, emit)


def _b_proj(x, w, layer):
    m, k = x.shape
    tm = min(m, PROJ_ROWS)
    tn = GROUP_WIDTH
    return pl.pallas_call(
        _b_proj_kernel,
        grid=(m // tm, B_GATE_STEP + 3),
        in_specs=[pl.BlockSpec((tm, k), lambda i, j: (i, 0)),
                  pl.BlockSpec((None, k, tn), lambda i, j: (layer, 0, j))],
        out_specs=[pl.BlockSpec((tm, tn), lambda i, j: (i, jnp.minimum(j, B_GATE_STEP - 1))),
                   pl.BlockSpec((tm, tn), lambda i, j: (i, jnp.maximum(j - B_GATE_STEP, 0)))],
        out_shape=[jax.ShapeDtypeStruct((m, MIX_WIDTH), F32),
                   jax.ShapeDtypeStruct((m, B_OUT + 2 * MEM_WIDTH), BF16)],
        compiler_params=_params("arbitrary", "arbitrary"),
        name="b_proj",
    )(x, w)


def _store_kv_slots(o_ref, lead, slot0, vals):
    rows = vals.shape[0]
    for j in range(vals.shape[1] // HEAD_DIM):
        o_ref[lead + (pl.ds(slot0 + j, rows, stride=KV_SLOTS), slice(None))] = vals[:, j * HEAD_DIM:(j + 1) * HEAD_DIM]


def _mem_proj_kernel(x_ref, w_ref, o_ref, xb_ref):
    @pl.when(pl.program_id(0) == 0)
    def _():
        xb_ref[...] = x_ref[...].astype(BF16)

    _store_kv_slots(o_ref, (), 0, jnp.dot(xb_ref[...], w_ref[...].astype(BF16), preferred_element_type=F32))


def _mem_proj(x, w):
    m, k = x.shape
    nl, _, n = w.shape
    return pl.pallas_call(
        _mem_proj_kernel,
        grid=(nl,),
        in_specs=[pl.BlockSpec((m, k), lambda l: (0, 0)),
                  pl.BlockSpec((None, k, n), lambda l: (l, 0, 0))],
        out_specs=pl.BlockSpec((None, m * KV_SLOTS, HEAD_DIM), lambda l: (l, 0, 0)),
        out_shape=jax.ShapeDtypeStruct((nl, m * KV_SLOTS, HEAD_DIM), F32),
        scratch_shapes=[pltpu.VMEM((m, k), BF16)],
        compiler_params=_params("arbitrary"),
        name="mem_proj",
    )(x, w)


def _kv_window_kernel(k_ref, v_ref, o_ref):
    _store_kv_slots(o_ref, (0,), 0, k_ref[0])
    _store_kv_slots(o_ref, (0,), HEADS_PER_GROUP, v_ref[0])


def _kv_window(kv, g, w):
    b, t, _ = kv.shape
    rows = min(w, 512)
    first = (t - w) // rows
    return pl.pallas_call(
        _kv_window_kernel,
        grid=(b, w // rows),
        in_specs=[pl.BlockSpec((1, rows, GROUP_WIDTH), lambda bi, i: (bi, first + i, g)),
                  pl.BlockSpec((1, rows, GROUP_WIDTH), lambda bi, i: (bi, first + i, N_GROUPS + g))],
        out_specs=pl.BlockSpec((1, rows * KV_SLOTS, HEAD_DIM), lambda bi, i: (bi, i, 0)),
        out_shape=jax.ShapeDtypeStruct((b, w * KV_SLOTS, HEAD_DIM), F32),
        compiler_params=_params("arbitrary", "arbitrary"),
        name="kv_window",
    )(kv, kv)


A_TN = 512
N_GLU = CONV_CH // A_TN
QM_STEP = 3 * N_GLU


def _a_proj_kernel(x_ref, w_ref, u_ref, g_ref, ga_ref, *xb):
    j = pl.program_id(1)
    if xb:
        @pl.when(j == 0)
        def _():
            xb[0][...] = x_ref[...].astype(BF16)
    project = functools.partial(_project, xb[0] if xb else x_ref, w_ref)
    is_glu = j < 2 * N_GLU

    @pl.when(is_glu & (j % 2 == 0))
    def _():
        def emit(rows, y):
            ga_ref[rows, :] = y
        project(emit)

    @pl.when(is_glu & (j % 2 == 1))
    def _():
        def emit(rows, y):
            u_ref[rows, :] = ga_ref[rows, :] * _sigmoid(y)
        project(emit)

    @pl.when(j == QM_STEP)
    def _():
        def emit(rows, y):
            g_ref[rows, :] = y.astype(BF16)
        project(emit)

    @pl.when(jnp.logical_not(is_glu) & (j != QM_STEP))
    def _():
        def emit(rows, y):
            g_ref[rows, :] = _silu(y).astype(BF16)
        project(emit)


def _a_proj(x, w, layer):
    m, k = x.shape
    tm = min(m, PROJ_ROWS)
    n_steps = 3 * N_GLU + 2 * MEM_WIDTH // A_TN
    if x.dtype == BF16:
        x_spec, cast_scratch = pl.BlockSpec((tm, k), lambda i, j: (i, 0)), []
    else:
        x_spec = pl.BlockSpec((tm, k), lambda i, j: (i, 0), pipeline_mode=pl.Buffered(1))
        cast_scratch = [pltpu.VMEM((tm, k), BF16)]

    def w_col(j):
        return jnp.where(j < 2 * N_GLU, (j % 2) * N_GLU + j // 2, j)

    return pl.pallas_call(
        _a_proj_kernel,
        grid=(m // tm, n_steps),
        in_specs=[x_spec,
                  pl.BlockSpec((None, k, A_TN), lambda i, j: (layer, 0, w_col(j)))],
        out_specs=[pl.BlockSpec((tm, A_TN), lambda i, j: (i, jnp.minimum(j // 2, N_GLU - 1))),
                   pl.BlockSpec((tm, A_TN), lambda i, j: (i, jnp.maximum(j - 2 * N_GLU, 0)))],
        out_shape=[jax.ShapeDtypeStruct((m, CONV_CH), F32),
                   jax.ShapeDtypeStruct((m, CONV_CH + 2 * MEM_WIDTH), BF16)],
        scratch_shapes=[pltpu.VMEM((tm, A_TN), F32)] + cast_scratch,
        compiler_params=_params("arbitrary", "arbitrary"),
        name="a_proj",
    )(x, w)


def _out_ln_kernel(cm_ref, w_ref, x_ref, g_ref, b_ref, o_ref, ob_ref, wb_ref):
    @pl.when(pl.program_id(0) == 0)
    def _():
        wb_ref[...] = w_ref[...].astype(BF16)

    cm = cm_ref[...]
    for c in range(0, o_ref.shape[1], DOT_CHUNK):
        cols = slice(c, c + DOT_CHUNK)
        o_ref[:, cols] = ALPHA * x_ref[:, cols] + jnp.dot(cm, wb_ref[:, cols], preferred_element_type=F32)
    out = _layernorm(o_ref[...], g_ref[...], b_ref[...])
    o_ref[...] = out
    ob_ref[...] = out.astype(BF16)


def _out_ln(cm, w, layer, x, g, b):
    m, kk = cm.shape
    d = w.shape[2]
    tm = min(m, 512)
    row = pl.BlockSpec((tm, d), lambda i: (i, 0))
    return pl.pallas_call(
        _out_ln_kernel,
        grid=(m // tm,),
        in_specs=[pl.BlockSpec((tm, kk), lambda i: (i, 0)),
                  pl.BlockSpec((None, kk, d), lambda i: (layer, 0, 0), pipeline_mode=pl.Buffered(1)),
                  row,
                  pl.BlockSpec((1, d), lambda i: (0, 0)),
                  pl.BlockSpec((1, d), lambda i: (0, 0))],
        out_specs=[row, row],
        out_shape=[jax.ShapeDtypeStruct((m, d), F32), jax.ShapeDtypeStruct((m, d), BF16)],
        scratch_shapes=[pltpu.VMEM((kk, d), BF16)],
        compiler_params=_params("arbitrary"),
        name="out_ln",
    )(cm, w, x, g.reshape(1, d), b.reshape(1, d))


def _kv_rows(ref, lead, slot, n):
    return ref[lead + (pl.ds(slot, n, stride=KV_SLOTS), slice(None))]


def _mem_attn(qm, mem_ref, entry=0):
    outs = []
    for h in range(MEM_HEADS):
        lo = h * HEAD_DIM
        q = qm[:, lo:lo + HEAD_DIM]
        k = _kv_rows(mem_ref, (entry,), h, N_MEM).astype(BF16)
        v = _kv_rows(mem_ref, (entry,), MEM_HEADS + h, N_MEM).astype(BF16)
        s = jax.lax.dot_general(q, k, (((1,), (1,)), ((), ())), preferred_element_type=F32) * SCALE
        s = s - jnp.max(s, -1, keepdims=True)
        e = jnp.exp(s)
        p = e / jnp.sum(e, -1, keepdims=True)
        outs.append(jnp.dot(p.astype(BF16), v, preferred_element_type=F32))
    return jnp.concatenate(outs, -1)


def _a_mix_kernel(*refs, tt, tr, has_halo):
    if has_halo:
        (u_ref, sgate, qm, sgm, u_prev, cprev, mem, wdw, bdw, cng, cnb,
         cm_out, nc_out, ext, wrep, conv) = refs
    else:
        (u_ref, sgate, qm, sgm, cprev, mem, wdw, bdw, cng, cnb,
         cm_out, nc_out, ext, wrep, conv) = refs
    i = pl.program_id(1)
    n_blk = CONV_CH // HEAD_DIM

    @pl.when((pl.program_id(0) == 0) & (i == 0))
    def _():
        for k in range(CONV_WIDTH):
            wrep[k] = jnp.broadcast_to(wdw[k:k + 1, :], (8, CONV_CH))

    for c in range(n_blk):
        lanes = slice(c * HEAD_DIM, (c + 1) * HEAD_DIM)
        if has_halo:
            ext[c, 0:HALO, :] = jnp.where(i == 0, cprev[0, :, lanes], u_prev[0, :, lanes])
        else:
            ext[c, 0:HALO, :] = cprev[0, :, lanes]
        ext[c, HALO:HALO + tt, :] = u_ref[0, :, lanes]
        nc_out[0, :, lanes] = ext[c, HALO + tt - (CONV_WIDTH - 1):HALO + tt, :]

    off0 = HALO - (CONV_WIDTH - 1)
    for c in range(n_blk):
        lanes = slice(c * HEAD_DIM, (c + 1) * HEAD_DIM)

        def body(rc, carry, c=c, lanes=lanes):
            r0 = pl.multiple_of(rc * tr, 8)
            acc = jnp.zeros((tr // 8, 8, HEAD_DIM), F32)
            for k in range(CONV_WIDTH):
                win = ext[c, pl.ds(r0 + k + off0, tr), :].reshape(tr // 8, 8, HEAD_DIM)
                acc = acc + win * wrep[k, :, lanes][None]
            conv[pl.ds(r0, tr), lanes] = acc.reshape(tr, HEAD_DIM) + bdw[:, lanes]
            return carry

        jax.lax.fori_loop(0, tt // tr, body, 0)

    zn = _layernorm(conv[...], cng[...], cnb[...])
    cm_out[0, :, 0:CONV_CH] = (_silu(zn) * sgate[0].astype(F32)).astype(BF16)
    cm_out[0, :, CONV_CH:D_MODEL] = (_mem_attn(qm[0], mem) * sgm[0].astype(F32)).astype(BF16)


def _a_mix(u, gates, conv_prev, mem, layer, w_dw, b_dw, cn_g, cn_b):
    b, t, _ = u.shape
    tt = min(t, 512)
    tr = min(tt, 256)
    nt = t // tt
    has_halo = nt > 1
    cprev = jnp.pad(conv_prev, ((0, 0), (HALO - (CONV_WIDTH - 1), 0), (0, 0)))
    c3 = CONV_CH
    qcol = CONV_CH // MEM_WIDTH

    in_specs = [pl.BlockSpec((1, tt, c3), lambda bi, i: (bi, i, 0)),
                pl.BlockSpec((1, tt, c3), lambda bi, i: (bi, i, 0)),
                pl.BlockSpec((1, tt, MEM_WIDTH), lambda bi, i: (bi, i, qcol)),
                pl.BlockSpec((1, tt, MEM_WIDTH), lambda bi, i: (bi, i, qcol + 1))]
    args = [u, gates, gates, gates]
    if has_halo:
        per = tt // HALO
        in_specs += [pl.BlockSpec((1, HALO, c3), lambda bi, i: (bi, jnp.maximum(i * per - 1, 0), 0))]
        args += [u]
    in_specs += [pl.BlockSpec((1, HALO, c3), lambda bi, i: (bi, 0, 0)),
                 pl.BlockSpec((None, 1, N_MEM * KV_SLOTS, HEAD_DIM), lambda bi, i: (layer, bi, 0, 0)),
                 pl.BlockSpec((CONV_WIDTH, c3), lambda bi, i: (0, 0)),
                 pl.BlockSpec((1, c3), lambda bi, i: (0, 0)),
                 pl.BlockSpec((1, c3), lambda bi, i: (0, 0)),
                 pl.BlockSpec((1, c3), lambda bi, i: (0, 0))]
    args += [cprev, mem, w_dw, b_dw.reshape(1, c3), cn_g.reshape(1, c3), cn_b.reshape(1, c3)]

    return pl.pallas_call(
        functools.partial(_a_mix_kernel, tt=tt, tr=tr, has_halo=has_halo),
        grid=(b, nt),
        in_specs=in_specs,
        out_specs=[pl.BlockSpec((1, tt, D_MODEL), lambda bi, i: (bi, i, 0)),
                   pl.BlockSpec((1, CONV_WIDTH - 1, c3), lambda bi, i: (bi, 0, 0))],
        out_shape=[jax.ShapeDtypeStruct((b, t, D_MODEL), BF16),
                   jax.ShapeDtypeStruct((b, CONV_WIDTH - 1, c3), F32)],
        scratch_shapes=[pltpu.VMEM((c3 // HEAD_DIM, tt + HALO, HEAD_DIM), F32),
                        pltpu.VMEM((CONV_WIDTH, 8, c3), F32),
                        pltpu.VMEM((tt, c3), F32)],
        compiler_params=_params("arbitrary", "arbitrary"),
        name="a_mix",
    )(*args)


def _rows(ref, start, size, stride):
    if stride == 1:
        return ref[0, start:start + size, :]
    return ref[0, pl.ds(start, size, stride=stride), :]


def _b_attn_kernel(q0, q1, q2, k0, k1, k2, v0, v1, v2, hk0, hk1, hk2, hv0, hv1, hv2, bias_ref, gate_ref,
                   o_ref, num_scr, m_scr, s_scr):
    first = jnp.minimum(pl.program_id(2), 1)
    qs, ks, vs = (q0, q1, q2), (k0, k1, k2), (v0, v1, v2)
    hks, hvs = (hk0, hk1, hk2), (hv0, hv1, hv2)
    dn = (((1,), (1,)), ((), ()))
    for g, (_, d) in enumerate(B_GROUPS):
        for r in range(d):
            for s in range(ATTN_TOKENS // (BAND * d)):
                start = r + d * BAND * s
                q = _rows(qs[g], start, BAND, d)
                if s == 0:
                    k = jnp.concatenate([_rows(hks[g], r, BAND, d), _rows(ks[g], r, BAND, d)], 0)
                    v = jnp.concatenate([_rows(hvs[g], r, BAND, d), _rows(vs[g], r, BAND, d)], 0)
                    bias = bias_ref[g, first, 0]
                else:
                    k = _rows(ks[g], start - d * BAND, 2 * BAND, d)
                    v = _rows(vs[g], start - d * BAND, 2 * BAND, d)
                    bias = bias_ref[g, 1, 0]
                logits = jax.lax.dot_general(q.astype(BF16), k.astype(BF16), dn, preferred_element_type=F32)
                logits = logits * SCALE + bias
                m = jnp.max(logits, -1, keepdims=True)
                e = jnp.exp(logits - m)
                ssum = jnp.sum(e, -1, keepdims=True)
                num = jnp.dot(e.astype(BF16), v.astype(BF16), preferred_element_type=F32)
                if d == 1:
                    dst = pl.ds(start, BAND)
                else:
                    dst = pl.ds(start, BAND, stride=d)
                num_scr[g, dst, :] = num
                m_scr[g, dst, :] = jnp.broadcast_to(m, (BAND, HEAD_DIM))
                s_scr[g, dst, :] = jnp.broadcast_to(ssum, (BAND, HEAD_DIM))

    def merge(c, carry):
        sl = pl.ds(pl.multiple_of(c * BAND, BAND), BAND)
        ms = [m_scr[g, sl, :] for g in range(N_GROUPS)]
        m_all = jnp.maximum(jnp.maximum(ms[0], ms[1]), ms[2])
        coef = [jnp.exp(m - m_all) for m in ms]
        num = coef[0] * num_scr[0, sl, :]
        den = coef[0] * s_scr[0, sl, :]
        for g in range(1, N_GROUPS):
            num = num + coef[g] * num_scr[g, sl, :]
            den = den + coef[g] * s_scr[g, sl, :]
        o_ref[0, sl, :] = (num / den * gate_ref[0, sl, :].astype(F32)).astype(BF16)
        return carry

    jax.lax.fori_loop(0, ATTN_TOKENS // BAND, merge, 0)


def _b_attn(q, gates, kv, band_bias):
    b, t, _ = q.shape
    tb = ATTN_TOKENS
    blk = (1, tb, HEAD_DIM)

    def cur(col0):
        return lambda bi, hi, i: (bi, i, col0 + hi)

    def halo_spec(d, col0):
        rows = BAND * d
        per = tb // rows
        return pl.BlockSpec((1, rows, HEAD_DIM), lambda bi, hi, i: (bi, jnp.maximum(i * per - 1, 0), col0 + hi))

    hpg = HEADS_PER_GROUP
    in_specs = ([pl.BlockSpec(blk, cur(hpg * g)) for g in range(N_GROUPS)]
                + [pl.BlockSpec(blk, cur(hpg * g)) for g in range(N_GROUPS)]
                + [pl.BlockSpec(blk, cur(B_HEADS + hpg * g)) for g in range(N_GROUPS)]
                + [halo_spec(d, hpg * g) for g, (_, d) in enumerate(B_GROUPS)]
                + [halo_spec(d, B_HEADS + hpg * g) for g, (_, d) in enumerate(B_GROUPS)]
                + [pl.BlockSpec((N_GROUPS, 2, 1, BAND, 2 * BAND), lambda bi, hi, i: (0, 0, hi, 0, 0)),
                   pl.BlockSpec(blk, cur(0))])
    return pl.pallas_call(
        _b_attn_kernel,
        grid=(b, hpg, t // tb),
        in_specs=in_specs,
        out_specs=pl.BlockSpec(blk, lambda bi, hi, i: (bi, i, hi)),
        out_shape=jax.ShapeDtypeStruct((b, t, B_OUT), BF16),
        scratch_shapes=[pltpu.VMEM((N_GROUPS, tb, HEAD_DIM), F32)] * 3,
        compiler_params=_params("parallel", "parallel", "arbitrary"),
        name="b_attn",
    )(q, q, q, kv, kv, kv, kv, kv, kv, kv, kv, kv, kv, kv, kv, band_bias, gates)


def _b_out_kernel(o_ref, qm_ref, gm_ref, mem_ref, w_ref, x_ref, g_ref, b_ref, y_ref, *rest, nb, tt):
    yb_ref, wb_ref = rest if len(rest) == 2 else (None, rest[0])

    @pl.when((pl.program_id(0) == 0) & (pl.program_id(1) == 0))
    def _():
        wb_ref[...] = w_ref[...].astype(BF16)

    stack = lambda parts: parts[0] if nb == 1 else jnp.concatenate(parts, 0)
    mbranch = stack([_mem_attn(qm_ref[bi], mem_ref, bi) * gm_ref[bi].astype(F32) for bi in range(nb)])
    if nb == 1:
        obranch = o_ref[0]
    else:
        obranch = jnp.concatenate([o_ref[bi].astype(F32) for bi in range(nb)], 0).astype(BF16)
    y = jnp.dot(obranch, wb_ref[0:B_OUT, :], preferred_element_type=F32)
    y = y + jnp.dot(mbranch.astype(BF16), wb_ref[B_OUT:B_OUT + MEM_WIDTH, :], preferred_element_type=F32)
    x = stack([x_ref[bi] for bi in range(nb)])
    out = _layernorm(ALPHA * x + y, g_ref[...], b_ref[...])
    for bi in range(nb):
        y_ref[bi] = out[bi * tt:(bi + 1) * tt]
        if yb_ref is not None:
            yb_ref[bi] = out[bi * tt:(bi + 1) * tt].astype(BF16)


def _b_out(o, gates, mem, layer, w, wl, x, g, b, emit_bf16):
    bsz, t, d = x.shape
    tt = min(t, 512)
    nb = max(1, min(bsz, 64 // tt))
    kk = w.shape[1]
    qcol = B_OUT // MEM_WIDTH
    row = pl.BlockSpec((nb, tt, d), lambda bi, i: (bi, i, 0))
    out_dtypes = [F32, BF16] if emit_bf16 else [F32]
    res = pl.pallas_call(
        functools.partial(_b_out_kernel, nb=nb, tt=tt),
        grid=(bsz // nb, t // tt),
        in_specs=[pl.BlockSpec((nb, tt, B_OUT), lambda bi, i: (bi, i, 0)),
                  pl.BlockSpec((nb, tt, MEM_WIDTH), lambda bi, i: (bi, i, qcol)),
                  pl.BlockSpec((nb, tt, MEM_WIDTH), lambda bi, i: (bi, i, qcol + 1)),
                  pl.BlockSpec((None, nb, N_MEM * KV_SLOTS, HEAD_DIM), lambda bi, i: (layer, bi, 0, 0)),
                  pl.BlockSpec((None, kk, d), lambda bi, i: (wl, 0, 0), pipeline_mode=pl.Buffered(1)),
                  row,
                  pl.BlockSpec((1, d), lambda bi, i: (0, 0)),
                  pl.BlockSpec((1, d), lambda bi, i: (0, 0))],
        out_specs=[row] * len(out_dtypes),
        out_shape=[jax.ShapeDtypeStruct((bsz, t, d), dt) for dt in out_dtypes],
        scratch_shapes=[pltpu.VMEM((kk, d), BF16)],
        compiler_params=_params("arbitrary", "arbitrary"),
        name="b_out",
    )(o, gates, gates, mem, w, x, g.reshape(1, d), b.reshape(1, d))
    return res if emit_bf16 else (res[0], None)


def _samp_attn_kernel(*refs, nq, widths, emit_state):
    ng = N_GROUPS
    q_ref, gate_ref = refs[0], refs[1]
    st_refs, new_refs = refs[2:2 + ng], refs[2 + ng:2 + 2 * ng]
    bs_refs, bn_refs = refs[2 + 2 * ng:2 + 3 * ng], refs[2 + 3 * ng:2 + 4 * ng]
    o_ref = refs[2 + 4 * ng]
    dn = (((1,), (1,)), ((), ()))
    for h in range(HEADS_PER_GROUP):
        ms, sums, nums = [], [], []
        for g in range(ng):
            lo = (g * HEADS_PER_GROUP + h) * HEAD_DIM
            q = q_ref[0, :, lo:lo + HEAD_DIM].astype(BF16)
            ks = _kv_rows(st_refs[g], (0,), h, widths[g]).astype(BF16)
            vs = _kv_rows(st_refs[g], (0,), HEADS_PER_GROUP + h, widths[g]).astype(BF16)
            kn = _kv_rows(new_refs[g], (0,), h, nq).astype(BF16)
            vn = _kv_rows(new_refs[g], (0,), HEADS_PER_GROUP + h, nq).astype(BF16)
            ls = jax.lax.dot_general(q, ks, dn, preferred_element_type=F32) * SCALE + bs_refs[g][h]
            ln = jax.lax.dot_general(q, kn, dn, preferred_element_type=F32) * SCALE + bn_refs[g][h]
            m = jnp.maximum(jnp.max(ls, -1, keepdims=True), jnp.max(ln, -1, keepdims=True))
            es = jnp.exp(ls - m)
            en = jnp.exp(ln - m)
            ms.append(m)
            sums.append(jnp.sum(es, -1, keepdims=True) + jnp.sum(en, -1, keepdims=True))
            nums.append(jnp.dot(es.astype(BF16), vs, preferred_element_type=F32)
                        + jnp.dot(en.astype(BF16), vn, preferred_element_type=F32))
        m_all = jnp.maximum(jnp.maximum(ms[0], ms[1]), ms[2])
        coef = [jnp.exp(m - m_all) for m in ms]
        num = coef[0] * nums[0]
        den = coef[0] * sums[0]
        for g in range(1, ng):
            num = num + coef[g] * nums[g]
            den = den + coef[g] * sums[g]
        lo = h * HEAD_DIM
        o_ref[0, :, lo:lo + HEAD_DIM] = (num / den * gate_ref[0, :, lo:lo + HEAD_DIM].astype(F32)).astype(BF16)

    if emit_state:
        for g in range(ng):
            out = refs[3 + 4 * ng + g]
            rows, shift = widths[g] * KV_SLOTS, nq * KV_SLOTS
            out[0, 0:rows - shift, :] = st_refs[g][0, shift:rows, :]
            out[0, rows - shift:rows, :] = new_refs[g][0]


def _samp_attn(q, gates, states, news, bias_state, bias_new, emit_state):
    b, nq, _ = q.shape
    widths = tuple(s.shape[1] // KV_SLOTS for s in states)
    whole = lambda a: pl.BlockSpec((1,) + a.shape[1:], lambda i: (i, 0, 0))
    const = lambda a: pl.BlockSpec(a.shape, lambda i: (0, 0, 0))
    out_spec = pl.BlockSpec((1, nq, GROUP_WIDTH), lambda i: (i, 0, 0))
    out_shape = jax.ShapeDtypeStruct((b, nq, GROUP_WIDTH), BF16)
    res = pl.pallas_call(
        functools.partial(_samp_attn_kernel, nq=nq, widths=widths, emit_state=emit_state),
        grid=(b,),
        in_specs=([pl.BlockSpec((1, nq, MIX_WIDTH), lambda i: (i, 0, 0)),
                   pl.BlockSpec((1, nq, GROUP_WIDTH), lambda i: (i, 0, 0))]
                  + [whole(s) for s in states] + [whole(n) for n in news]
                  + [const(a) for a in bias_state] + [const(a) for a in bias_new]),
        out_specs=[out_spec] + ([whole(s) for s in states] if emit_state else []),
        out_shape=[out_shape] + ([jax.ShapeDtypeStruct(s.shape, s.dtype) for s in states] if emit_state else []),
        compiler_params=_params("arbitrary"),
        name="samp_attn",
    )(q, gates, *states, *news, *bias_state, *bias_new)
    return res[0], list(res[1:])


def _t5_bucket(dist):
    max_exact = REL_BUCKETS // 2
    safe = jnp.maximum(dist, 1).astype(F32)
    large = max_exact + (jnp.log(safe / max_exact) / math.log(REL_MAX_DIST / max_exact)
                         * (REL_BUCKETS - max_exact)).astype(jnp.int32)
    large = jnp.minimum(large, REL_BUCKETS - 1)
    return jnp.where(dist < max_exact, dist, large)


def _group_bias(rel_bias, g):
    w, d = B_GROUPS[g]
    dist = d * jnp.arange(w // d + 1, dtype=jnp.int32)
    tab = rel_bias[_t5_bucket(dist)]
    return tab[:, g * HEADS_PER_GROUP:(g + 1) * HEADS_PER_GROUP].T.astype(F32)


def _toeplitz(vec, rows, cols):
    hh, p = vec.shape
    flat = jnp.tile(vec, (1, rows))[:, :rows * (p - 1)]
    return flat.reshape(hh, rows, p - 1)[:, :, :cols]


def _band_bias(bias):
    period = 3 * BAND
    vec = jnp.full((bias.shape[0], period), NEG_INF, F32)
    vec = jax.lax.dynamic_update_slice(vec, bias[:, ::-1], (0, 0))
    mat = _toeplitz(vec, BAND, 2 * BAND)
    c = jnp.arange(2 * BAND, dtype=jnp.int32)[None, None, :]
    first = jnp.where(c >= BAND, mat, NEG_INF)
    return jnp.stack([first, mat])


def _sample_bias(bias, w, d, nq):
    hh = bias.shape[0]
    neg = jnp.full_like(bias, NEG_INF)
    pad = jnp.full((hh, nq), NEG_INF, F32)
    rev_dist = jnp.stack([neg] * (d - 1) + [bias[:, ::-1]], -1).reshape(hh, -1)
    bs = _toeplitz(jnp.concatenate([rev_dist[:, d - 1:d - 1 + w], pad], 1), nq, w)
    bn = _toeplitz(jnp.concatenate([bias[:, 0:1], pad, rev_dist[:, w + d - nq:w + d - 1]], 1), nq, nq)
    return bs, bn


def _trunk(x, conv_prev, mem_kv, kv_state, a_w_in, a_w_dw, a_b_dw, a_cn_g, a_cn_b, a_w_out,
           b_w_in, b_w_out, w_kv_shared, rel_bias, ln_g, ln_b):
    b, t, d_model = x.shape
    m = b * t
    x2 = x.reshape(m, d_model)
    xb = x2
    new_conv = []
    for l in range(N_A_LAYERS):
        u, gates = _a_proj(xb, a_w_in, l)
        cm, nc = _a_mix(u.reshape(b, t, -1), gates.reshape(b, t, -1), conv_prev[l], mem_kv, l,
                        a_w_dw[l], a_b_dw[l], a_cn_g[l], a_cn_b[l])
        new_conv.append(nc)
        x2, xb = _out_ln(cm.reshape(m, d_model), a_w_out, l, x2, ln_g[l], ln_b[l])

    kv = _mm(xb, w_kv_shared).reshape(b, t, 2 * MIX_WIDTH)
    biases = [_group_bias(rel_bias, g) for g in range(N_GROUPS)]
    kv_new = [_kv_window(kv, g, min(w, t)) for g, (w, _) in enumerate(B_GROUPS)]
    if kv_state is None:
        new_bufs = kv_new
        band_bias = jnp.stack([_band_bias(bi) for bi in biases])
    else:
        bias_state, bias_new = zip(*[_sample_bias(bi, w, d, t) for bi, (w, d) in zip(biases, B_GROUPS)])

    x3 = x2.reshape(b, t, d_model)
    for i in range(DEPTH - N_A_LAYERS):
        l = N_A_LAYERS + i
        q, gates = _b_proj(xb.reshape(m, d_model), b_w_in, i)
        q, gates = q.reshape(b, t, -1), gates.reshape(b, t, -1)
        if kv_state is None:
            o = _b_attn(q, gates, kv, band_bias)
        else:
            o, updated = _samp_attn(q, gates, kv_state, kv_new, bias_state, bias_new, emit_state=(i == 0))
            if i == 0:
                new_bufs = updated
        x3, xb = _b_out(o, gates, mem_kv, l, b_w_out, i, x3, ln_g[l], ln_b[l], emit_bf16=l < DEPTH - 1)
    new_bufs = [nb.reshape(b, -1, 2, HEADS_PER_GROUP, HEAD_DIM) for nb in new_bufs]
    return x3, jnp.stack(new_conv), new_bufs


def kernel(x_prompt, x_sample, state_conv, state_kv_g0, state_kv_g1, state_kv_g2, cache_mem_kv, mem_prompt,
           a_w_in, a_w_dw, a_b_dw, a_cn_g, a_cn_b, a_w_out, b_w_in, b_w_out, w_kv_shared, w_mem_kv,
           rel_bias, ln_g, ln_b):
    bp = x_prompt.shape[0]
    bs = x_sample.shape[0]
    mem_p = _mem_proj(mem_prompt.reshape(bp * N_MEM, D_MODEL), w_mem_kv).reshape(
        DEPTH, bp, N_MEM * KV_SLOTS, HEAD_DIM)
    new_mem_kv = mem_p.reshape(DEPTH, bp, N_MEM, 2, MEM_HEADS, HEAD_DIM)
    conv_zero = jnp.zeros((N_A_LAYERS, bp, CONV_WIDTH - 1, CONV_CH), x_prompt.dtype)
    weights = (a_w_in, a_w_dw, a_b_dw, a_cn_g, a_cn_b, a_w_out, b_w_in, b_w_out, w_kv_shared, rel_bias, ln_g, ln_b)

    y_p, conv_p, bufs_p = _trunk(x_prompt, conv_zero, mem_p, None, *weights)
    kv_state = [s.reshape(bs, s.shape[1] * KV_SLOTS, HEAD_DIM) for s in (state_kv_g0, state_kv_g1, state_kv_g2)]
    mem_s = cache_mem_kv.reshape(DEPTH, bs, N_MEM * KV_SLOTS, HEAD_DIM)
    y_s, conv_s, bufs_s = _trunk(x_sample, state_conv, mem_s, kv_state, *weights)

    return (y_p, y_s, conv_p, conv_s, bufs_p[0], bufs_s[0], bufs_p[1], bufs_s[1], bufs_p[2], bufs_s[2], new_mem_kv)
```
